```python
import math
import jax, jax.numpy as jnp
from jax import lax
import numpy as np

D_MODEL = 1024
BATCH = 1
SEQ = 16384
DEPTH = 2

CHUNK = 64
Q_BLOCK = 128
ROPE_THETA = 10000.0
EPS = 1e-6

A_HEADS = 4
A_HEAD_DIM = 64
A_V_DIM = 2 * A_HEAD_DIM
B_HEADS = 8
B_KV_HEADS = 2
B_HEAD_DIM = 64
WINDOW = 128
WINDOW_CHUNKS = WINDOW // CHUNK

A_Q = A_HEADS * 2 * A_HEAD_DIM
A_K = A_HEADS * 2 * A_HEAD_DIM
A_V = A_HEADS * A_V_DIM
B_Q = B_HEADS * B_HEAD_DIM
B_K = B_KV_HEADS * B_HEAD_DIM
B_V = B_KV_HEADS * B_HEAD_DIM
L0_IN = A_Q + A_K + A_V + B_Q + B_K + B_V
L0_OUT = A_V + B_Q
L0_SPLIT = (A_Q, A_Q + A_K, A_Q + A_K + A_V, A_Q + A_K + A_V + B_Q,
            A_Q + A_K + A_V + B_Q + B_K)

C_HEADS = 16
C_Q_RANK = 256
C_KV_RANK = 128
C_NOPE = 64
C_ROPE = 32
C_V = 64
C_QK = C_NOPE + C_ROPE
C_IN = C_Q_RANK + C_KV_RANK + C_ROPE
C_OUT = C_HEADS * C_V

D_FF = 2816
N_EXPERTS = 8
TOP_K = 2
D_FF_EXPERT = 3584

kernel_name = "hybrid_diff_swa_mla_moe_trunk"


def rms_norm(x, g):
    xf = x.astype(jnp.float32)
    y = xf * lax.rsqrt(jnp.mean(xf * xf, axis=-1, keepdims=True) + EPS)
    return (y * g.astype(jnp.float32)).astype(x.dtype)


def rope_tables(seq, dim):
    pos = jnp.arange(seq, dtype=jnp.float32)
    inv = ROPE_THETA ** (-jnp.arange(0, dim, 2, dtype=jnp.float32) / dim)
    ang = pos[:, None] * inv[None, :]
    return jnp.cos(ang), jnp.sin(ang)


def apply_rope(x, cos, sin):
    shape = (1, cos.shape[0]) + (1,) * (x.ndim - 3) + (cos.shape[1],)
    c = cos.reshape(shape).astype(x.dtype)
    s = sin.reshape(shape).astype(x.dtype)
    x1, x2 = jnp.split(x, 2, axis=-1)
    return jnp.concatenate([x1 * c - x2 * s, x2 * c + x1 * s], axis=-1)


def to_blocks(x):
    nb = x.shape[1] // Q_BLOCK
    x = x.reshape((x.shape[0], nb, Q_BLOCK) + x.shape[2:])
    return jnp.moveaxis(x, 1, 0)


def from_blocks(y):
    y = jnp.moveaxis(y, 0, 1)
    return y.reshape((y.shape[0], y.shape[1] * y.shape[2]) + y.shape[3:])


def chunk_causal_mask(block_idx, seq):
    q_chunk = (block_idx * Q_BLOCK + jnp.arange(Q_BLOCK)) // CHUNK
    k_chunk = jnp.arange(seq) // CHUNK
    return k_chunk[None, :] <= q_chunk[:, None]


def masked_softmax(s, mask):
    return jax.nn.softmax(jnp.where(mask, s, -jnp.inf), axis=-1)


def diff_attention(q, k, v, lam):
    scale = A_HEAD_DIM ** -0.5
    seq = k.shape[1]
    nb = seq // Q_BLOCK

    def one_block(args):
        b, qb = args
        s = jnp.einsum('bqhmd,bkhmd->bhmqk', qb, k).astype(jnp.float32) * scale
        p = masked_softmax(s, chunk_causal_mask(b, seq))
        w = p[:, :, 0] - lam * p[:, :, 1]
        return jnp.einsum('bhqk,bkhd->bqhd', w.astype(v.dtype), v)

    return from_blocks(lax.map(one_block, (jnp.arange(nb), to_blocks(q))))


def swa_sink_attention(q, k, v, sinks):
    B, S, _, D = q.shape
    nb = S // Q_BLOCK
    G = B_HEADS // B_KV_HEADS
    scale = B_HEAD_DIM ** -0.5
    qb = q.reshape(B, nb, Q_BLOCK, B_KV_HEADS, G, D)

    def band(t):
        tp = jnp.pad(t, ((0, 0), (Q_BLOCK, 0), (0, 0), (0, 0)))
        tb = tp.reshape(B, nb + 1, Q_BLOCK, B_KV_HEADS, D)
        return jnp.concatenate([tb[:, :-1], tb[:, 1:]], axis=2)

    kb, vb = band(k), band(v)
    q_pos = jnp.arange(S).reshape(nb, Q_BLOCK)
    k_pos = jnp.arange(nb)[:, None] * Q_BLOCK - Q_BLOCK + jnp.arange(2 * Q_BLOCK)[None, :]
    qc = (q_pos // CHUNK)[:, :, None]
    kc = (k_pos // CHUNK)[:, None, :]
    mask = (k_pos >= 0)[:, None, :] & (kc <= qc) & (kc >= qc - WINDOW_CHUNKS)
    s = jnp.einsum('bnqhgd,bnkhd->bnhgqk', qb, kb).astype(jnp.float32) * scale
    s = jnp.where(mask[None, :, None, None], s, -jnp.inf)
    sink = sinks.astype(jnp.float32).reshape(1, 1, B_KV_HEADS, G, 1, 1)
    m = jnp.maximum(jnp.max(s, axis=-1, keepdims=True), sink)
    e = jnp.exp(s - m)
    p = e / (jnp.sum(e, axis=-1, keepdims=True) + jnp.exp(sink - m))
    out = jnp.einsum('bnhgqk,bnkhd->bnqhgd', p.astype(v.dtype), vb)
    return out.reshape(B, S, B_HEADS, D)


def dense_chunk_causal_attention(q, k, v, scale):
    seq = k.shape[1]
    nb = seq // Q_BLOCK

    def one_block(args):
        b, qb = args
        s = jnp.einsum('bqhd,bkhd->bhqk', qb, k).astype(jnp.float32) * scale
        p = masked_softmax(s, chunk_causal_mask(b, seq))
        return jnp.einsum('bhqk,bkhd->bqhd', p.astype(v.dtype), v)

    return from_blocks(lax.map(one_block, (jnp.arange(nb), to_blocks(q))))


def hybrid_a_b_mixer(h, w_in, a_q_norm, a_k_norm, a_lambda, a_subln,
                     b_q_norm, b_k_norm, b_sinks, w_out, cos, sin, layer):
    B, S, _ = h.shape
    z = h @ w_in
    aq, ak, av, bq, bk, bv = jnp.split(z, L0_SPLIT, axis=-1)
    aq = apply_rope(rms_norm(aq.reshape(B, S, A_HEADS, 2, A_HEAD_DIM), a_q_norm), cos, sin)
    ak = apply_rope(rms_norm(ak.reshape(B, S, A_HEADS, 2, A_HEAD_DIM), a_k_norm), cos, sin)
    av = av.reshape(B, S, A_HEADS, A_V_DIM)
    lam_init = 0.8 - 0.6 * math.exp(-0.3 * layer)
    lf = a_lambda.astype(jnp.float32)
    lam = jnp.exp(jnp.sum(lf[0] * lf[1])) - jnp.exp(jnp.sum(lf[2] * lf[3])) + lam_init
    a_out = diff_attention(aq, ak, av, lam)
    a_out = rms_norm(a_out, a_subln) * (1.0 - lam_init)
    bq = apply_rope(rms_norm(bq.reshape(B, S, B_HEADS, B_HEAD_DIM), b_q_norm), cos, sin)
    bk = apply_rope(rms_norm(bk.reshape(B, S, B_KV_HEADS, B_HEAD_DIM), b_k_norm), cos, sin)
    bv = bv.reshape(B, S, B_KV_HEADS, B_HEAD_DIM)
    b_out = swa_sink_attention(bq, bk, bv, b_sinks)
    merged = jnp.concatenate([a_out.reshape(B, S, A_V), b_out.reshape(B, S, B_Q)], axis=-1)
    return merged @ w_out


def mla_mixer(h, w_in, q_lora_norm, kv_lora_norm, w_uq, w_ukv, q_norm, k_norm,
              w_out, cos, sin):
    B, S, _ = h.shape
    z = h @ w_in
    c_q, c_kv, k_rope = jnp.split(z, (C_Q_RANK, C_Q_RANK + C_KV_RANK), axis=-1)
    c_q = rms_norm(c_q, q_lora_norm)
    c_kv = rms_norm(c_kv, kv_lora_norm)
    q = (c_q @ w_uq).reshape(B, S, C_HEADS, C_QK)
    kv = (c_kv @ w_ukv).reshape(B, S, C_HEADS, C_NOPE + C_V)
    k_nope, v = jnp.split(kv, (C_NOPE,), axis=-1)
    k_r = jnp.broadcast_to(k_rope[:, :, None, :], (B, S, C_HEADS, C_ROPE))
    k = jnp.concatenate([k_nope, k_r], axis=-1)
    q = rms_norm(q, q_norm)
    k = rms_norm(k, k_norm)
    q = jnp.concatenate([q[..., :C_NOPE], apply_rope(q[..., C_NOPE:], cos, sin)], axis=-1)
    k = jnp.concatenate([k[..., :C_NOPE], apply_rope(k[..., C_NOPE:], cos, sin)], axis=-1)
    out = dense_chunk_causal_attention(q, k, v, C_QK ** -0.5)
    return out.reshape(B, S, C_OUT) @ w_out


def swiglu(h, w_gu, w_down):
    g, u = jnp.split(h @ w_gu, 2, axis=-1)
    return (jax.nn.silu(g) * u) @ w_down


def moe_swiglu(h, w_router, w_gu, w_down):
    logits = (h @ w_router).astype(jnp.float32)
    vals, idx = lax.top_k(logits, TOP_K)
    gates = jax.nn.softmax(vals, axis=-1)
    combine = jnp.sum(jax.nn.one_hot(idx, N_EXPERTS, dtype=jnp.float32) * gates[..., None], axis=-2)
    y = jnp.zeros_like(h)
    for e in range(N_EXPERTS):
        y = y + combine[..., e:e + 1].astype(h.dtype) * swiglu(h, w_gu[e], w_down[e])
    return y


def setup_inputs(seed: int = 0) -> dict:
    key = jax.random.key(seed)
    ks = jax.random.split(key, 32)

    def w(k, shape, fan_in):
        return jax.random.normal(k, shape, jnp.float32) * fan_in ** -0.5

    def gain(k, n):
        return 1.0 + 0.05 * jax.random.normal(k, (n,), jnp.float32)

    return {
        "x": jax.random.normal(ks[0], (BATCH, SEQ, D_MODEL), jnp.float32),
        "l0_norm_attn": gain(ks[1], D_MODEL),
        "l0_w_in": w(ks[2], (D_MODEL, L0_IN), D_MODEL),
        "l0_a_q_norm": gain(ks[3], A_HEAD_DIM),
        "l0_a_k_norm": gain(ks[4], A_HEAD_DIM),
        "l0_a_lambda": 0.1 * jax.random.normal(ks[5], (4, A_HEAD_DIM), jnp.float32),
        "l0_a_subln": gain(ks[6], A_V_DIM),
        "l0_b_q_norm": gain(ks[7], B_HEAD_DIM),
        "l0_b_k_norm": gain(ks[8], B_HEAD_DIM),
        "l0_b_sinks": 0.5 * jax.random.normal(ks[9], (B_HEADS,), jnp.float32),
        "l0_w_out": w(ks[10], (L0_OUT, D_MODEL), L0_OUT),
        "l0_norm_ffn": gain(ks[11], D_MODEL),
        "l0_ffn_w_gu": w(ks[12], (D_MODEL, 2 * D_FF), D_MODEL),
        "l0_ffn_w_down": w(ks[13], (D_FF, D_MODEL), D_FF),
        "l1_norm_attn": gain(ks[14], D_MODEL),
        "l1_c_w_in": w(ks[15], (D_MODEL, C_IN), D_MODEL),
        "l1_c_q_lora_norm": gain(ks[16], C_Q_RANK),
        "l1_c_kv_lora_norm": gain(ks[17], C_KV_RANK),
        "l1_c_w_uq": w(ks[18], (C_Q_RANK, C_HEADS * C_QK), C_Q_RANK),
        "l1_c_w_ukv": w(ks[19], (C_KV_RANK, C_HEADS * (C_NOPE + C_V)), C_KV_RANK),
        "l1_c_q_norm": gain(ks[20], C_QK),
        "l1_c_k_norm": gain(ks[21], C_QK),
        "l1_c_w_out": w(ks[22], (C_OUT, D_MODEL), C_OUT),
        "l1_norm_ffn": gain(ks[23], D_MODEL),
        "l1_router": w(ks[24], (D_MODEL, N_EXPERTS), D_MODEL),
        "l1_exp_w_gu": w(ks[25], (N_EXPERTS, D_MODEL, 2 * D_FF_EXPERT), D_MODEL),
        "l1_exp_w_down": w(ks[26], (N_EXPERTS, D_FF_EXPERT, D_MODEL), D_FF_EXPERT),
    }


def reference(x, l0_norm_attn, l0_w_in, l0_a_q_norm, l0_a_k_norm, l0_a_lambda, l0_a_subln,
              l0_b_q_norm, l0_b_k_norm, l0_b_sinks, l0_w_out, l0_norm_ffn, l0_ffn_w_gu,
              l0_ffn_w_down, l1_norm_attn, l1_c_w_in, l1_c_q_lora_norm, l1_c_kv_lora_norm,
              l1_c_w_uq, l1_c_w_ukv, l1_c_q_norm, l1_c_k_norm, l1_c_w_out, l1_norm_ffn,
              l1_router, l1_exp_w_gu, l1_exp_w_down):
    seq = x.shape[1]
    cos64, sin64 = rope_tables(seq, A_HEAD_DIM)
    cos32, sin32 = rope_tables(seq, C_ROPE)
    for layer in range(DEPTH):
        if layer % 2 == 0:
            h = rms_norm(x, l0_norm_attn)
            x = x + hybrid_a_b_mixer(h, l0_w_in, l0_a_q_norm, l0_a_k_norm, l0_a_lambda,
                                     l0_a_subln, l0_b_q_norm, l0_b_k_norm, l0_b_sinks,
                                     l0_w_out, cos64, sin64, layer)
            h = rms_norm(x, l0_norm_ffn)
            x = x + swiglu(h, l0_ffn_w_gu, l0_ffn_w_down)
        else:
            h = rms_norm(x, l1_norm_attn)
            x = x + mla_mixer(h, l1_c_w_in, l1_c_q_lora_norm, l1_c_kv_lora_norm, l1_c_w_uq,
                              l1_c_w_ukv, l1_c_q_norm, l1_c_k_norm, l1_c_w_out, cos32, sin32)
            h = rms_norm(x, l1_norm_ffn)
            x = x + moe_swiglu(h, l1_router, l1_exp_w_gu, l1_exp_w_down)
    return x
```

```python
import functools
import math

import jax
import jax.numpy as jnp
from jax import lax
from jax.experimental import pallas as pl
from jax.experimental.pallas import tpu as pltpu

F32 = jnp.float32
BF16 = jnp.bfloat16

D_MODEL = 1024
CHUNK = 64
ROPE_THETA = 10000.0
EPS = 1e-6
LANES = 128
ROW_TILES = D_MODEL // LANES

A_HEADS = 4
A_HEAD_DIM = 64
B_HEADS = 8
B_KV_HEADS = 2
B_HEAD_DIM = 64
A_W = A_HEADS * 2 * A_HEAD_DIM
B_QW = B_HEADS * B_HEAD_DIM
B_KW = B_KV_HEADS * B_HEAD_DIM

C_HEADS = 16
C_Q_RANK = 256
C_KV_RANK = 128
C_NOPE = 64
C_ROPE = 32
C_V = 64
C_QK = C_NOPE + C_ROPE

D_FF = 2816
N_EXPERTS = 8
D_FF_EXPERT = 3584

NEG_BIG = -1e30
VMEM_LIMIT = 56 * 1024 * 1024


def _cparams(sem):
    return pltpu.CompilerParams(dimension_semantics=sem, vmem_limit_bytes=VMEM_LIMIT)


def _dot(a, b):
    return jnp.dot(a, b, preferred_element_type=F32)


def _dot_nt(a, b):
    return lax.dot_general(a, b, (((1,), (1,)), ((), ())), preferred_element_type=F32)


def _rms(x, g):
    ms = jnp.mean(x * x, axis=-1, keepdims=True)
    return x * lax.rsqrt(ms + EPS) * g


def _l0_in_kernel(x_ref, g_ref, w_ref, gains_ref, ones_ref, cos_ref, sin_ref,
                  aq_o, ak_o, av_o, bq_o, bk_o, bv_o):
    h = _rms(x_ref[...], g_ref[...]).astype(BF16)
    z = _dot(h, w_ref[...])
    cos = cos_ref[...]
    sin = sin_ref[...]
    ones_blk = ones_ref[...]
    lane = lax.broadcasted_iota(jnp.int32, (1, LANES), 1)
    first_half = (lane % 64) < 32

    def norm_rope(zs, gain, scale):
        ss = _dot((zs * zs).astype(BF16), ones_blk)
        r = lax.rsqrt(ss * (1.0 / 64.0) + EPS) * scale
        y = zs * gain
        sw = jnp.where(first_half, pltpu.roll(y, 96, 1), pltpu.roll(y, 32, 1))
        return ((y * cos + sw * sin) * r).astype(BF16)

    col = 0
    gcol = 0
    scale = A_HEAD_DIM ** -0.5
    outs = {"aq": aq_o, "ak": ak_o, "bq": bq_o, "bk": bk_o}
    for name, nslab in (("aq", 4), ("ak", 4), ("av", 4), ("bq", 4), ("bk", 2), ("bv", 2)):
        if name == "av":
            av_o[...] = z[:, col:col + 4 * LANES].astype(BF16)
            col += 4 * LANES
            continue
        if name == "bv":
            bv_o[...] = z[:, col:col + 2 * LANES].astype(BF16)
            col += 2 * LANES
            continue
        sc = scale if name in ("aq", "bq") else 1.0
        gain = gains_ref[gcol:gcol + 1, :]
        gcol += 1
        for s in range(nslab):
            zs = z[:, col:col + LANES]
            outs[name][:, s * LANES:(s + 1) * LANES] = norm_rope(zs, gain, sc)
            col += LANES


def _l0_in(x2d, g, w_in_p, gains, ones_blk, cos_t, sin_t, tm):
    S = x2d.shape[0]
    nw = w_in_p.shape[1]
    row = lambda i: (i, 0)
    const = lambda i: (0, 0)
    out_shapes = (
        jax.ShapeDtypeStruct((S, A_W), BF16), jax.ShapeDtypeStruct((S, A_W), BF16),
        jax.ShapeDtypeStruct((S, A_W), BF16), jax.ShapeDtypeStruct((S, B_QW), BF16),
        jax.ShapeDtypeStruct((S, 2 * B_KW), BF16), jax.ShapeDtypeStruct((S, 2 * B_KW), BF16))
    return pl.pallas_call(
        _l0_in_kernel,
        grid=(S // tm,),
        in_specs=[
            pl.BlockSpec((tm, D_MODEL), row),
            pl.BlockSpec((1, D_MODEL), const),
            pl.BlockSpec((D_MODEL, nw), const),
            pl.BlockSpec((8, LANES), const),
            pl.BlockSpec((LANES, LANES), const),
            pl.BlockSpec((tm, LANES), row),
            pl.BlockSpec((tm, LANES), row),
        ],
        out_specs=[
            pl.BlockSpec((tm, A_W), row), pl.BlockSpec((tm, A_W), row),
            pl.BlockSpec((tm, A_W), row), pl.BlockSpec((tm, B_QW), row),
            pl.BlockSpec((tm, 2 * B_KW), row), pl.BlockSpec((tm, 2 * B_KW), row)],
        out_shape=out_shapes,
        compiler_params=_cparams(("arbitrary",)),
        name="l0_in_proj",
    )(x2d, g, w_in_p, gains, ones_blk, cos_t, sin_t)


def _flash_step(q, k, v, m, l, acc, mask):
    s = _dot_nt(q, k)
    if mask is not None:
        s = jnp.where(mask, s, NEG_BIG)
    m_new = jnp.maximum(m, jnp.max(s, axis=1, keepdims=True))
    alpha = jnp.exp(m - m_new)
    p = jnp.exp(s - m_new)
    l_new = alpha * l + jnp.sum(p, axis=1, keepdims=True)
    acc_new = alpha * acc + _dot(p.astype(BF16), v)
    return m_new, l_new, acc_new


def _diag_mask(t):
    r = lax.broadcasted_iota(jnp.int32, (t, t), 0) // CHUNK
    c = lax.broadcasted_iota(jnp.int32, (t, t), 1) // CHUNK
    return c <= r


def _a_attn_kernel(lam_ref, q_ref, k_ref, v_ref, subln_ref, o_ref, *, t, lam_init):
    i = pl.program_id(1)
    lane = lax.broadcasted_iota(jnp.int32, (1, LANES), 1)
    q = q_ref[...]
    zero = jnp.zeros_like(q)
    q1 = jnp.where(lane < 64, q, zero)
    q2 = jnp.where(lane >= 64, q, zero)

    def init():
        return (jnp.full((t, 1), NEG_BIG, F32), jnp.zeros((t, 1), F32),
                jnp.zeros((t, LANES), F32))

    def body(j, carry):
        c1, c2 = carry
        off = pl.multiple_of(j * t, t)
        k = k_ref[pl.ds(off, t), :]
        v = v_ref[pl.ds(off, t), :]
        return _flash_step(q1, k, v, *c1, None), _flash_step(q2, k, v, *c2, None)

    c1, c2 = lax.fori_loop(0, i, body, (init(), init()))
    off = pl.multiple_of(i * t, t)
    k = k_ref[pl.ds(off, t), :]
    v = v_ref[pl.ds(off, t), :]
    mask = _diag_mask(t)
    _, l1, a1 = _flash_step(q1, k, v, *c1, mask)
    _, l2, a2 = _flash_step(q2, k, v, *c2, mask)

    lf = lam_ref[...]
    lam = (jnp.exp(jnp.sum(lf[0:1] * lf[1:2], axis=1, keepdims=True))
           - jnp.exp(jnp.sum(lf[2:3] * lf[3:4], axis=1, keepdims=True)) + lam_init)
    out = a1 / l1 - lam * (a2 / l2)
    out = _rms(out, subln_ref[...]) * (1.0 - lam_init)
    o_ref[...] = out.astype(BF16)


def _a_attn(lam_p, aq, ak, av, subln, t, lam_init):
    S = aq.shape[0]
    return pl.pallas_call(
        functools.partial(_a_attn_kernel, t=t, lam_init=lam_init),
        grid=(A_HEADS, S // t),
        in_specs=[
            pl.BlockSpec((4, A_HEAD_DIM), lambda h, i: (0, 0)),
            pl.BlockSpec((t, LANES), lambda h, i: (i, h)),
            pl.BlockSpec((S, LANES), lambda h, i: (0, h)),
            pl.BlockSpec((S, LANES), lambda h, i: (0, h)),
            pl.BlockSpec((1, LANES), lambda h, i: (0, 0)),
        ],
        out_specs=pl.BlockSpec((t, LANES), lambda h, i: (i, h)),
        out_shape=jax.ShapeDtypeStruct((S, A_W), BF16),
        compiler_params=_cparams(("arbitrary", "arbitrary")),
        name="a_diff_attn",
    )(lam_p, aq, ak, av, subln)


B_BLK = 128


def _b_attn_kernel(sink_ref, q_ref, kp_ref, kc_ref, vp_ref, vc_ref, o_ref):
    i = pl.program_id(0)
    lane = lax.broadcasted_iota(jnp.int32, (1, LANES), 1)
    lo = lane < 64
    r = lax.broadcasted_iota(jnp.int32, (B_BLK, 2 * B_BLK), 0) // CHUNK
    c_idx = lax.broadcasted_iota(jnp.int32, (B_BLK, 2 * B_BLK), 1)
    c = c_idx // CHUNK
    mask = (c >= r) & (c <= r + 2) & ((c_idx >= B_BLK) | (i > 0))
    for slab in range(B_HEADS // 2):
        g = slab // 2
        k = jnp.concatenate([kp_ref[:, g * LANES:(g + 1) * LANES],
                             kc_ref[:, g * LANES:(g + 1) * LANES]], axis=0)
        v = jnp.concatenate([vp_ref[:, g * LANES:(g + 1) * LANES],
                             vc_ref[:, g * LANES:(g + 1) * LANES]], axis=0)
        qs = q_ref[:, slab * LANES:(slab + 1) * LANES]
        zq = jnp.zeros_like(qs)
        zv = jnp.zeros_like(v)
        out = jnp.zeros((B_BLK, LANES), F32)
        for half in range(2):
            sel = lo if half == 0 else jnp.logical_not(lo)
            sink = sink_ref[2 * slab + half]
            s = _dot_nt(jnp.where(sel, qs, zq), k)
            s = jnp.where(mask, s, NEG_BIG)
            m = jnp.maximum(jnp.max(s, axis=1, keepdims=True), sink)
            e = jnp.exp(s - m)
            denom = jnp.sum(e, axis=1, keepdims=True) + jnp.exp(sink - m)
            p = (e / denom).astype(BF16)
            out = out + _dot(p, jnp.where(sel, v, zv))
        o_ref[:, slab * LANES:(slab + 1) * LANES] = out.astype(BF16)


def _b_attn(sinks, bq, bk, bv):
    S = bq.shape[0]
    prev = lambda i: (jnp.maximum(i - 1, 0), 0)
    cur = lambda i: (i, 0)
    return pl.pallas_call(
        _b_attn_kernel,
        grid=(S // B_BLK,),
        in_specs=[
            pl.BlockSpec(memory_space=pltpu.SMEM),
            pl.BlockSpec((B_BLK, B_QW), cur),
            pl.BlockSpec((B_BLK, 2 * B_KW), prev),
            pl.BlockSpec((B_BLK, 2 * B_KW), cur),
            pl.BlockSpec((B_BLK, 2 * B_KW), prev),
            pl.BlockSpec((B_BLK, 2 * B_KW), cur),
        ],
        out_specs=pl.BlockSpec((B_BLK, B_QW), cur),
        out_shape=jax.ShapeDtypeStruct((S, B_QW), BF16),
        compiler_params=_cparams(("arbitrary",)),
        name="b_swa_attn",
    )(sinks, bq, bk, bk, bv, bv)


def _l0_ffn_kernel(a_ref, b_ref, x_ref, woa_ref, wob_ref, g_ref, wg_ref, wu_ref, wd_ref,
                   o_ref, h_sc, acc_sc):
    j = pl.program_id(1)

    @pl.when(j == 0)
    def _():
        x1 = x_ref[...] + _dot(a_ref[...], woa_ref[...]) + _dot(b_ref[...], wob_ref[...])
        acc_sc[...] = x1
        h_sc[...] = _rms(x1, g_ref[...]).astype(BF16)

    h = h_sc[...]
    gate = _dot(h, wg_ref[...])
    up = _dot(h, wu_ref[...])
    act = (gate * jax.nn.sigmoid(gate) * up).astype(BF16)
    acc_sc[...] += _dot(act, wd_ref[...])

    @pl.when(j == pl.num_programs(1) - 1)
    def _():
        o_ref[...] = acc_sc[...]


def _l0_ffn(a_out, b_out, x2d, wo_a, wo_b, g, w_gu, w_down, tm, tf):
    S = x2d.shape[0]
    nf = D_FF // tf
    row = lambda i, j: (i, 0)
    const = lambda i, j: (0, 0)
    return pl.pallas_call(
        _l0_ffn_kernel,
        grid=(S // tm, nf),
        in_specs=[
            pl.BlockSpec((tm, A_W), row),
            pl.BlockSpec((tm, B_QW), row),
            pl.BlockSpec((tm, D_MODEL), row),
            pl.BlockSpec((A_W, D_MODEL), const),
            pl.BlockSpec((B_QW, D_MODEL), const),
            pl.BlockSpec((1, D_MODEL), const),
            pl.BlockSpec((D_MODEL, tf), lambda i, j: (0, j)),
            pl.BlockSpec((D_MODEL, tf), lambda i, j: (0, nf + j)),
            pl.BlockSpec((tf, D_MODEL), lambda i, j: (j, 0)),
        ],
        out_specs=pl.BlockSpec((tm, D_MODEL), row),
        out_shape=jax.ShapeDtypeStruct((S, D_MODEL), F32),
        scratch_shapes=[pltpu.VMEM((tm, D_MODEL), BF16), pltpu.VMEM((tm, D_MODEL), F32)],
        compiler_params=_cparams(("arbitrary", "arbitrary")),
        name="l0_out_ffn",
    )(a_out, b_out, x2d, wo_a, wo_b, g, w_gu, w_gu, w_down)


def _c_in_kernel(x_ref, g_ref, wq_ref, wkv_ref, wkr_ref, wkrs_ref, gql_ref, gkvl_ref,
                 wuq_ref, wuqs_ref, wuk_ref, wuv_ref, qg_ref, qgs_ref, kg_ref, kgs_ref,
                 ones_ref, cos_ref, sin_ref, q_o, k_o, v_o):
    h = _rms(x_ref[...], g_ref[...]).astype(BF16)
    cq = _rms(_dot(h, wq_ref[...]), gql_ref[...]).astype(BF16)
    ckv = _rms(_dot(h, wkv_ref[...]), gkvl_ref[...]).astype(BF16)
    kr = _dot(h, wkr_ref[...])
    krs = _dot(h, wkrs_ref[...])
    q = _dot(cq, wuq_ref[...])
    qs = _dot(cq, wuqs_ref[...])
    kn = _dot(ckv, wuk_ref[...])
    v_o[...] = _dot(ckv, wuv_ref[...]).astype(BF16)

    cos = cos_ref[...]
    sin = sin_ref[...]
    ones_blk = ones_ref[...]
    qg, qgs, kg, kgs = qg_ref[...], qgs_ref[...], kg_ref[...], kgs_ref[...]
    ss_kr = _dot((kr * kr).astype(BF16), ones_blk)
    kr_roped = kr * kg * cos + krs * kgs * sin
    scale = C_QK ** -0.5
    inv = 1.0 / C_QK
    for hd in range(C_HEADS):
        sl = slice(hd * LANES, (hd + 1) * LANES)
        qh = q[:, sl]
        r = lax.rsqrt(_dot((qh * qh).astype(BF16), ones_blk) * inv + EPS) * scale
        q_o[:, sl] = ((qh * qg * cos + qs[:, sl] * qgs * sin) * r).astype(BF16)
        kh = kn[:, sl]
        rk = lax.rsqrt((_dot((kh * kh).astype(BF16), ones_blk) + ss_kr) * inv + EPS)
        k_o[:, sl] = ((kh * kg + kr_roped) * rk).astype(BF16)


def _c_in(x2d, g, wq, wkv, wkr, wkrs, gql, gkvl, wuq, wuqs, wuk, wuv, qg, qgs, kg, kgs,
          ones_blk, cos_t, sin_t, tm):
    S = x2d.shape[0]
    row = lambda i: (i, 0)
    const = lambda i: (0, 0)
    full = lambda a: pl.BlockSpec(a.shape, const)
    W = C_HEADS * LANES
    return pl.pallas_call(
        _c_in_kernel,
        grid=(S // tm,),
        in_specs=[pl.BlockSpec((tm, D_MODEL), row), full(g), full(wq), full(wkv), full(wkr),
                  full(wkrs), full(gql), full(gkvl), full(wuq), full(wuqs), full(wuk),
                  full(wuv), full(qg), full(qgs), full(kg), full(kgs), full(ones_blk),
                  pl.BlockSpec((tm, LANES), row), pl.BlockSpec((tm, LANES), row)],
        out_specs=[pl.BlockSpec((tm, W), row), pl.BlockSpec((tm, W), row),
                   pl.BlockSpec((tm, C_HEADS * C_V), row)],
        out_shape=(jax.ShapeDtypeStruct((S, W), BF16), jax.ShapeDtypeStruct((S, W), BF16),
                   jax.ShapeDtypeStruct((S, C_HEADS * C_V), BF16)),
        compiler_params=_cparams(("arbitrary",)),
        name="c_in_proj",
    )(x2d, g, wq, wkv, wkr, wkrs, gql, gkvl, wuq, wuqs, wuk, wuv, qg, qgs, kg, kgs,
      ones_blk, cos_t, sin_t)


def _c_attn_kernel(q_ref, k_ref, v_ref, o_ref, *, t):
    i = pl.program_id(1)
    lane = lax.broadcasted_iota(jnp.int32, (1, LANES), 1)
    qa = q_ref[:, :LANES]
    qb = q_ref[:, LANES:]

    def init():
        return (jnp.full((t, 1), NEG_BIG, F32), jnp.zeros((t, 1), F32),
                jnp.zeros((t, LANES), F32))

    def load(off):
        kk = k_ref[pl.ds(off, t), :]
        return kk[:, :LANES], kk[:, LANES:], v_ref[pl.ds(off, t), :]

    def body(j, carry):
        ca, cb = carry
        ka, kb, v = load(pl.multiple_of(j * t, t))
        return _flash_step(qa, ka, v, *ca, None), _flash_step(qb, kb, v, *cb, None)

    ca, cb = lax.fori_loop(0, i, body, (init(), init()))
    ka, kb, v = load(pl.multiple_of(i * t, t))
    mask = _diag_mask(t)
    _, la, aa = _flash_step(qa, ka, v, *ca, mask)
    _, lb, ab = _flash_step(qb, kb, v, *cb, mask)
    o_ref[...] = jnp.where(lane < C_V, aa / la, ab / lb).astype(BF16)


def _c_attn(q, k, v, t):
    S = q.shape[0]
    return pl.pallas_call(
        functools.partial(_c_attn_kernel, t=t),
        grid=(C_HEADS // 2, S // t),
        in_specs=[
            pl.BlockSpec((t, 2 * LANES), lambda h, i: (i, h)),
            pl.BlockSpec((S, 2 * LANES), lambda h, i: (0, h)),
            pl.BlockSpec((S, LANES), lambda h, i: (0, h)),
        ],
        out_specs=pl.BlockSpec((t, LANES), lambda h, i: (i, h)),
        out_shape=jax.ShapeDtypeStruct((S, C_HEADS * C_V), BF16),
        compiler_params=_cparams(("arbitrary", "arbitrary")),
        name="c_mla_attn",
    )(q, k, v)


def _c_out_router_kernel(o_ref, x_ref, wo_ref, g_ref, rhi_ref, rlo_ref, tri_ref,
                         x3_o, h_o, meta_o, cnt_o, run_sc):
    i = pl.program_id(0)

    @pl.when(i == 0)
    def _():
        run_sc[...] = jnp.zeros_like(run_sc)

    x3 = x_ref[...] + _dot(o_ref[...], wo_ref[...])
    x3_o[...] = x3
    h = _rms(x3, g_ref[...])
    for c in range(ROW_TILES):
        h_o[:, c, :] = h[:, c * LANES:(c + 1) * LANES]
    h_hi = h.astype(BF16)
    h_lo = (h - h_hi.astype(F32)).astype(BF16)
    logits = (_dot(h_hi, rhi_ref[...]) + _dot(h_hi, rlo_ref[...])) + _dot(h_lo, rhi_ref[...])
    tm = logits.shape[0]
    lane = lax.broadcasted_iota(jnp.int32, (tm, LANES), 1)
    lanef = lane.astype(F32)
    logits = jnp.where(lane < N_EXPERTS, logits, -jnp.inf)
    v1 = jnp.max(logits, axis=1, keepdims=True)
    i1 = jnp.min(jnp.where(logits == v1, lanef, float(LANES)), axis=1, keepdims=True)
    m1 = lanef == i1
    rest = jnp.where(m1, -jnp.inf, logits)
    v2 = jnp.max(rest, axis=1, keepdims=True)
    i2 = jnp.min(jnp.where(rest == v2, lanef, float(LANES)), axis=1, keepdims=True)
    m2 = lanef == i2
    e = jnp.exp(v2 - v1)
    g1 = 1.0 / (1.0 + e)
    g2 = e / (1.0 + e)
    chosen = jnp.where(m1 | m2, 1.0, 0.0)
    before = _dot(tri_ref[...], chosen.astype(BF16)) + run_sc[0:1, :]
    p1 = jnp.sum(jnp.where(m1, before, 0.0), axis=1, keepdims=True)
    p2 = jnp.sum(jnp.where(m2, before, 0.0), axis=1, keepdims=True)
    run_sc[...] = run_sc[...] + jnp.sum(chosen, axis=0, keepdims=True)
    meta = jnp.where(lane == 0, i1, 0.0)
    meta = jnp.where(lane == 1, i2, meta)
    meta = jnp.where(lane == 2, g1, meta)
    meta = jnp.where(lane == 3, g2, meta)
    meta = jnp.where(lane == 4, p1, meta)
    meta = jnp.where(lane == 5, p2, meta)
    meta_o[...] = meta
    cnt_o[...] = run_sc[...]


def _c_out_router(o, x2d, wo, g, rhi, rlo, tri, tm):
    S = x2d.shape[0]
    row = lambda i: (i, 0)
    const = lambda i: (0, 0)
    return pl.pallas_call(
        _c_out_router_kernel,
        grid=(S // tm,),
        in_specs=[pl.BlockSpec((tm, D_MODEL), row), pl.BlockSpec((tm, D_MODEL), row),
                  pl.BlockSpec((D_MODEL, D_MODEL), const), pl.BlockSpec((1, D_MODEL), const),
                  pl.BlockSpec((D_MODEL, LANES), const), pl.BlockSpec((D_MODEL, LANES), const),
                  pl.BlockSpec((tm, tm), const)],
        out_specs=[pl.BlockSpec((tm, D_MODEL), row),
                   pl.BlockSpec((tm, ROW_TILES, LANES), lambda i: (i, 0, 0)),
                   pl.BlockSpec((tm, LANES), row), pl.BlockSpec((8, LANES), const)],
        out_shape=(jax.ShapeDtypeStruct((S, D_MODEL), F32),
                   jax.ShapeDtypeStruct((S, ROW_TILES, LANES), F32),
                   jax.ShapeDtypeStruct((S, LANES), F32), jax.ShapeDtypeStruct((8, LANES), F32)),
        scratch_shapes=[pltpu.VMEM((8, LANES), F32)],
        compiler_params=_cparams(("arbitrary",)),
        name="c_out_router",
    )(o, x2d, wo, g, rhi, rlo, tri)


MOE_TM = 256
PAD_PIECES = (128, 64, 32, 16, 8, 4, 2, 1)


def _dispatch_kernel(pad_ref, slot_ref, h_ref, xs_ref, zero_sc, sem, zsem, *, ts):
    i = pl.program_id(0)

    @pl.when(i == 0)
    def _():
        zero_sc[...] = jnp.zeros_like(zero_sc)
        for e in range(N_EXPERTS):
            start = pad_ref[0, e]
            npad = pad_ref[1, e]
            for p in PAD_PIECES:
                hit = (npad & p) != 0

                @pl.when(hit)
                def _(start=start, p=p):
                    cp = pltpu.make_async_copy(zero_sc.at[pl.ds(0, p)],
                                               xs_ref.at[pl.ds(start, p)], zsem)
                    cp.start()
                    cp.wait()

                start = start + jnp.where(hit, p, 0)

        half = MOE_TM // 2

        def zero_tile(tile, c):
            for part in range(2):
                cp = pltpu.make_async_copy(
                    zero_sc, xs_ref.at[pl.ds(tile * MOE_TM + part * half, half)], zsem)
                cp.start()
                cp.wait()
            return c

        lax.fori_loop(pad_ref[2, 0], xs_ref.shape[0] // MOE_TM, zero_tile, 0)

    def row_copy(r, k):
        return pltpu.make_async_copy(h_ref.at[r], xs_ref.at[slot_ref[0, 0, 2 * r + k]], sem)

    def issue(r, c):
        row_copy(r, 0).start()
        row_copy(r, 1).start()
        return c

    lax.fori_loop(0, ts, issue, 0)

    def drain(r, c):
        row_copy(r, 0).wait()
        row_copy(r, 1).wait()
        return c

    lax.fori_loop(0, ts, drain, 0)


def _dispatch(pad_info, slots3, h2, n_slots, ts):
    S = h2.shape[0]
    return pl.pallas_call(
        functools.partial(_dispatch_kernel, ts=ts),
        grid=(S // ts,),
        in_specs=[pl.BlockSpec(memory_space=pltpu.SMEM),
                  pl.BlockSpec((1, 1, 2 * ts), lambda i: (i, 0, 0), memory_space=pltpu.SMEM),
                  pl.BlockSpec((ts, ROW_TILES, LANES), lambda i: (i, 0, 0))],
        out_specs=pl.BlockSpec(memory_space=pl.ANY),
        out_shape=jax.ShapeDtypeStruct((n_slots, ROW_TILES, LANES), F32),
        scratch_shapes=[pltpu.VMEM((MOE_TM // 2, ROW_TILES, LANES), F32),
                        pltpu.SemaphoreType.DMA(()), pltpu.SemaphoreType.DMA(())],
        compiler_params=_cparams(("arbitrary",)),
        name="moe_dispatch",
    )(pad_info, slots3, h2)


MOE_FC = 512


def _moe_kernel(te_ref, tv_ref, xs_ref, wg_ref, wu_ref, wd_ref, y_ref, x_sc):
    t = pl.program_id(0)

    @pl.when(tv_ref[t] == 0)
    def _():
        y_ref[...] = jnp.zeros_like(y_ref)

    @pl.when(tv_ref[t] != 0)
    def _():
        for c in range(ROW_TILES):
            x_sc[:, c * LANES:(c + 1) * LANES] = xs_ref[:, c, :].astype(BF16)
        x = x_sc[...]
        acc = jnp.zeros(x.shape, F32)
        for c in range(D_FF_EXPERT // MOE_FC):
            sl = slice(c * MOE_FC, (c + 1) * MOE_FC)
            gate = _dot(x, wg_ref[:, sl])
            up = _dot(x, wu_ref[:, sl])
            act = (gate * jax.nn.sigmoid(gate) * up).astype(BF16)
            acc = acc + _dot(act, wd_ref[sl, :])
        for c in range(ROW_TILES):
            y_ref[:, c, :] = acc[:, c * LANES:(c + 1) * LANES]


def _moe(tile_expert, tile_valid, xs, w_gu, w_down):
    n_slots = xs.shape[0]
    nt = n_slots // MOE_TM
    grid_spec = pltpu.PrefetchScalarGridSpec(
        num_scalar_prefetch=2,
        grid=(nt,),
        in_specs=[
            pl.BlockSpec((MOE_TM, ROW_TILES, LANES), lambda t, te, tv: (t, 0, 0)),
            pl.BlockSpec((None, D_MODEL, D_FF_EXPERT), lambda t, te, tv: (te[t], 0, 0),
                         pipeline_mode=pl.Buffered(1)),
            pl.BlockSpec((None, D_MODEL, D_FF_EXPERT), lambda t, te, tv: (te[t], 0, 1),
                         pipeline_mode=pl.Buffered(1)),
            pl.BlockSpec((None, D_FF_EXPERT, D_MODEL), lambda t, te, tv: (te[t], 0, 0),
                         pipeline_mode=pl.Buffered(1)),
        ],
        out_specs=pl.BlockSpec((MOE_TM, ROW_TILES, LANES), lambda t, te, tv: (t, 0, 0)),
        scratch_shapes=[pltpu.VMEM((MOE_TM, D_MODEL), BF16)],
    )
    return pl.pallas_call(
        _moe_kernel,
        grid_spec=grid_spec,
        out_shape=jax.ShapeDtypeStruct((n_slots, ROW_TILES, LANES), F32),
        compiler_params=_cparams(("arbitrary",)),
        name="moe_experts",
    )(tile_expert, tile_valid, xs, w_gu, w_gu, w_down)


def _combine_kernel(slot_ref, x3_ref, meta_ref, y_ref, o_ref, buf, sem, *, ts):
    def row_copy(r, k):
        return pltpu.make_async_copy(y_ref.at[slot_ref[0, 0, 2 * r + k]], buf.at[k, r], sem)

    def issue(r, c):
        row_copy(r, 0).start()
        row_copy(r, 1).start()
        return c

    lax.fori_loop(0, ts, issue, 0)

    def drain(r, c):
        row_copy(r, 0).wait()
        row_copy(r, 1).wait()
        return c

    lax.fori_loop(0, ts, drain, 0)
    meta = meta_ref[...]
    g1 = meta[:, 2:3]
    g2 = meta[:, 3:4]
    for c in range(ROW_TILES):
        sl = slice(c * LANES, (c + 1) * LANES)
        o_ref[:, sl] = x3_ref[:, sl] + g1 * buf[0, :, c, :] + g2 * buf[1, :, c, :]


def _combine(slots3, x3, meta, y, ts):
    S = x3.shape[0]
    return pl.pallas_call(
        functools.partial(_combine_kernel, ts=ts),
        grid=(S // ts,),
        in_specs=[pl.BlockSpec((1, 1, 2 * ts), lambda i: (i, 0, 0), memory_space=pltpu.SMEM),
                  pl.BlockSpec((ts, D_MODEL), lambda i: (i, 0)),
                  pl.BlockSpec((ts, LANES), lambda i: (i, 0)),
                  pl.BlockSpec(memory_space=pl.ANY)],
        out_specs=pl.BlockSpec((ts, D_MODEL), lambda i: (i, 0)),
        out_shape=jax.ShapeDtypeStruct((S, D_MODEL), F32),
        scratch_shapes=[pltpu.VMEM((2, ts, ROW_TILES, LANES), F32), pltpu.SemaphoreType.DMA(())],
        compiler_params=_cparams(("arbitrary",)),
        name="moe_combine",
    )(slots3, x3, meta, y)


def _tile_lanes(v, reps):
    return jnp.tile(v, reps)[None, :].astype(F32)


def _rope_tables_64(S):
    pos = jnp.arange(S, dtype=F32)
    inv = ROPE_THETA ** (-jnp.arange(0, A_HEAD_DIM, 2, dtype=F32) / A_HEAD_DIM)
    ang = pos[:, None] * inv[None, :]
    c, s = jnp.cos(ang), jnp.sin(ang)
    return jnp.concatenate([c, c, c, c], axis=1), jnp.concatenate([-s, s, -s, s], axis=1)


def _rope_tables_32(S):
    pos = jnp.arange(S, dtype=F32)
    inv = ROPE_THETA ** (-jnp.arange(0, C_ROPE, 2, dtype=F32) / C_ROPE)
    ang = pos[:, None] * inv[None, :]
    c, s = jnp.cos(ang), jnp.sin(ang)
    one = jnp.ones((S, C_NOPE), F32)
    zero = jnp.zeros((S, C_NOPE), F32)
    pad = jnp.zeros((S, LANES - C_QK), F32)
    return (jnp.concatenate([one, c, c, pad], axis=1),
            jnp.concatenate([zero, -s, s, pad], axis=1))


def _pad_heads(w, n_heads, width):
    k = w.shape[0]
    w = w.reshape(k, n_heads, width)
    return jnp.pad(w, ((0, 0), (0, 0), (0, LANES - width))).reshape(k, n_heads * LANES)


def _swap_rope_cols(w_p):
    k = w_p.shape[0]
    w = w_p.reshape(k, -1, LANES)
    half = C_ROPE // 2
    sw = jnp.concatenate([jnp.zeros_like(w[:, :, :C_NOPE]),
                          w[:, :, C_NOPE + half:C_QK], w[:, :, C_NOPE:C_NOPE + half],
                          jnp.zeros_like(w[:, :, C_QK:])], axis=2)
    return sw.reshape(k, -1)


def kernel(x, l0_norm_attn, l0_w_in, l0_a_q_norm, l0_a_k_norm, l0_a_lambda, l0_a_subln, l0_b_q_norm, l0_b_k_norm, l0_b_sinks, l0_w_out, l0_norm_ffn, l0_ffn_w_gu, l0_ffn_w_down, l1_norm_attn, l1_c_w_in, l1_c_q_lora_norm, l1_c_kv_lora_norm, l1_c_w_uq, l1_c_w_ukv, l1_c_q_norm, l1_c_k_norm, l1_c_w_out, l1_norm_ffn, l1_router, l1_exp_w_gu, l1_exp_w_down):
    B, S, _ = x.shape
    assert B == 1
    x2d = x.reshape(S, D_MODEL)
    t_attn = min(512, S)
    tm = min(512, S)

    o_bk = 3 * A_W + B_QW
    o_bv = o_bk + B_KW
    bk_w = l0_w_in[:, o_bk:o_bv].reshape(D_MODEL, B_KV_HEADS, B_HEAD_DIM)
    bv_w = l0_w_in[:, o_bv:].reshape(D_MODEL, B_KV_HEADS, B_HEAD_DIM)
    dup = lambda w: jnp.concatenate([w, w], axis=2).reshape(D_MODEL, 2 * B_KW)
    w_in_p = jnp.concatenate([l0_w_in[:, :o_bk], dup(bk_w), dup(bv_w)], axis=1).astype(BF16)
    gains = jnp.concatenate([_tile_lanes(l0_a_q_norm, 2), _tile_lanes(l0_a_k_norm, 2),
                             _tile_lanes(l0_b_q_norm, 2), _tile_lanes(l0_b_k_norm, 2),
                             jnp.zeros((4, LANES), F32)], axis=0)
    lane = jnp.arange(LANES)
    ones64 = (lane[:, None] // 64 == lane[None, :] // 64).astype(BF16)
    cos64, sin64 = _rope_tables_64(S)
    aq, ak, av, bq, bk, bv = _l0_in(x2d, l0_norm_attn[None, :], w_in_p, gains, ones64,
                                    cos64, sin64, tm)
    lam_init = 0.8 - 0.6 * math.exp(-0.3 * 0)
    a_out = _a_attn(l0_a_lambda.astype(F32), aq, ak, av, l0_a_subln[None, :].astype(F32),
                    t_attn, lam_init)
    b_out = _b_attn(l0_b_sinks.astype(F32), bq, bk, bv)
    w_out = l0_w_out.astype(BF16)
    x2 = _l0_ffn(a_out, b_out, x2d, w_out[:A_W], w_out[A_W:], l0_norm_ffn[None, :],
                 l0_ffn_w_gu.astype(BF16), l0_ffn_w_down.astype(BF16), tm, D_FF // 2)

    wq = l1_c_w_in[:, :C_Q_RANK].astype(BF16)
    wkv = l1_c_w_in[:, C_Q_RANK:C_Q_RANK + C_KV_RANK].astype(BF16)
    wkr = jnp.pad(l1_c_w_in[:, C_Q_RANK + C_KV_RANK:], ((0, 0), (C_NOPE, LANES - C_QK)))
    wkrs = _swap_rope_cols(wkr).astype(BF16)
    wkr = wkr.astype(BF16)
    wuq = _pad_heads(l1_c_w_uq, C_HEADS, C_QK)
    wuqs = _swap_rope_cols(wuq).astype(BF16)
    wuq = wuq.astype(BF16)
    ukv = l1_c_w_ukv.reshape(C_KV_RANK, C_HEADS, C_NOPE + C_V)
    wuk = _pad_heads(ukv[:, :, :C_NOPE].reshape(C_KV_RANK, -1), C_HEADS, C_NOPE).astype(BF16)
    wuv = ukv[:, :, C_NOPE:].reshape(C_KV_RANK, C_HEADS * C_V).astype(BF16)
    pad_gain = lambda gvec: jnp.pad(gvec.astype(F32), (0, LANES - C_QK))[None, :]
    qg, kg = pad_gain(l1_c_q_norm), pad_gain(l1_c_k_norm)
    qgs, kgs = _swap_rope_cols(qg), _swap_rope_cols(kg)
    ones128 = jnp.ones((LANES, LANES), BF16)
    cos32, sin32 = _rope_tables_32(S)
    cq, ck, cv = _c_in(x2, l1_norm_attn[None, :], wq, wkv, wkr, wkrs,
                       l1_c_q_lora_norm[None, :], l1_c_kv_lora_norm[None, :],
                       wuq, wuqs, wuk, wuv, qg, qgs, kg, kgs, ones128, cos32, sin32, tm)
    c_o = _c_attn(cq, ck, cv, t_attn)

    r_pad = jnp.pad(l1_router.astype(F32), ((0, 0), (0, LANES - N_EXPERTS)))
    r_hi = r_pad.astype(BF16)
    r_lo = (r_pad - r_hi.astype(F32)).astype(BF16)
    tr = min(256, S)
    ridx = jnp.arange(tr)
    tri = (ridx[None, :] < ridx[:, None]).astype(BF16)
    x3, h2, meta, cnt = _c_out_router(c_o, x2, l1_c_w_out.astype(BF16), l1_norm_ffn[None, :],
                                      r_hi, r_lo, tri, tr)

    counts = cnt[0, :N_EXPERTS].astype(jnp.int32)
    padded = ((counts + MOE_TM - 1) // MOE_TM) * MOE_TM
    ends = jnp.cumsum(padded)
    offs = ends - padded
    idx = meta[:, 0:2].astype(jnp.int32)
    pos = meta[:, 4:6].astype(jnp.int32)
    slots = offs[idx] + pos
    n_slots = 2 * S + N_EXPERTS * MOE_TM
    nt = n_slots // MOE_TM
    tile_start = jnp.arange(nt, dtype=jnp.int32) * MOE_TM
    n_valid = jnp.broadcast_to(ends[-1] // MOE_TM, (N_EXPERTS,))
    tile_valid = (tile_start < ends[-1]).astype(jnp.int32)
    tile_expert = jnp.minimum(
        jnp.sum((tile_start[:, None] >= ends[None, :]).astype(jnp.int32), axis=1),
        N_EXPERTS - 1).astype(jnp.int32)
    pad_info = jnp.stack([offs + counts, padded - counts, n_valid]).astype(jnp.int32)

    ts = min(256, S)
    slots3 = slots.reshape(S // ts, 1, 2 * ts)
    xs = _dispatch(pad_info, slots3, h2, n_slots, ts)
    y = _moe(tile_expert, tile_valid, xs,
             l1_exp_w_gu.astype(BF16), l1_exp_w_down.astype(BF16))
    out = _combine(slots3, x3, meta, y, ts)
    return out.reshape(B, S, D_MODEL)
```

```python
import functools
import math

import jax
import jax.numpy as jnp
from jax import lax
from jax.experimental import pallas as pl
from jax.experimental.pallas import tpu as pltpu

F32 = jnp.float32
BF16 = jnp.bfloat16

D_MODEL = 1024
CHUNK = 64
ROPE_THETA = 10000.0
EPS = 1e-6
LANES = 128
ROW_TILES = D_MODEL // LANES

A_HEADS = 4
A_HEAD_DIM = 64
B_HEADS = 8
B_KV_HEADS = 2
B_HEAD_DIM = 64
A_W = A_HEADS * 2 * A_HEAD_DIM
B_QW = B_HEADS * B_HEAD_DIM
B_KW = B_KV_HEADS * B_HEAD_DIM

C_HEADS = 16
C_Q_RANK = 256
C_KV_RANK = 128
C_NOPE = 64
C_ROPE = 32
C_V = 64
C_QK = C_NOPE + C_ROPE

D_FF = 2816
N_EXPERTS = 8
D_FF_EXPERT = 3584

NEG_BIG = -1e30
LOG2E = 1.0 / math.log(2.0)
VMEM_LIMIT = 56 * 1024 * 1024


def _cparams(sem):
    return pltpu.CompilerParams(dimension_semantics=sem, vmem_limit_bytes=VMEM_LIMIT)


def _dot(a, b):
    return jnp.dot(a, b, preferred_element_type=F32)


def _dot_nt(a, b):
    return lax.dot_general(a, b, (((1,), (1,)), ((), ())), preferred_element_type=F32)


def _rms(x, g):
    ms = jnp.mean(x * x, axis=-1, keepdims=True)
    return x * lax.rsqrt(ms + EPS) * g


def _l0_in_kernel(x_ref, g_ref, w_ref, wvt_ref, vone_ref, gains_ref, ones_ref, cos_ref, sin_ref,
                  aq_o, ak_o, avt_o, bq_o, bk_o, bv_o):
    h = _rms(x_ref[...], g_ref[...]).astype(BF16)
    z = _dot(h, w_ref[...])
    avt_o[...] = (_dot_nt(wvt_ref[...], h) + vone_ref[...]).astype(BF16)
    cos = cos_ref[...]
    sin = sin_ref[...]
    ones_blk = ones_ref[...]
    lane = lax.broadcasted_iota(jnp.int32, (1, LANES), 1)
    first_half = (lane % 64) < 32

    def norm_rope(zs, gain, scale):
        ss = _dot((zs * zs).astype(BF16), ones_blk)
        r = lax.rsqrt(ss * (1.0 / 64.0) + EPS) * scale
        y = zs * gain
        sw = jnp.where(first_half, pltpu.roll(y, 96, 1), pltpu.roll(y, 32, 1))
        return ((y * cos + sw * sin) * r).astype(BF16)

    col = 0
    gcol = 0
    scale = A_HEAD_DIM ** -0.5
    scales = {"aq": scale * LOG2E, "ak": 1.0, "bq": scale, "bk": 1.0}
    outs = {"aq": aq_o, "ak": ak_o, "bq": bq_o, "bk": bk_o}
    for name, nslab in (("aq", 4), ("ak", 4), ("bq", 4), ("bk", 2), ("bv", 2)):
        if name == "bv":
            bv_o[...] = z[:, col:col + 2 * LANES].astype(BF16)
            col += 2 * LANES
            continue
        sc = scales[name]
        gain = gains_ref[gcol:gcol + 1, :]
        gcol += 1
        for s in range(nslab):
            zs = z[:, col:col + LANES]
            outs[name][:, s * LANES:(s + 1) * LANES] = norm_rope(zs, gain, sc)
            col += LANES


def _l0_in(x2d, g, w_in_p, w_avt, vone, gains, ones_blk, cos_t, sin_t, tm):
    S = x2d.shape[0]
    nw = w_in_p.shape[1]
    row = lambda i: (i, 0)
    const = lambda i: (0, 0)
    out_shapes = (
        jax.ShapeDtypeStruct((S, A_W), BF16), jax.ShapeDtypeStruct((S, A_W), BF16),
        jax.ShapeDtypeStruct((A_HEADS * A_V_ROWS, S), BF16), jax.ShapeDtypeStruct((S, B_QW), BF16),
        jax.ShapeDtypeStruct((S, 2 * B_KW), BF16), jax.ShapeDtypeStruct((S, 2 * B_KW), BF16))
    return pl.pallas_call(
        _l0_in_kernel,
        grid=(S // tm,),
        in_specs=[
            pl.BlockSpec((tm, D_MODEL), row),
            pl.BlockSpec((1, D_MODEL), const),
            pl.BlockSpec((D_MODEL, nw), const),
            pl.BlockSpec((A_HEADS * A_V_ROWS, D_MODEL), const),
            pl.BlockSpec((A_HEADS * A_V_ROWS, 1), const),
            pl.BlockSpec((8, LANES), const),
            pl.BlockSpec((LANES, LANES), const),
            pl.BlockSpec((tm, LANES), row),
            pl.BlockSpec((tm, LANES), row),
        ],
        out_specs=[
            pl.BlockSpec((tm, A_W), row), pl.BlockSpec((tm, A_W), row),
            pl.BlockSpec((A_HEADS * A_V_ROWS, tm), lambda i: (0, i)), pl.BlockSpec((tm, B_QW), row),
            pl.BlockSpec((tm, 2 * B_KW), row), pl.BlockSpec((tm, 2 * B_KW), row)],
        out_shape=out_shapes,
        compiler_params=_cparams(("arbitrary",)),
        name="l0_in_proj",
    )(x2d, g, w_in_p, w_avt, vone, gains, ones_blk, cos_t, sin_t)


ATT_TQ = 512
ATT_TK = 256
V_PAD = 16


def _softmax_pv(s_ref, vt, m, acc):
    s = s_ref[...]
    m_new = jnp.maximum(m, jnp.max(s, axis=0, keepdims=True))
    alpha = jnp.exp2(m - m_new)
    p = jnp.exp2(s - m_new).astype(BF16)
    return m_new, alpha * acc + _dot(vt, p)


def _diag_masks():
    key_chunk = lax.broadcasted_iota(jnp.int32, (ATT_TK, ATT_TQ), 0) // CHUNK
    query_chunk = lax.broadcasted_iota(jnp.int32, (ATT_TK, ATT_TQ), 1) // CHUNK
    return [key_chunk + b * (ATT_TK // CHUNK) <= query_chunk for b in range(ATT_TQ // ATT_TK)]


def _attn_pipeline(i, k_ref, vt_ref, streams, v_rows):
    n = len(streams)

    def scores(st, blk, mask, dst):
        q, lanes = streams[st][0], streams[st][1]
        off = pl.multiple_of(blk * ATT_TK, ATT_TK)
        s = _dot_nt(k_ref[pl.ds(off, ATT_TK), lanes], q)
        dst[...] = s if mask is None else jnp.where(mask, s, NEG_BIG)

    def consume(st, blk, src, state):
        off = pl.multiple_of(blk * ATT_TK, ATT_TK)
        return _softmax_pv(src, vt_ref[streams[st][2], pl.ds(off, ATT_TK)], *state)

    buf_a = [st[3] for st in streams]
    buf_b = [st[4] for st in streams]
    mask0, mask1 = _diag_masks()
    d0 = 2 * i
    d1 = d0 + 1
    state = [(jnp.full((1, ATT_TQ), NEG_BIG, F32), jnp.zeros((v_rows, ATT_TQ), F32))] * n
    for st in range(n):
        scores(st, d0, mask0, buf_a[st])
    for st in range(n):
        scores(st, d1, mask1, buf_b[st])
        state[st] = consume(st, d0, buf_a[st], state[st])
    for st in range(n):
        scores(st, 0, None, buf_a[st])
        state[st] = consume(st, d1, buf_b[st], state[st])

    def body(p, carry):
        carry = list(carry)
        u0 = 2 * p
        u1 = u0 + 1
        nxt = jnp.minimum(u0 + 2, d0 - 1)
        for st in range(n):
            scores(st, u1, None, buf_b[st])
            carry[st] = consume(st, u0, buf_a[st], carry[st])
        for st in range(n):
            scores(st, nxt, None, buf_a[st])
            carry[st] = consume(st, u1, buf_b[st], carry[st])
        return tuple(carry)

    return [acc for _, acc in lax.fori_loop(0, i, body, tuple(state))]


_SCORE_BUF = pltpu.VMEM((ATT_TK, ATT_TQ), F32)


A_V_ROWS = 2 * A_HEAD_DIM + V_PAD


def _a_attn_kernel(lam_ref, q_ref, k_ref, vt_ref, subln_ref, o_ref, s1a, s1b, s2a, s2b,
                   *, lam_init):
    i = pl.program_id(1)
    lane = lax.broadcasted_iota(jnp.int32, (1, LANES), 1)
    q = q_ref[...]
    zero = jnp.zeros_like(q)
    all_lanes = slice(0, LANES)
    all_rows = slice(0, A_V_ROWS)
    streams = ((jnp.where(lane < 64, q, zero), all_lanes, all_rows, s1a, s1b),
               (jnp.where(lane >= 64, q, zero), all_lanes, all_rows, s2a, s2b))
    acc1, acc2 = _attn_pipeline(i, k_ref, vt_ref, streams, A_V_ROWS)
    dv = 2 * A_HEAD_DIM
    lf = lam_ref[...]
    lam = (jnp.exp(jnp.sum(lf[0:1] * lf[1:2], axis=1, keepdims=True))
           - jnp.exp(jnp.sum(lf[2:3] * lf[3:4], axis=1, keepdims=True)) + lam_init)
    out = (acc1[:dv] / acc1[dv:dv + 1] - lam * (acc2[:dv] / acc2[dv:dv + 1])).T
    out = _rms(out, subln_ref[...]) * (1.0 - lam_init)
    o_ref[...] = out.astype(BF16)


def _a_attn(lam_p, aq, ak, avt, subln, lam_init):
    S = aq.shape[0]
    return pl.pallas_call(
        functools.partial(_a_attn_kernel, lam_init=lam_init),
        grid=(A_HEADS, S // ATT_TQ),
        in_specs=[
            pl.BlockSpec((4, A_HEAD_DIM), lambda h, i: (0, 0)),
            pl.BlockSpec((ATT_TQ, LANES), lambda h, i: (i, h)),
            pl.BlockSpec((S, LANES), lambda h, i: (0, h)),
            pl.BlockSpec((A_V_ROWS, S), lambda h, i: (h, 0)),
            pl.BlockSpec((1, LANES), lambda h, i: (0, 0)),
        ],
        out_specs=pl.BlockSpec((ATT_TQ, LANES), lambda h, i: (i, h)),
        out_shape=jax.ShapeDtypeStruct((S, A_W), BF16),
        scratch_shapes=[_SCORE_BUF] * 4,
        compiler_params=_cparams(("arbitrary", "arbitrary")),
        name="a_diff_attn",
    )(lam_p, aq, ak, avt, subln)


B_BLK = 128


def _b_attn_kernel(sink_ref, q_ref, kp_ref, kc_ref, vp_ref, vc_ref, o_ref):
    i = pl.program_id(0)
    lane = lax.broadcasted_iota(jnp.int32, (1, LANES), 1)
    lo = lane < 64
    r = lax.broadcasted_iota(jnp.int32, (B_BLK, 2 * B_BLK), 0) // CHUNK
    c_idx = lax.broadcasted_iota(jnp.int32, (B_BLK, 2 * B_BLK), 1)
    c = c_idx // CHUNK
    mask = (c >= r) & (c <= r + 2) & ((c_idx >= B_BLK) | (i > 0))
    for slab in range(B_HEADS // 2):
        g = slab // 2
        k = jnp.concatenate([kp_ref[:, g * LANES:(g + 1) * LANES],
                             kc_ref[:, g * LANES:(g + 1) * LANES]], axis=0)
        v = jnp.concatenate([vp_ref[:, g * LANES:(g + 1) * LANES],
                             vc_ref[:, g * LANES:(g + 1) * LANES]], axis=0)
        qs = q_ref[:, slab * LANES:(slab + 1) * LANES]
        zq = jnp.zeros_like(qs)
        zv = jnp.zeros_like(v)
        out = jnp.zeros((B_BLK, LANES), F32)
        for half in range(2):
            sel = lo if half == 0 else jnp.logical_not(lo)
            sink = sink_ref[2 * slab + half]
            s = _dot_nt(jnp.where(sel, qs, zq), k)
            s = jnp.where(mask, s, NEG_BIG)
            m = jnp.maximum(jnp.max(s, axis=1, keepdims=True), sink)
            e = jnp.exp(s - m)
            denom = jnp.sum(e, axis=1, keepdims=True) + jnp.exp(sink - m)
            p = (e / denom).astype(BF16)
            out = out + _dot(p, jnp.where(sel, v, zv))
        o_ref[:, slab * LANES:(slab + 1) * LANES] = out.astype(BF16)


def _b_attn(sinks, bq, bk, bv):
    S = bq.shape[0]
    prev = lambda i: (jnp.maximum(i - 1, 0), 0)
    cur = lambda i: (i, 0)
    return pl.pallas_call(
        _b_attn_kernel,
        grid=(S // B_BLK,),
        in_specs=[
            pl.BlockSpec(memory_space=pltpu.SMEM),
            pl.BlockSpec((B_BLK, B_QW), cur),
            pl.BlockSpec((B_BLK, 2 * B_KW), prev),
            pl.BlockSpec((B_BLK, 2 * B_KW), cur),
            pl.BlockSpec((B_BLK, 2 * B_KW), prev),
            pl.BlockSpec((B_BLK, 2 * B_KW), cur),
        ],
        out_specs=pl.BlockSpec((B_BLK, B_QW), cur),
        out_shape=jax.ShapeDtypeStruct((S, B_QW), BF16),
        compiler_params=_cparams(("arbitrary",)),
        name="b_swa_attn",
    )(sinks, bq, bk, bk, bv, bv)


def _l0_ffn_kernel(a_ref, b_ref, x_ref, woa_ref, wob_ref, g_ref, wg_ref, wu_ref, wd_ref,
                   o_ref, h_sc, acc_sc):
    j = pl.program_id(1)

    @pl.when(j == 0)
    def _():
        x1 = x_ref[...] + _dot(a_ref[...], woa_ref[...]) + _dot(b_ref[...], wob_ref[...])
        acc_sc[...] = x1
        h_sc[...] = _rms(x1, g_ref[...]).astype(BF16)

    h = h_sc[...]
    gate = _dot(h, wg_ref[...])
    up = _dot(h, wu_ref[...])
    act = (gate * jax.nn.sigmoid(gate) * up).astype(BF16)
    acc_sc[...] += _dot(act, wd_ref[...])

    @pl.when(j == pl.num_programs(1) - 1)
    def _():
        o_ref[...] = acc_sc[...]


def _l0_ffn(a_out, b_out, x2d, wo_a, wo_b, g, w_gu, w_down, tm, tf):
    S = x2d.shape[0]
    nf = D_FF // tf
    row = lambda i, j: (i, 0)
    const = lambda i, j: (0, 0)
    return pl.pallas_call(
        _l0_ffn_kernel,
        grid=(S // tm, nf),
        in_specs=[
            pl.BlockSpec((tm, A_W), row),
            pl.BlockSpec((tm, B_QW), row),
            pl.BlockSpec((tm, D_MODEL), row),
            pl.BlockSpec((A_W, D_MODEL), const),
            pl.BlockSpec((B_QW, D_MODEL), const),
            pl.BlockSpec((1, D_MODEL), const),
            pl.BlockSpec((D_MODEL, tf), lambda i, j: (0, j)),
            pl.BlockSpec((D_MODEL, tf), lambda i, j: (0, nf + j)),
            pl.BlockSpec((tf, D_MODEL), lambda i, j: (j, 0)),
        ],
        out_specs=pl.BlockSpec((tm, D_MODEL), row),
        out_shape=jax.ShapeDtypeStruct((S, D_MODEL), F32),
        scratch_shapes=[pltpu.VMEM((tm, D_MODEL), BF16), pltpu.VMEM((tm, D_MODEL), F32)],
        compiler_params=_cparams(("arbitrary", "arbitrary")),
        name="l0_out_ffn",
    )(a_out, b_out, x2d, wo_a, wo_b, g, w_gu, w_gu, w_down)


def _c_in_kernel(x_ref, g_ref, wq_ref, wkv_ref, wkr_ref, wkrs_ref, gql_ref, gkvl_ref,
                 wuq_ref, wuqs_ref, wuk_ref, wuv_ref, qg_ref, qgs_ref, kg_ref, kgs_ref,
                 vone_ref, ones_ref, cos_ref, sin_ref, q_o, k_o, vt_o):
    h = _rms(x_ref[...], g_ref[...]).astype(BF16)
    cq = _rms(_dot(h, wq_ref[...]), gql_ref[...]).astype(BF16)
    ckv = _rms(_dot(h, wkv_ref[...]), gkvl_ref[...]).astype(BF16)
    kr = _dot(h, wkr_ref[...])
    krs = _dot(h, wkrs_ref[...])
    q = _dot(cq, wuq_ref[...])
    qs = _dot(cq, wuqs_ref[...])
    kn = _dot(ckv, wuk_ref[...])
    vt_o[...] = (_dot_nt(wuv_ref[...], ckv) + vone_ref[...]).astype(BF16)

    cos = cos_ref[...]
    sin = sin_ref[...]
    ones_blk = ones_ref[...]
    qg, qgs, kg, kgs = qg_ref[...], qgs_ref[...], kg_ref[...], kgs_ref[...]
    ss_kr = _dot((kr * kr).astype(BF16), ones_blk)
    kr_roped = kr * kg * cos + krs * kgs * sin
    scale = C_QK ** -0.5 * LOG2E
    inv = 1.0 / C_QK
    for hd in range(C_HEADS):
        sl = slice(hd * LANES, (hd + 1) * LANES)
        qh = q[:, sl]
        r = lax.rsqrt(_dot((qh * qh).astype(BF16), ones_blk) * inv + EPS) * scale
        q_o[:, sl] = ((qh * qg * cos + qs[:, sl] * qgs * sin) * r).astype(BF16)
        kh = kn[:, sl]
        rk = lax.rsqrt((_dot((kh * kh).astype(BF16), ones_blk) + ss_kr) * inv + EPS)
        k_o[:, sl] = ((kh * kg + kr_roped) * rk).astype(BF16)


def _c_in(x2d, g, wq, wkv, wkr, wkrs, gql, gkvl, wuq, wuqs, wuk, wuv, qg, qgs, kg, kgs,
          vone, ones_blk, cos_t, sin_t, tm):
    S = x2d.shape[0]
    row = lambda i: (i, 0)
    const = lambda i: (0, 0)
    full = lambda a: pl.BlockSpec(a.shape, const)
    W = C_HEADS * LANES
    return pl.pallas_call(
        _c_in_kernel,
        grid=(S // tm,),
        in_specs=[pl.BlockSpec((tm, D_MODEL), row), full(g), full(wq), full(wkv), full(wkr),
                  full(wkrs), full(gql), full(gkvl), full(wuq), full(wuqs), full(wuk),
                  full(wuv), full(qg), full(qgs), full(kg), full(kgs), full(vone),
                  full(ones_blk),
                  pl.BlockSpec((tm, LANES), row), pl.BlockSpec((tm, LANES), row)],
        out_specs=[pl.BlockSpec((tm, W), row), pl.BlockSpec((tm, W), row),
                   pl.BlockSpec((C_HEADS * V_ROWS, tm), lambda i: (0, i))],
        out_shape=(jax.ShapeDtypeStruct((S, W), BF16), jax.ShapeDtypeStruct((S, W), BF16),
                   jax.ShapeDtypeStruct((C_HEADS * V_ROWS, S), BF16)),
        compiler_params=_cparams(("arbitrary",)),
        name="c_in_proj",
    )(x2d, g, wq, wkv, wkr, wkrs, gql, gkvl, wuq, wuqs, wuk, wuv, qg, qgs, kg, kgs,
      vone, ones_blk, cos_t, sin_t)


V_ROWS = C_V + V_PAD


def _c_attn_kernel(q_ref, k_ref, vt_ref, o_ref, sa0, sb0, sa1, sb1):
    i = pl.program_id(1)
    streams = (
        (q_ref[:, :LANES], slice(0, LANES), slice(0, V_ROWS), sa0, sb0),
        (q_ref[:, LANES:], slice(LANES, 2 * LANES), slice(V_ROWS, 2 * V_ROWS), sa1, sb1))
    acc_a, acc_b = _attn_pipeline(i, k_ref, vt_ref, streams, V_ROWS)
    out = jnp.concatenate([acc_a[:C_V] / acc_a[C_V:C_V + 1], acc_b[:C_V] / acc_b[C_V:C_V + 1]],
                          axis=0)
    o_ref[...] = out.T.astype(BF16)


def _c_attn(q, k, vt):
    S = q.shape[0]
    return pl.pallas_call(
        _c_attn_kernel,
        grid=(C_HEADS // 2, S // ATT_TQ),
        in_specs=[
            pl.BlockSpec((ATT_TQ, 2 * LANES), lambda h, i: (i, h)),
            pl.BlockSpec((S, 2 * LANES), lambda h, i: (0, h)),
            pl.BlockSpec((2 * V_ROWS, S), lambda h, i: (h, 0)),
        ],
        out_specs=pl.BlockSpec((ATT_TQ, 2 * C_V), lambda h, i: (i, h)),
        out_shape=jax.ShapeDtypeStruct((S, C_HEADS * C_V), BF16),
        scratch_shapes=[_SCORE_BUF] * 4,
        compiler_params=_cparams(("arbitrary", "arbitrary")),
        name="c_mla_attn",
    )(q, k, vt)


def _c_out_router_kernel(o_ref, x_ref, wo_ref, g_ref, rhi_ref, rlo_ref, tri_ref,
                         x3_o, h_o, meta_o, cnt_o, run_sc):
    i = pl.program_id(0)

    @pl.when(i == 0)
    def _():
        run_sc[...] = jnp.zeros_like(run_sc)

    x3 = x_ref[...] + _dot(o_ref[...], wo_ref[...])
    x3_o[...] = x3
    h = _rms(x3, g_ref[...])
    for c in range(ROW_TILES):
        h_o[:, c, :] = h[:, c * LANES:(c + 1) * LANES]
    h_hi = h.astype(BF16)
    h_lo = (h - h_hi.astype(F32)).astype(BF16)
    logits = (_dot(h_hi, rhi_ref[...]) + _dot(h_hi, rlo_ref[...])) + _dot(h_lo, rhi_ref[...])
    tm = logits.shape[0]
    lane = lax.broadcasted_iota(jnp.int32, (tm, LANES), 1)
    lanef = lane.astype(F32)
    logits = jnp.where(lane < N_EXPERTS, logits, -jnp.inf)
    v1 = jnp.max(logits, axis=1, keepdims=True)
    i1 = jnp.min(jnp.where(logits == v1, lanef, float(LANES)), axis=1, keepdims=True)
    m1 = lanef == i1
    rest = jnp.where(m1, -jnp.inf, logits)
    v2 = jnp.max(rest, axis=1, keepdims=True)
    i2 = jnp.min(jnp.where(rest == v2, lanef, float(LANES)), axis=1, keepdims=True)
    m2 = lanef == i2
    e = jnp.exp(v2 - v1)
    g1 = 1.0 / (1.0 + e)
    g2 = e / (1.0 + e)
    chosen = jnp.where(m1 | m2, 1.0, 0.0)
    before = _dot(tri_ref[...], chosen.astype(BF16)) + run_sc[0:1, :]
    p1 = jnp.sum(jnp.where(m1, before, 0.0), axis=1, keepdims=True)
    p2 = jnp.sum(jnp.where(m2, before, 0.0), axis=1, keepdims=True)
    run_sc[...] = run_sc[...] + jnp.sum(chosen, axis=0, keepdims=True)
    meta = jnp.where(lane == 0, i1, 0.0)
    meta = jnp.where(lane == 1, i2, meta)
    meta = jnp.where(lane == 2, g1, meta)
    meta = jnp.where(lane == 3, g2, meta)
    meta = jnp.where(lane == 4, p1, meta)
    meta = jnp.where(lane == 5, p2, meta)
    meta_o[...] = meta
    cnt_o[...] = run_sc[...]


def _c_out_router(o, x2d, wo, g, rhi, rlo, tri, tm):
    S = x2d.shape[0]
    row = lambda i: (i, 0)
    const = lambda i: (0, 0)
    return pl.pallas_call(
        _c_out_router_kernel,
        grid=(S // tm,),
        in_specs=[pl.BlockSpec((tm, D_MODEL), row), pl.BlockSpec((tm, D_MODEL), row),
                  pl.BlockSpec((D_MODEL, D_MODEL), const), pl.BlockSpec((1, D_MODEL), const),
                  pl.BlockSpec((D_MODEL, LANES), const), pl.BlockSpec((D_MODEL, LANES), const),
                  pl.BlockSpec((tm, tm), const)],
        out_specs=[pl.BlockSpec((tm, D_MODEL), row),
                   pl.BlockSpec((tm, ROW_TILES, LANES), lambda i: (i, 0, 0)),
                   pl.BlockSpec((tm, LANES), row), pl.BlockSpec((8, LANES), const)],
        out_shape=(jax.ShapeDtypeStruct((S, D_MODEL), F32),
                   jax.ShapeDtypeStruct((S, ROW_TILES, LANES), F32),
                   jax.ShapeDtypeStruct((S, LANES), F32), jax.ShapeDtypeStruct((8, LANES), F32)),
        scratch_shapes=[pltpu.VMEM((8, LANES), F32)],
        compiler_params=_cparams(("arbitrary",)),
        name="c_out_router",
    )(o, x2d, wo, g, rhi, rlo, tri)


MOE_TM = 256
PAD_PIECES = (128, 64, 32, 16, 8, 4, 2, 1)


def _dispatch_kernel(pad_ref, slot_ref, h_ref, xs_ref, zero_sc, sem, zsem, *, ts):
    i = pl.program_id(0)

    @pl.when(i == 0)
    def _():
        zero_sc[...] = jnp.zeros_like(zero_sc)
        for e in range(N_EXPERTS):
            start = pad_ref[0, e]
            npad = pad_ref[1, e]
            for p in PAD_PIECES:
                hit = (npad & p) != 0

                @pl.when(hit)
                def _(start=start, p=p):
                    cp = pltpu.make_async_copy(zero_sc.at[pl.ds(0, p)],
                                               xs_ref.at[pl.ds(start, p)], zsem)
                    cp.start()
                    cp.wait()

                start = start + jnp.where(hit, p, 0)

        half = MOE_TM // 2

        def zero_tile(tile, c):
            for part in range(2):
                cp = pltpu.make_async_copy(
                    zero_sc, xs_ref.at[pl.ds(tile * MOE_TM + part * half, half)], zsem)
                cp.start()
                cp.wait()
            return c

        lax.fori_loop(pad_ref[2, 0], xs_ref.shape[0] // MOE_TM, zero_tile, 0)

    def row_copy(r, k):
        return pltpu.make_async_copy(h_ref.at[r], xs_ref.at[slot_ref[0, 0, 2 * r + k]], sem)

    def issue(r, c):
        row_copy(r, 0).start()
        row_copy(r, 1).start()
        return c

    lax.fori_loop(0, ts, issue, 0)

    def drain(r, c):
        row_copy(r, 0).wait()
        row_copy(r, 1).wait()
        return c

    lax.fori_loop(0, ts, drain, 0)


def _dispatch(pad_info, slots3, h2, n_slots, ts):
    S = h2.shape[0]
    return pl.pallas_call(
        functools.partial(_dispatch_kernel, ts=ts),
        grid=(S // ts,),
        in_specs=[pl.BlockSpec(memory_space=pltpu.SMEM),
                  pl.BlockSpec((1, 1, 2 * ts), lambda i: (i, 0, 0), memory_space=pltpu.SMEM),
                  pl.BlockSpec((ts, ROW_TILES, LANES), lambda i: (i, 0, 0))],
        out_specs=pl.BlockSpec(memory_space=pl.ANY),
        out_shape=jax.ShapeDtypeStruct((n_slots, ROW_TILES, LANES), F32),
        scratch_shapes=[pltpu.VMEM((MOE_TM // 2, ROW_TILES, LANES), F32),
                        pltpu.SemaphoreType.DMA(()), pltpu.SemaphoreType.DMA(())],
        compiler_params=_cparams(("arbitrary",)),
        name="moe_dispatch",
    )(pad_info, slots3, h2)


MOE_FC = 512


def _moe_kernel(te_ref, tv_ref, xs_ref, wg_ref, wu_ref, wd_ref, y_ref, x_sc):
    t = pl.program_id(0)

    @pl.when(tv_ref[t] == 0)
    def _():
        y_ref[...] = jnp.zeros_like(y_ref)

    @pl.when(tv_ref[t] != 0)
    def _():
        for c in range(ROW_TILES):
            x_sc[:, c * LANES:(c + 1) * LANES] = xs_ref[:, c, :].astype(BF16)
        x = x_sc[...]
        acc = jnp.zeros(x.shape, F32)
        for c in range(D_FF_EXPERT // MOE_FC):
            sl = slice(c * MOE_FC, (c + 1) * MOE_FC)
            gate = _dot(x, wg_ref[:, sl])
            up = _dot(x, wu_ref[:, sl])
            act = (gate * jax.nn.sigmoid(gate) * up).astype(BF16)
            acc = acc + _dot(act, wd_ref[sl, :])
        for c in range(ROW_TILES):
            y_ref[:, c, :] = acc[:, c * LANES:(c + 1) * LANES]


def _moe(tile_expert, tile_valid, xs, w_gu, w_down):
    n_slots = xs.shape[0]
    nt = n_slots // MOE_TM
    grid_spec = pltpu.PrefetchScalarGridSpec(
        num_scalar_prefetch=2,
        grid=(nt,),
        in_specs=[
            pl.BlockSpec((MOE_TM, ROW_TILES, LANES), lambda t, te, tv: (t, 0, 0)),
            pl.BlockSpec((None, D_MODEL, D_FF_EXPERT), lambda t, te, tv: (te[t], 0, 0),
                         pipeline_mode=pl.Buffered(1)),
            pl.BlockSpec((None, D_MODEL, D_FF_EXPERT), lambda t, te, tv: (te[t], 0, 1),
                         pipeline_mode=pl.Buffered(1)),
            pl.BlockSpec((None, D_FF_EXPERT, D_MODEL), lambda t, te, tv: (te[t], 0, 0),
                         pipeline_mode=pl.Buffered(1)),
        ],
        out_specs=pl.BlockSpec((MOE_TM, ROW_TILES, LANES), lambda t, te, tv: (t, 0, 0)),
        scratch_shapes=[pltpu.VMEM((MOE_TM, D_MODEL), BF16)],
    )
    return pl.pallas_call(
        _moe_kernel,
        grid_spec=grid_spec,
        out_shape=jax.ShapeDtypeStruct((n_slots, ROW_TILES, LANES), F32),
        compiler_params=_cparams(("arbitrary",)),
        name="moe_experts",
    )(tile_expert, tile_valid, xs, w_gu, w_gu, w_down)


def _combine_kernel(slot_ref, x3_ref, meta_ref, y_ref, o_ref, buf, sem, *, ts):
    def row_copy(r, k):
        return pltpu.make_async_copy(y_ref.at[slot_ref[0, 0, 2 * r + k]], buf.at[k, r], sem)

    def issue(r, c):
        row_copy(r, 0).start()
        row_copy(r, 1).start()
        return c

    lax.fori_loop(0, ts, issue, 0)

    def drain(r, c):
        row_copy(r, 0).wait()
        row_copy(r, 1).wait()
        return c

    lax.fori_loop(0, ts, drain, 0)
    meta = meta_ref[...]
    g1 = meta[:, 2:3]
    g2 = meta[:, 3:4]
    for c in range(ROW_TILES):
        sl = slice(c * LANES, (c + 1) * LANES)
        o_ref[:, sl] = x3_ref[:, sl] + g1 * buf[0, :, c, :] + g2 * buf[1, :, c, :]


def _combine(slots3, x3, meta, y, ts):
    S = x3.shape[0]
    return pl.pallas_call(
        functools.partial(_combine_kernel, ts=ts),
        grid=(S // ts,),
        in_specs=[pl.BlockSpec((1, 1, 2 * ts), lambda i: (i, 0, 0), memory_space=pltpu.SMEM),
                  pl.BlockSpec((ts, D_MODEL), lambda i: (i, 0)),
                  pl.BlockSpec((ts, LANES), lambda i: (i, 0)),
                  pl.BlockSpec(memory_space=pl.ANY)],
        out_specs=pl.BlockSpec((ts, D_MODEL), lambda i: (i, 0)),
        out_shape=jax.ShapeDtypeStruct((S, D_MODEL), F32),
        scratch_shapes=[pltpu.VMEM((2, ts, ROW_TILES, LANES), F32), pltpu.SemaphoreType.DMA(())],
        compiler_params=_cparams(("arbitrary",)),
        name="moe_combine",
    )(slots3, x3, meta, y)


def _tile_lanes(v, reps):
    return jnp.tile(v, reps)[None, :].astype(F32)


def _rope_tables_64(S):
    pos = jnp.arange(S, dtype=F32)
    inv = ROPE_THETA ** (-jnp.arange(0, A_HEAD_DIM, 2, dtype=F32) / A_HEAD_DIM)
    ang = pos[:, None] * inv[None, :]
    c, s = jnp.cos(ang), jnp.sin(ang)
    return jnp.concatenate([c, c, c, c], axis=1), jnp.concatenate([-s, s, -s, s], axis=1)


def _rope_tables_32(S):
    pos = jnp.arange(S, dtype=F32)
    inv = ROPE_THETA ** (-jnp.arange(0, C_ROPE, 2, dtype=F32) / C_ROPE)
    ang = pos[:, None] * inv[None, :]
    c, s = jnp.cos(ang), jnp.sin(ang)
    one = jnp.ones((S, C_NOPE), F32)
    zero = jnp.zeros((S, C_NOPE), F32)
    pad = jnp.zeros((S, LANES - C_QK), F32)
    return (jnp.concatenate([one, c, c, pad], axis=1),
            jnp.concatenate([zero, -s, s, pad], axis=1))


def _pad_heads(w, n_heads, width):
    k = w.shape[0]
    w = w.reshape(k, n_heads, width)
    return jnp.pad(w, ((0, 0), (0, 0), (0, LANES - width))).reshape(k, n_heads * LANES)


def _swap_rope_cols(w_p):
    k = w_p.shape[0]
    w = w_p.reshape(k, -1, LANES)
    half = C_ROPE // 2
    sw = jnp.concatenate([jnp.zeros_like(w[:, :, :C_NOPE]),
                          w[:, :, C_NOPE + half:C_QK], w[:, :, C_NOPE:C_NOPE + half],
                          jnp.zeros_like(w[:, :, C_QK:])], axis=2)
    return sw.reshape(k, -1)


def kernel(x, l0_norm_attn, l0_w_in, l0_a_q_norm, l0_a_k_norm, l0_a_lambda, l0_a_subln, l0_b_q_norm, l0_b_k_norm, l0_b_sinks, l0_w_out, l0_norm_ffn, l0_ffn_w_gu, l0_ffn_w_down, l1_norm_attn, l1_c_w_in, l1_c_q_lora_norm, l1_c_kv_lora_norm, l1_c_w_uq, l1_c_w_ukv, l1_c_q_norm, l1_c_k_norm, l1_c_w_out, l1_norm_ffn, l1_router, l1_exp_w_gu, l1_exp_w_down):
    B, S, _ = x.shape
    assert B == 1
    x2d = x.reshape(S, D_MODEL)
    tm = min(512, S)

    o_bk = 3 * A_W + B_QW
    o_bv = o_bk + B_KW
    bk_w = l0_w_in[:, o_bk:o_bv].reshape(D_MODEL, B_KV_HEADS, B_HEAD_DIM)
    bv_w = l0_w_in[:, o_bv:].reshape(D_MODEL, B_KV_HEADS, B_HEAD_DIM)
    dup = lambda w: jnp.concatenate([w, w], axis=2).reshape(D_MODEL, 2 * B_KW)
    w_in_p = jnp.concatenate([l0_w_in[:, :2 * A_W], l0_w_in[:, 3 * A_W:o_bk], dup(bk_w),
                              dup(bv_w)], axis=1).astype(BF16)
    w_avt = l0_w_in[:, 2 * A_W:3 * A_W].T.reshape(A_HEADS, 2 * A_HEAD_DIM, D_MODEL)
    w_avt = jnp.pad(w_avt, ((0, 0), (0, V_PAD), (0, 0))).reshape(-1, D_MODEL).astype(BF16)
    a_vone = jnp.zeros((A_HEADS, A_V_ROWS, 1), F32).at[:, 2 * A_HEAD_DIM, 0].set(1.0).reshape(-1, 1)
    gains = jnp.concatenate([_tile_lanes(l0_a_q_norm, 2), _tile_lanes(l0_a_k_norm, 2),
                             _tile_lanes(l0_b_q_norm, 2), _tile_lanes(l0_b_k_norm, 2),
                             jnp.zeros((4, LANES), F32)], axis=0)
    lane = jnp.arange(LANES)
    ones64 = (lane[:, None] // 64 == lane[None, :] // 64).astype(BF16)
    cos64, sin64 = _rope_tables_64(S)
    aq, ak, avt, bq, bk, bv = _l0_in(x2d, l0_norm_attn[None, :], w_in_p, w_avt, a_vone, gains,
                                     ones64, cos64, sin64, tm)
    lam_init = 0.8 - 0.6 * math.exp(-0.3 * 0)
    a_out = _a_attn(l0_a_lambda.astype(F32), aq, ak, avt, l0_a_subln[None, :].astype(F32),
                    lam_init)
    b_out = _b_attn(l0_b_sinks.astype(F32), bq, bk, bv)
    w_out = l0_w_out.astype(BF16)
    x2 = _l0_ffn(a_out, b_out, x2d, w_out[:A_W], w_out[A_W:], l0_norm_ffn[None, :],
                 l0_ffn_w_gu.astype(BF16), l0_ffn_w_down.astype(BF16), tm, D_FF // 2)

    wq = l1_c_w_in[:, :C_Q_RANK].astype(BF16)
    wkv = l1_c_w_in[:, C_Q_RANK:C_Q_RANK + C_KV_RANK].astype(BF16)
    wkr = jnp.pad(l1_c_w_in[:, C_Q_RANK + C_KV_RANK:], ((0, 0), (C_NOPE, LANES - C_QK)))
    wkrs = _swap_rope_cols(wkr).astype(BF16)
    wkr = wkr.astype(BF16)
    wuq = _pad_heads(l1_c_w_uq, C_HEADS, C_QK)
    wuqs = _swap_rope_cols(wuq).astype(BF16)
    wuq = wuq.astype(BF16)
    ukv = l1_c_w_ukv.reshape(C_KV_RANK, C_HEADS, C_NOPE + C_V)
    wuk = _pad_heads(ukv[:, :, :C_NOPE].reshape(C_KV_RANK, -1), C_HEADS, C_NOPE).astype(BF16)
    wuv = jnp.pad(jnp.transpose(ukv[:, :, C_NOPE:], (1, 2, 0)), ((0, 0), (0, V_PAD), (0, 0)))
    wuv = wuv.reshape(C_HEADS * V_ROWS, C_KV_RANK).astype(BF16)
    vone = jnp.zeros((C_HEADS, V_ROWS, 1), F32).at[:, C_V, 0].set(1.0).reshape(-1, 1)
    pad_gain = lambda gvec: jnp.pad(gvec.astype(F32), (0, LANES - C_QK))[None, :]
    qg, kg = pad_gain(l1_c_q_norm), pad_gain(l1_c_k_norm)
    qgs, kgs = _swap_rope_cols(qg), _swap_rope_cols(kg)
    ones128 = jnp.ones((LANES, LANES), BF16)
    cos32, sin32 = _rope_tables_32(S)
    cq, ck, cvt = _c_in(x2, l1_norm_attn[None, :], wq, wkv, wkr, wkrs,
                       l1_c_q_lora_norm[None, :], l1_c_kv_lora_norm[None, :],
                       wuq, wuqs, wuk, wuv, qg, qgs, kg, kgs, vone, ones128, cos32, sin32, tm)
    c_o = _c_attn(cq, ck, cvt)

    r_pad = jnp.pad(l1_router.astype(F32), ((0, 0), (0, LANES - N_EXPERTS)))
    r_hi = r_pad.astype(BF16)
    r_lo = (r_pad - r_hi.astype(F32)).astype(BF16)
    tr = min(256, S)
    ridx = jnp.arange(tr)
    tri = (ridx[None, :] < ridx[:, None]).astype(BF16)
    x3, h2, meta, cnt = _c_out_router(c_o, x2, l1_c_w_out.astype(BF16), l1_norm_ffn[None, :],
                                      r_hi, r_lo, tri, tr)

    counts = cnt[0, :N_EXPERTS].astype(jnp.int32)
    padded = ((counts + MOE_TM - 1) // MOE_TM) * MOE_TM
    ends = jnp.cumsum(padded)
    offs = ends - padded
    idx = meta[:, 0:2].astype(jnp.int32)
    pos = meta[:, 4:6].astype(jnp.int32)
    slots = offs[idx] + pos
    n_slots = 2 * S + N_EXPERTS * MOE_TM
    nt = n_slots // MOE_TM
    tile_start = jnp.arange(nt, dtype=jnp.int32) * MOE_TM
    n_valid = jnp.broadcast_to(ends[-1] // MOE_TM, (N_EXPERTS,))
    tile_valid = (tile_start < ends[-1]).astype(jnp.int32)
    tile_expert = jnp.minimum(
        jnp.sum((tile_start[:, None] >= ends[None, :]).astype(jnp.int32), axis=1),
        N_EXPERTS - 1).astype(jnp.int32)
    pad_info = jnp.stack([offs + counts, padded - counts, n_valid]).astype(jnp.int32)

    ts = min(256, S)
    slots3 = slots.reshape(S // ts, 1, 2 * ts)
    xs = _dispatch(pad_info, slots3, h2, n_slots, ts)
    y = _moe(tile_expert, tile_valid, xs,
             l1_exp_w_gu.astype(BF16), l1_exp_w_down.astype(BF16))
    out = _combine(slots3, x3, meta, y, ts)
    return out.reshape(B, S, D_MODEL)
```

```python
import functools
import math

import jax
import jax.numpy as jnp
from jax import lax
from jax.experimental import pallas as pl
from jax.experimental.pallas import tpu as pltpu

F32 = jnp.float32
BF16 = jnp.bfloat16

D_MODEL = 1024
CHUNK = 64
ROPE_THETA = 10000.0
EPS = 1e-6
LANES = 128
ROW_TILES = D_MODEL // LANES

A_HEADS = 4
A_HEAD_DIM = 64
B_HEADS = 8
B_KV_HEADS = 2
B_HEAD_DIM = 64
A_W = A_HEADS * 2 * A_HEAD_DIM
B_QW = B_HEADS * B_HEAD_DIM
B_KW = B_KV_HEADS * B_HEAD_DIM

C_HEADS = 16
C_Q_RANK = 256
C_KV_RANK = 128
C_NOPE = 64
C_ROPE = 32
C_V = 64
C_QK = C_NOPE + C_ROPE

D_FF = 2816
N_EXPERTS = 8
D_FF_EXPERT = 3584

NEG_BIG = -1e30
LOG2E = 1.0 / math.log(2.0)
VMEM_LIMIT = 56 * 1024 * 1024


def _cparams(sem):
    return pltpu.CompilerParams(dimension_semantics=sem, vmem_limit_bytes=VMEM_LIMIT)


def _dot(a, b):
    return jnp.dot(a, b, preferred_element_type=F32)


def _dot_nt(a, b):
    return lax.dot_general(a, b, (((1,), (1,)), ((), ())), preferred_element_type=F32)


def _rms(x, g):
    ms = jnp.mean(x * x, axis=-1, keepdims=True)
    return x * lax.rsqrt(ms + EPS) * g


def _l0_in_kernel(x_ref, g_ref, w_ref, wvt_ref, vone_ref, gains_ref, ones_ref, cos_ref, sin_ref,
                  aq_o, ak_o, avt_o, bq_o, bk_o, bv_o):
    h = _rms(x_ref[...], g_ref[...]).astype(BF16)
    z = _dot(h, w_ref[...])
    avt_o[...] = (_dot_nt(wvt_ref[...], h) + vone_ref[...]).astype(BF16)
    cos = cos_ref[...]
    sin = sin_ref[...]
    ones_blk = ones_ref[...]
    lane = lax.broadcasted_iota(jnp.int32, (1, LANES), 1)
    first_half = (lane % 64) < 32

    def norm_rope(zs, gain, scale):
        ss = _dot((zs * zs).astype(BF16), ones_blk)
        r = lax.rsqrt(ss * (1.0 / 64.0) + EPS) * scale
        y = zs * gain
        sw = jnp.where(first_half, pltpu.roll(y, 96, 1), pltpu.roll(y, 32, 1))
        return ((y * cos + sw * sin) * r).astype(BF16)

    col = 0
    gcol = 0
    scale = A_HEAD_DIM ** -0.5
    scales = {"aq": scale * LOG2E, "ak": 1.0, "bq": scale, "bk": 1.0}
    outs = {"aq": aq_o, "ak": ak_o, "bq": bq_o, "bk": bk_o}
    for name, nslab in (("aq", 4), ("ak", 4), ("bq", 4), ("bk", 2), ("bv", 2)):
        if name == "bv":
            bv_o[...] = z[:, col:col + 2 * LANES].astype(BF16)
            col += 2 * LANES
            continue
        sc = scales[name]
        gain = gains_ref[gcol:gcol + 1, :]
        gcol += 1
        for s in range(nslab):
            zs = z[:, col:col + LANES]
            outs[name][:, s * LANES:(s + 1) * LANES] = norm_rope(zs, gain, sc)
            col += LANES


def _l0_in(x2d, g, w_in_p, w_avt, vone, gains, ones_blk, cos_t, sin_t, tm):
    S = x2d.shape[0]
    nw = w_in_p.shape[1]
    row = lambda i: (i, 0)
    const = lambda i: (0, 0)
    out_shapes = (
        jax.ShapeDtypeStruct((S, A_W), BF16), jax.ShapeDtypeStruct((S, A_W), BF16),
        jax.ShapeDtypeStruct((A_HEADS * A_V_ROWS, S), BF16), jax.ShapeDtypeStruct((S, B_QW), BF16),
        jax.ShapeDtypeStruct((S, 2 * B_KW), BF16), jax.ShapeDtypeStruct((S, 2 * B_KW), BF16))
    return pl.pallas_call(
        _l0_in_kernel,
        grid=(S // tm,),
        in_specs=[
            pl.BlockSpec((tm, D_MODEL), row),
            pl.BlockSpec((1, D_MODEL), const),
            pl.BlockSpec((D_MODEL, nw), const),
            pl.BlockSpec((A_HEADS * A_V_ROWS, D_MODEL), const),
            pl.BlockSpec((A_HEADS * A_V_ROWS, 1), const),
            pl.BlockSpec((8, LANES), const),
            pl.BlockSpec((LANES, LANES), const),
            pl.BlockSpec((tm, LANES), row),
            pl.BlockSpec((tm, LANES), row),
        ],
        out_specs=[
            pl.BlockSpec((tm, A_W), row), pl.BlockSpec((tm, A_W), row),
            pl.BlockSpec((A_HEADS * A_V_ROWS, tm), lambda i: (0, i)), pl.BlockSpec((tm, B_QW), row),
            pl.BlockSpec((tm, 2 * B_KW), row), pl.BlockSpec((tm, 2 * B_KW), row)],
        out_shape=out_shapes,
        compiler_params=_cparams(("arbitrary",)),
        name="l0_in_proj",
    )(x2d, g, w_in_p, w_avt, vone, gains, ones_blk, cos_t, sin_t)


ATT_TQ = 512
ATT_TK = 256
V_PAD = 16


def _softmax_pv(s_ref, s_max, vt, m, acc):
    m_new = jnp.maximum(m, s_max)
    alpha = jnp.exp2(m - m_new)
    p = jnp.exp2(s_ref[...] - m_new).astype(BF16)
    return m_new, alpha * acc + _dot(vt, p)


def _diag_masks():
    key_chunk = lax.broadcasted_iota(jnp.int32, (ATT_TK, ATT_TQ), 0) // CHUNK
    query_chunk = lax.broadcasted_iota(jnp.int32, (ATT_TK, ATT_TQ), 1) // CHUNK
    return [key_chunk + b * (ATT_TK // CHUNK) <= query_chunk for b in range(ATT_TQ // ATT_TK)]


def _attn_pipeline(i, k_ref, vt_ref, streams, v_rows):
    n = len(streams)

    for st in streams:
        st[5][...] = st[0].T

    def scores(st, blk, mask, dst):
        lanes, qt_ref = streams[st][1], streams[st][5]
        off = pl.multiple_of(blk * ATT_TK, ATT_TK)
        s = _dot(k_ref[pl.ds(off, ATT_TK), lanes], qt_ref[...])
        if mask is not None:
            s = jnp.where(mask, s, NEG_BIG)
        dst[...] = s
        return jnp.max(s, axis=0, keepdims=True)

    def consume(st, blk, src, s_max, state):
        off = pl.multiple_of(blk * ATT_TK, ATT_TK)
        return _softmax_pv(src, s_max, vt_ref[streams[st][2], pl.ds(off, ATT_TK)], *state)

    buf_a = [st[3] for st in streams]
    buf_b = [st[4] for st in streams]
    mask0, mask1 = _diag_masks()
    d0 = 2 * i
    d1 = d0 + 1
    state = [(jnp.full((1, ATT_TQ), NEG_BIG, F32), jnp.zeros((v_rows, ATT_TQ), F32))] * n
    max_a = [scores(st, d0, mask0, buf_a[st]) for st in range(n)]
    max_b = [None] * n
    for st in range(n):
        max_b[st] = scores(st, d1, mask1, buf_b[st])
        state[st] = consume(st, d0, buf_a[st], max_a[st], state[st])
    for st in range(n):
        max_a[st] = scores(st, 0, None, buf_a[st])
        state[st] = consume(st, d1, buf_b[st], max_b[st], state[st])

    def body(p, carry):
        carry, max_a = list(carry[0]), list(carry[1])
        max_b = [None] * n
        u0 = 2 * p
        u1 = u0 + 1
        nxt = jnp.minimum(u0 + 2, d0 - 1)
        for st in range(n):
            max_b[st] = scores(st, u1, None, buf_b[st])
            carry[st] = consume(st, u0, buf_a[st], max_a[st], carry[st])
        for st in range(n):
            max_a[st] = scores(st, nxt, None, buf_a[st])
            carry[st] = consume(st, u1, buf_b[st], max_b[st], carry[st])
        return tuple(carry), tuple(max_a)

    final, _ = lax.fori_loop(0, i, body, (tuple(state), tuple(max_a)))
    return [acc for _, acc in final]


_SCORE_BUF = pltpu.VMEM((ATT_TK, ATT_TQ), F32)
_QT_BUF = pltpu.VMEM((LANES, ATT_TQ), BF16)
_ATTN_SCRATCH = [_SCORE_BUF] * 4 + [_QT_BUF] * 2


A_V_ROWS = 2 * A_HEAD_DIM + V_PAD


def _a_attn_kernel(lam_ref, q_ref, k_ref, vt_ref, subln_ref, o_ref, s1a, s1b, s2a, s2b,
                   qt1, qt2, *, lam_init):
    i = pl.program_id(1)
    lane = lax.broadcasted_iota(jnp.int32, (1, LANES), 1)
    q = q_ref[...]
    zero = jnp.zeros_like(q)
    all_lanes = slice(0, LANES)
    all_rows = slice(0, A_V_ROWS)
    streams = ((jnp.where(lane < 64, q, zero), all_lanes, all_rows, s1a, s1b, qt1),
               (jnp.where(lane >= 64, q, zero), all_lanes, all_rows, s2a, s2b, qt2))
    acc1, acc2 = _attn_pipeline(i, k_ref, vt_ref, streams, A_V_ROWS)
    dv = 2 * A_HEAD_DIM
    lf = lam_ref[...]
    lam = (jnp.exp(jnp.sum(lf[0:1] * lf[1:2], axis=1, keepdims=True))
           - jnp.exp(jnp.sum(lf[2:3] * lf[3:4], axis=1, keepdims=True)) + lam_init)
    out = (acc1[:dv] / acc1[dv:dv + 1] - lam * (acc2[:dv] / acc2[dv:dv + 1])).T
    out = _rms(out, subln_ref[...]) * (1.0 - lam_init)
    o_ref[...] = out.astype(BF16)


def _a_attn(lam_p, aq, ak, avt, subln, lam_init):
    S = aq.shape[0]
    return pl.pallas_call(
        functools.partial(_a_attn_kernel, lam_init=lam_init),
        grid=(A_HEADS, S // ATT_TQ),
        in_specs=[
            pl.BlockSpec((4, A_HEAD_DIM), lambda h, i: (0, 0)),
            pl.BlockSpec((ATT_TQ, LANES), lambda h, i: (i, h)),
            pl.BlockSpec((S, LANES), lambda h, i: (0, h)),
            pl.BlockSpec((A_V_ROWS, S), lambda h, i: (h, 0)),
            pl.BlockSpec((1, LANES), lambda h, i: (0, 0)),
        ],
        out_specs=pl.BlockSpec((ATT_TQ, LANES), lambda h, i: (i, h)),
        out_shape=jax.ShapeDtypeStruct((S, A_W), BF16),
        scratch_shapes=_ATTN_SCRATCH,
        compiler_params=_cparams(("arbitrary", "arbitrary")),
        name="a_diff_attn",
    )(lam_p, aq, ak, avt, subln)


B_BLK = 128


def _b_attn_kernel(sink_ref, q_ref, kp_ref, kc_ref, vp_ref, vc_ref, o_ref):
    i = pl.program_id(0)
    lane = lax.broadcasted_iota(jnp.int32, (1, LANES), 1)
    lo = lane < 64
    r = lax.broadcasted_iota(jnp.int32, (B_BLK, 2 * B_BLK), 0) // CHUNK
    c_idx = lax.broadcasted_iota(jnp.int32, (B_BLK, 2 * B_BLK), 1)
    c = c_idx // CHUNK
    mask = (c >= r) & (c <= r + 2) & ((c_idx >= B_BLK) | (i > 0))
    for slab in range(B_HEADS // 2):
        g = slab // 2
        k = jnp.concatenate([kp_ref[:, g * LANES:(g + 1) * LANES],
                             kc_ref[:, g * LANES:(g + 1) * LANES]], axis=0)
        v = jnp.concatenate([vp_ref[:, g * LANES:(g + 1) * LANES],
                             vc_ref[:, g * LANES:(g + 1) * LANES]], axis=0)
        qs = q_ref[:, slab * LANES:(slab + 1) * LANES]
        zq = jnp.zeros_like(qs)
        zv = jnp.zeros_like(v)
        out = jnp.zeros((B_BLK, LANES), F32)
        for half in range(2):
            sel = lo if half == 0 else jnp.logical_not(lo)
            sink = sink_ref[2 * slab + half]
            s = _dot_nt(jnp.where(sel, qs, zq), k)
            s = jnp.where(mask, s, NEG_BIG)
            m = jnp.maximum(jnp.max(s, axis=1, keepdims=True), sink)
            e = jnp.exp(s - m)
            denom = jnp.sum(e, axis=1, keepdims=True) + jnp.exp(sink - m)
            p = (e / denom).astype(BF16)
            out = out + _dot(p, jnp.where(sel, v, zv))
        o_ref[:, slab * LANES:(slab + 1) * LANES] = out.astype(BF16)


def _b_attn(sinks, bq, bk, bv):
    S = bq.shape[0]
    prev = lambda i: (jnp.maximum(i - 1, 0), 0)
    cur = lambda i: (i, 0)
    return pl.pallas_call(
        _b_attn_kernel,
        grid=(S // B_BLK,),
        in_specs=[
            pl.BlockSpec(memory_space=pltpu.SMEM),
            pl.BlockSpec((B_BLK, B_QW), cur),
            pl.BlockSpec((B_BLK, 2 * B_KW), prev),
            pl.BlockSpec((B_BLK, 2 * B_KW), cur),
            pl.BlockSpec((B_BLK, 2 * B_KW), prev),
            pl.BlockSpec((B_BLK, 2 * B_KW), cur),
        ],
        out_specs=pl.BlockSpec((B_BLK, B_QW), cur),
        out_shape=jax.ShapeDtypeStruct((S, B_QW), BF16),
        compiler_params=_cparams(("arbitrary",)),
        name="b_swa_attn",
    )(sinks, bq, bk, bk, bv, bv)


def _l0_ffn_kernel(a_ref, b_ref, x_ref, woa_ref, wob_ref, g_ref, wg_ref, wu_ref, wd_ref,
                   o_ref, h_sc, acc_sc):
    j = pl.program_id(1)

    @pl.when(j == 0)
    def _():
        x1 = x_ref[...] + _dot(a_ref[...], woa_ref[...]) + _dot(b_ref[...], wob_ref[...])
        acc_sc[...] = x1
        h_sc[...] = _rms(x1, g_ref[...]).astype(BF16)

    h = h_sc[...]
    gate = _dot(h, wg_ref[...])
    up = _dot(h, wu_ref[...])
    act = (gate * jax.nn.sigmoid(gate) * up).astype(BF16)
    acc_sc[...] += _dot(act, wd_ref[...])

    @pl.when(j == pl.num_programs(1) - 1)
    def _():
        o_ref[...] = acc_sc[...]


def _l0_ffn(a_out, b_out, x2d, wo_a, wo_b, g, w_gu, w_down, tm, tf):
    S = x2d.shape[0]
    nf = D_FF // tf
    row = lambda i, j: (i, 0)
    const = lambda i, j: (0, 0)
    return pl.pallas_call(
        _l0_ffn_kernel,
        grid=(S // tm, nf),
        in_specs=[
            pl.BlockSpec((tm, A_W), row),
            pl.BlockSpec((tm, B_QW), row),
            pl.BlockSpec((tm, D_MODEL), row),
            pl.BlockSpec((A_W, D_MODEL), const),
            pl.BlockSpec((B_QW, D_MODEL), const),
            pl.BlockSpec((1, D_MODEL), const),
            pl.BlockSpec((D_MODEL, tf), lambda i, j: (0, j)),
            pl.BlockSpec((D_MODEL, tf), lambda i, j: (0, nf + j)),
            pl.BlockSpec((tf, D_MODEL), lambda i, j: (j, 0)),
        ],
        out_specs=pl.BlockSpec((tm, D_MODEL), row),
        out_shape=jax.ShapeDtypeStruct((S, D_MODEL), F32),
        scratch_shapes=[pltpu.VMEM((tm, D_MODEL), BF16), pltpu.VMEM((tm, D_MODEL), F32)],
        compiler_params=_cparams(("arbitrary", "arbitrary")),
        name="l0_out_ffn",
    )(a_out, b_out, x2d, wo_a, wo_b, g, w_gu, w_gu, w_down)


def _c_in_kernel(x_ref, g_ref, wq_ref, wkv_ref, wkr_ref, wkrs_ref, gql_ref, gkvl_ref,
                 wuq_ref, wuqs_ref, wuk_ref, wuv_ref, qg_ref, qgs_ref, kg_ref, kgs_ref,
                 vone_ref, ones_ref, cos_ref, sin_ref, q_o, k_o, vt_o):
    h = _rms(x_ref[...], g_ref[...]).astype(BF16)
    cq = _rms(_dot(h, wq_ref[...]), gql_ref[...]).astype(BF16)
    ckv = _rms(_dot(h, wkv_ref[...]), gkvl_ref[...]).astype(BF16)
    kr = _dot(h, wkr_ref[...])
    krs = _dot(h, wkrs_ref[...])
    q = _dot(cq, wuq_ref[...])
    qs = _dot(cq, wuqs_ref[...])
    kn = _dot(ckv, wuk_ref[...])
    vt_o[...] = (_dot_nt(wuv_ref[...], ckv) + vone_ref[...]).astype(BF16)

    cos = cos_ref[...]
    sin = sin_ref[...]
    ones_blk = ones_ref[...]
    qg, qgs, kg, kgs = qg_ref[...], qgs_ref[...], kg_ref[...], kgs_ref[...]
    ss_kr = _dot((kr * kr).astype(BF16), ones_blk)
    kr_roped = kr * kg * cos + krs * kgs * sin
    scale = C_QK ** -0.5 * LOG2E
    inv = 1.0 / C_QK
    for hd in range(C_HEADS):
        sl = slice(hd * LANES, (hd + 1) * LANES)
        qh = q[:, sl]
        r = lax.rsqrt(_dot((qh * qh).astype(BF16), ones_blk) * inv + EPS) * scale
        q_o[:, sl] = ((qh * qg * cos + qs[:, sl] * qgs * sin) * r).astype(BF16)
        kh = kn[:, sl]
        rk = lax.rsqrt((_dot((kh * kh).astype(BF16), ones_blk) + ss_kr) * inv + EPS)
        k_o[:, sl] = ((kh * kg + kr_roped) * rk).astype(BF16)


def _c_in(x2d, g, wq, wkv, wkr, wkrs, gql, gkvl, wuq, wuqs, wuk, wuv, qg, qgs, kg, kgs,
          vone, ones_blk, cos_t, sin_t, tm):
    S = x2d.shape[0]
    row = lambda i: (i, 0)
    const = lambda i: (0, 0)
    full = lambda a: pl.BlockSpec(a.shape, const)
    W = C_HEADS * LANES
    return pl.pallas_call(
        _c_in_kernel,
        grid=(S // tm,),
        in_specs=[pl.BlockSpec((tm, D_MODEL), row), full(g), full(wq), full(wkv), full(wkr),
                  full(wkrs), full(gql), full(gkvl), full(wuq), full(wuqs), full(wuk),
                  full(wuv), full(qg), full(qgs), full(kg), full(kgs), full(vone),
                  full(ones_blk),
                  pl.BlockSpec((tm, LANES), row), pl.BlockSpec((tm, LANES), row)],
        out_specs=[pl.BlockSpec((tm, W), row), pl.BlockSpec((tm, W), row),
                   pl.BlockSpec((C_HEADS * V_ROWS, tm), lambda i: (0, i))],
        out_shape=(jax.ShapeDtypeStruct((S, W), BF16), jax.ShapeDtypeStruct((S, W), BF16),
                   jax.ShapeDtypeStruct((C_HEADS * V_ROWS, S), BF16)),
        compiler_params=_cparams(("arbitrary",)),
        name="c_in_proj",
    )(x2d, g, wq, wkv, wkr, wkrs, gql, gkvl, wuq, wuqs, wuk, wuv, qg, qgs, kg, kgs,
      vone, ones_blk, cos_t, sin_t)


V_ROWS = C_V + V_PAD


def _c_attn_kernel(q_ref, k_ref, vt_ref, o_ref, sa0, sb0, sa1, sb1, qt0, qt1):
    i = pl.program_id(1)
    streams = (
        (q_ref[:, :LANES], slice(0, LANES), slice(0, V_ROWS), sa0, sb0, qt0),
        (q_ref[:, LANES:], slice(LANES, 2 * LANES), slice(V_ROWS, 2 * V_ROWS), sa1, sb1, qt1))
    acc_a, acc_b = _attn_pipeline(i, k_ref, vt_ref, streams, V_ROWS)
    out = jnp.concatenate([acc_a[:C_V] / acc_a[C_V:C_V + 1], acc_b[:C_V] / acc_b[C_V:C_V + 1]],
                          axis=0)
    o_ref[...] = out.T.astype(BF16)


def _c_attn(q, k, vt):
    S = q.shape[0]
    return pl.pallas_call(
        _c_attn_kernel,
        grid=(C_HEADS // 2, S // ATT_TQ),
        in_specs=[
            pl.BlockSpec((ATT_TQ, 2 * LANES), lambda h, i: (i, h)),
            pl.BlockSpec((S, 2 * LANES), lambda h, i: (0, h)),
            pl.BlockSpec((2 * V_ROWS, S), lambda h, i: (h, 0)),
        ],
        out_specs=pl.BlockSpec((ATT_TQ, 2 * C_V), lambda h, i: (i, h)),
        out_shape=jax.ShapeDtypeStruct((S, C_HEADS * C_V), BF16),
        scratch_shapes=_ATTN_SCRATCH,
        compiler_params=_cparams(("arbitrary", "arbitrary")),
        name="c_mla_attn",
    )(q, k, vt)


def _c_out_router_kernel(o_ref, x_ref, wo_ref, g_ref, rhi_ref, rlo_ref, tri_ref,
                         x3_o, h_o, meta_o, cnt_o, run_sc):
    i = pl.program_id(0)

    @pl.when(i == 0)
    def _():
        run_sc[...] = jnp.zeros_like(run_sc)

    x3 = x_ref[...] + _dot(o_ref[...], wo_ref[...])
    x3_o[...] = x3
    h = _rms(x3, g_ref[...])
    for c in range(ROW_TILES):
        h_o[:, c, :] = h[:, c * LANES:(c + 1) * LANES]
    h_hi = h.astype(BF16)
    h_lo = (h - h_hi.astype(F32)).astype(BF16)
    logits = (_dot(h_hi, rhi_ref[...]) + _dot(h_hi, rlo_ref[...])) + _dot(h_lo, rhi_ref[...])
    tm = logits.shape[0]
    lane = lax.broadcasted_iota(jnp.int32, (tm, LANES), 1)
    lanef = lane.astype(F32)
    logits = jnp.where(lane < N_EXPERTS, logits, -jnp.inf)
    v1 = jnp.max(logits, axis=1, keepdims=True)
    i1 = jnp.min(jnp.where(logits == v1, lanef, float(LANES)), axis=1, keepdims=True)
    m1 = lanef == i1
    rest = jnp.where(m1, -jnp.inf, logits)
    v2 = jnp.max(rest, axis=1, keepdims=True)
    i2 = jnp.min(jnp.where(rest == v2, lanef, float(LANES)), axis=1, keepdims=True)
    m2 = lanef == i2
    e = jnp.exp(v2 - v1)
    g1 = 1.0 / (1.0 + e)
    g2 = e / (1.0 + e)
    chosen = jnp.where(m1 | m2, 1.0, 0.0)
    before = _dot(tri_ref[...], chosen.astype(BF16)) + run_sc[0:1, :]
    p1 = jnp.sum(jnp.where(m1, before, 0.0), axis=1, keepdims=True)
    p2 = jnp.sum(jnp.where(m2, before, 0.0), axis=1, keepdims=True)
    run_sc[...] = run_sc[...] + jnp.sum(chosen, axis=0, keepdims=True)
    meta = jnp.where(lane == 0, i1, 0.0)
    meta = jnp.where(lane == 1, i2, meta)
    meta = jnp.where(lane == 2, g1, meta)
    meta = jnp.where(lane == 3, g2, meta)
    meta = jnp.where(lane == 4, p1, meta)
    meta = jnp.where(lane == 5, p2, meta)
    meta_o[...] = meta
    cnt_o[...] = run_sc[...]


def _c_out_router(o, x2d, wo, g, rhi, rlo, tri, tm):
    S = x2d.shape[0]
    row = lambda i: (i, 0)
    const = lambda i: (0, 0)
    return pl.pallas_call(
        _c_out_router_kernel,
        grid=(S // tm,),
        in_specs=[pl.BlockSpec((tm, D_MODEL), row), pl.BlockSpec((tm, D_MODEL), row),
                  pl.BlockSpec((D_MODEL, D_MODEL), const), pl.BlockSpec((1, D_MODEL), const),
                  pl.BlockSpec((D_MODEL, LANES), const), pl.BlockSpec((D_MODEL, LANES), const),
                  pl.BlockSpec((tm, tm), const)],
        out_specs=[pl.BlockSpec((tm, D_MODEL), row),
                   pl.BlockSpec((tm, ROW_TILES, LANES), lambda i: (i, 0, 0)),
                   pl.BlockSpec((tm, LANES), row), pl.BlockSpec((8, LANES), const)],
        out_shape=(jax.ShapeDtypeStruct((S, D_MODEL), F32),
                   jax.ShapeDtypeStruct((S, ROW_TILES, LANES), F32),
                   jax.ShapeDtypeStruct((S, LANES), F32), jax.ShapeDtypeStruct((8, LANES), F32)),
        scratch_shapes=[pltpu.VMEM((8, LANES), F32)],
        compiler_params=_cparams(("arbitrary",)),
        name="c_out_router",
    )(o, x2d, wo, g, rhi, rlo, tri)


MOE_TM = 256
PAD_PIECES = (128, 64, 32, 16, 8, 4, 2, 1)


def _dispatch_kernel(pad_ref, slot_ref, h_ref, xs_ref, zero_sc, sem, zsem, *, ts):
    i = pl.program_id(0)

    @pl.when(i == 0)
    def _():
        zero_sc[...] = jnp.zeros_like(zero_sc)
        for e in range(N_EXPERTS):
            start = pad_ref[0, e]
            npad = pad_ref[1, e]
            for p in PAD_PIECES:
                hit = (npad & p) != 0

                @pl.when(hit)
                def _(start=start, p=p):
                    cp = pltpu.make_async_copy(zero_sc.at[pl.ds(0, p)],
                                               xs_ref.at[pl.ds(start, p)], zsem)
                    cp.start()
                    cp.wait()

                start = start + jnp.where(hit, p, 0)

        half = MOE_TM // 2

        def zero_tile(tile, c):
            for part in range(2):
                cp = pltpu.make_async_copy(
                    zero_sc, xs_ref.at[pl.ds(tile * MOE_TM + part * half, half)], zsem)
                cp.start()
                cp.wait()
            return c

        lax.fori_loop(pad_ref[2, 0], xs_ref.shape[0] // MOE_TM, zero_tile, 0)

    def row_copy(r, k):
        return pltpu.make_async_copy(h_ref.at[r], xs_ref.at[slot_ref[0, 0, 2 * r + k]], sem)

    def issue(r, c):
        row_copy(r, 0).start()
        row_copy(r, 1).start()
        return c

    lax.fori_loop(0, ts, issue, 0, unroll=8)
    for _ in range(2):
        pltpu.make_async_copy(h_ref, xs_ref.at[pl.ds(0, ts)], sem).wait()


def _dispatch(pad_info, slots3, h2, n_slots, ts):
    S = h2.shape[0]
    return pl.pallas_call(
        functools.partial(_dispatch_kernel, ts=ts),
        grid=(S // ts,),
        in_specs=[pl.BlockSpec(memory_space=pltpu.SMEM),
                  pl.BlockSpec((1, 1, 2 * ts), lambda i: (i, 0, 0), memory_space=pltpu.SMEM),
                  pl.BlockSpec((ts, ROW_TILES, LANES), lambda i: (i, 0, 0))],
        out_specs=pl.BlockSpec(memory_space=pl.ANY),
        out_shape=jax.ShapeDtypeStruct((n_slots, ROW_TILES, LANES), F32),
        scratch_shapes=[pltpu.VMEM((MOE_TM // 2, ROW_TILES, LANES), F32),
                        pltpu.SemaphoreType.DMA(()), pltpu.SemaphoreType.DMA(())],
        compiler_params=_cparams(("arbitrary",)),
        name="moe_dispatch",
    )(pad_info, slots3, h2)


MOE_FC = 512


def _moe_kernel(te_ref, tv_ref, xs_ref, wg_ref, wu_ref, wd_ref, y_ref, x_sc):
    t = pl.program_id(0)

    @pl.when(tv_ref[t] == 0)
    def _():
        y_ref[...] = jnp.zeros_like(y_ref)

    @pl.when(tv_ref[t] != 0)
    def _():
        for c in range(ROW_TILES):
            x_sc[:, c * LANES:(c + 1) * LANES] = xs_ref[:, c, :].astype(BF16)
        x = x_sc[...]
        acc = jnp.zeros(x.shape, F32)
        for c in range(D_FF_EXPERT // MOE_FC):
            sl = slice(c * MOE_FC, (c + 1) * MOE_FC)
            gate = _dot(x, wg_ref[:, sl])
            up = _dot(x, wu_ref[:, sl])
            act = (gate * jax.nn.sigmoid(gate) * up).astype(BF16)
            acc = acc + _dot(act, wd_ref[sl, :])
        for c in range(ROW_TILES):
            y_ref[:, c, :] = acc[:, c * LANES:(c + 1) * LANES]


def _moe(tile_expert, tile_valid, xs, w_gu, w_down):
    n_slots = xs.shape[0]
    nt = n_slots // MOE_TM
    grid_spec = pltpu.PrefetchScalarGridSpec(
        num_scalar_prefetch=2,
        grid=(nt,),
        in_specs=[
            pl.BlockSpec((MOE_TM, ROW_TILES, LANES), lambda t, te, tv: (t, 0, 0)),
            pl.BlockSpec((None, D_MODEL, D_FF_EXPERT), lambda t, te, tv: (te[t], 0, 0),
                         pipeline_mode=pl.Buffered(1)),
            pl.BlockSpec((None, D_MODEL, D_FF_EXPERT), lambda t, te, tv: (te[t], 0, 1),
                         pipeline_mode=pl.Buffered(1)),
            pl.BlockSpec((None, D_FF_EXPERT, D_MODEL), lambda t, te, tv: (te[t], 0, 0),
                         pipeline_mode=pl.Buffered(1)),
        ],
        out_specs=pl.BlockSpec((MOE_TM, ROW_TILES, LANES), lambda t, te, tv: (t, 0, 0)),
        scratch_shapes=[pltpu.VMEM((MOE_TM, D_MODEL), BF16)],
    )
    return pl.pallas_call(
        _moe_kernel,
        grid_spec=grid_spec,
        out_shape=jax.ShapeDtypeStruct((n_slots, ROW_TILES, LANES), F32),
        compiler_params=_cparams(("arbitrary",)),
        name="moe_experts",
    )(tile_expert, tile_valid, xs, w_gu, w_gu, w_down)


def _combine_kernel(slot_ref, x3_ref, meta_ref, y_ref, o_ref, buf, sem, *, ts):
    def row_copy(r, k):
        return pltpu.make_async_copy(y_ref.at[slot_ref[0, 0, 2 * r + k]], buf.at[k, r], sem)

    def issue(r, c):
        row_copy(r, 0).start()
        row_copy(r, 1).start()
        return c

    lax.fori_loop(0, ts, issue, 0, unroll=8)
    for k in range(2):
        pltpu.make_async_copy(y_ref.at[pl.ds(0, ts)], buf.at[k], sem).wait()
    meta = meta_ref[...]
    g1 = meta[:, 2:3]
    g2 = meta[:, 3:4]
    for c in range(ROW_TILES):
        sl = slice(c * LANES, (c + 1) * LANES)
        o_ref[:, sl] = x3_ref[:, sl] + g1 * buf[0, :, c, :] + g2 * buf[1, :, c, :]


def _combine(slots3, x3, meta, y, ts):
    S = x3.shape[0]
    return pl.pallas_call(
        functools.partial(_combine_kernel, ts=ts),
        grid=(S // ts,),
        in_specs=[pl.BlockSpec((1, 1, 2 * ts), lambda i: (i, 0, 0), memory_space=pltpu.SMEM),
                  pl.BlockSpec((ts, D_MODEL), lambda i: (i, 0)),
                  pl.BlockSpec((ts, LANES), lambda i: (i, 0)),
                  pl.BlockSpec(memory_space=pl.ANY)],
        out_specs=pl.BlockSpec((ts, D_MODEL), lambda i: (i, 0)),
        out_shape=jax.ShapeDtypeStruct((S, D_MODEL), F32),
        scratch_shapes=[pltpu.VMEM((2, ts, ROW_TILES, LANES), F32), pltpu.SemaphoreType.DMA(())],
        compiler_params=_cparams(("arbitrary",)),
        name="moe_combine",
    )(slots3, x3, meta, y)


def _tile_lanes(v, reps):
    return jnp.tile(v, reps)[None, :].astype(F32)


def _rope_tables_64(S):
    pos = jnp.arange(S, dtype=F32)
    inv = ROPE_THETA ** (-jnp.arange(0, A_HEAD_DIM, 2, dtype=F32) / A_HEAD_DIM)
    ang = pos[:, None] * inv[None, :]
    c, s = jnp.cos(ang), jnp.sin(ang)
    return jnp.concatenate([c, c, c, c], axis=1), jnp.concatenate([-s, s, -s, s], axis=1)


def _rope_tables_32(S):
    pos = jnp.arange(S, dtype=F32)
    inv = ROPE_THETA ** (-jnp.arange(0, C_ROPE, 2, dtype=F32) / C_ROPE)
    ang = pos[:, None] * inv[None, :]
    c, s = jnp.cos(ang), jnp.sin(ang)
    one = jnp.ones((S, C_NOPE), F32)
    zero = jnp.zeros((S, C_NOPE), F32)
    pad = jnp.zeros((S, LANES - C_QK), F32)
    return (jnp.concatenate([one, c, c, pad], axis=1),
            jnp.concatenate([zero, -s, s, pad], axis=1))


def _pad_heads(w, n_heads, width):
    k = w.shape[0]
    w = w.reshape(k, n_heads, width)
    return jnp.pad(w, ((0, 0), (0, 0), (0, LANES - width))).reshape(k, n_heads * LANES)


def _swap_rope_cols(w_p):
    k = w_p.shape[0]
    w = w_p.reshape(k, -1, LANES)
    half = C_ROPE // 2
    sw = jnp.concatenate([jnp.zeros_like(w[:, :, :C_NOPE]),
                          w[:, :, C_NOPE + half:C_QK], w[:, :, C_NOPE:C_NOPE + half],
                          jnp.zeros_like(w[:, :, C_QK:])], axis=2)
    return sw.reshape(k, -1)


def kernel(x, l0_norm_attn, l0_w_in, l0_a_q_norm, l0_a_k_norm, l0_a_lambda, l0_a_subln, l0_b_q_norm, l0_b_k_norm, l0_b_sinks, l0_w_out, l0_norm_ffn, l0_ffn_w_gu, l0_ffn_w_down, l1_norm_attn, l1_c_w_in, l1_c_q_lora_norm, l1_c_kv_lora_norm, l1_c_w_uq, l1_c_w_ukv, l1_c_q_norm, l1_c_k_norm, l1_c_w_out, l1_norm_ffn, l1_router, l1_exp_w_gu, l1_exp_w_down):
    B, S, _ = x.shape
    assert B == 1
    x2d = x.reshape(S, D_MODEL)
    tm = min(512, S)

    o_bk = 3 * A_W + B_QW
    o_bv = o_bk + B_KW
    bk_w = l0_w_in[:, o_bk:o_bv].reshape(D_MODEL, B_KV_HEADS, B_HEAD_DIM)
    bv_w = l0_w_in[:, o_bv:].reshape(D_MODEL, B_KV_HEADS, B_HEAD_DIM)
    dup = lambda w: jnp.concatenate([w, w], axis=2).reshape(D_MODEL, 2 * B_KW)
    w_in_p = jnp.concatenate([l0_w_in[:, :2 * A_W], l0_w_in[:, 3 * A_W:o_bk], dup(bk_w),
                              dup(bv_w)], axis=1).astype(BF16)
    w_avt = l0_w_in[:, 2 * A_W:3 * A_W].T.reshape(A_HEADS, 2 * A_HEAD_DIM, D_MODEL)
    w_avt = jnp.pad(w_avt, ((0, 0), (0, V_PAD), (0, 0))).reshape(-1, D_MODEL).astype(BF16)
    a_vone = jnp.zeros((A_HEADS, A_V_ROWS, 1), F32).at[:, 2 * A_HEAD_DIM, 0].set(1.0).reshape(-1, 1)
    gains = jnp.concatenate([_tile_lanes(l0_a_q_norm, 2), _tile_lanes(l0_a_k_norm, 2),
                             _tile_lanes(l0_b_q_norm, 2), _tile_lanes(l0_b_k_norm, 2),
                             jnp.zeros((4, LANES), F32)], axis=0)
    lane = jnp.arange(LANES)
    ones64 = (lane[:, None] // 64 == lane[None, :] // 64).astype(BF16)
    cos64, sin64 = _rope_tables_64(S)
    aq, ak, avt, bq, bk, bv = _l0_in(x2d, l0_norm_attn[None, :], w_in_p, w_avt, a_vone, gains,
                                     ones64, cos64, sin64, tm)
    lam_init = 0.8 - 0.6 * math.exp(-0.3 * 0)
    a_out = _a_attn(l0_a_lambda.astype(F32), aq, ak, avt, l0_a_subln[None, :].astype(F32),
                    lam_init)
    b_out = _b_attn(l0_b_sinks.astype(F32), bq, bk, bv)
    w_out = l0_w_out.astype(BF16)
    x2 = _l0_ffn(a_out, b_out, x2d, w_out[:A_W], w_out[A_W:], l0_norm_ffn[None, :],
                 l0_ffn_w_gu.astype(BF16), l0_ffn_w_down.astype(BF16), tm, D_FF // 2)

    wq = l1_c_w_in[:, :C_Q_RANK].astype(BF16)
    wkv = l1_c_w_in[:, C_Q_RANK:C_Q_RANK + C_KV_RANK].astype(BF16)
    wkr = jnp.pad(l1_c_w_in[:, C_Q_RANK + C_KV_RANK:], ((0, 0), (C_NOPE, LANES - C_QK)))
    wkrs = _swap_rope_cols(wkr).astype(BF16)
    wkr = wkr.astype(BF16)
    wuq = _pad_heads(l1_c_w_uq, C_HEADS, C_QK)
    wuqs = _swap_rope_cols(wuq).astype(BF16)
    wuq = wuq.astype(BF16)
    ukv = l1_c_w_ukv.reshape(C_KV_RANK, C_HEADS, C_NOPE + C_V)
    wuk = _pad_heads(ukv[:, :, :C_NOPE].reshape(C_KV_RANK, -1), C_HEADS, C_NOPE).astype(BF16)
    wuv = jnp.pad(jnp.transpose(ukv[:, :, C_NOPE:], (1, 2, 0)), ((0, 0), (0, V_PAD), (0, 0)))
    wuv = wuv.reshape(C_HEADS * V_ROWS, C_KV_RANK).astype(BF16)
    vone = jnp.zeros((C_HEADS, V_ROWS, 1), F32).at[:, C_V, 0].set(1.0).reshape(-1, 1)
    pad_gain = lambda gvec: jnp.pad(gvec.astype(F32), (0, LANES - C_QK))[None, :]
    qg, kg = pad_gain(l1_c_q_norm), pad_gain(l1_c_k_norm)
    qgs, kgs = _swap_rope_cols(qg), _swap_rope_cols(kg)
    ones128 = jnp.ones((LANES, LANES), BF16)
    cos32, sin32 = _rope_tables_32(S)
    cq, ck, cvt = _c_in(x2, l1_norm_attn[None, :], wq, wkv, wkr, wkrs,
                       l1_c_q_lora_norm[None, :], l1_c_kv_lora_norm[None, :],
                       wuq, wuqs, wuk, wuv, qg, qgs, kg, kgs, vone, ones128, cos32, sin32, tm)
    c_o = _c_attn(cq, ck, cvt)

    r_pad = jnp.pad(l1_router.astype(F32), ((0, 0), (0, LANES - N_EXPERTS)))
    r_hi = r_pad.astype(BF16)
    r_lo = (r_pad - r_hi.astype(F32)).astype(BF16)
    tr = min(256, S)
    ridx = jnp.arange(tr)
    tri = (ridx[None, :] < ridx[:, None]).astype(BF16)
    x3, h2, meta, cnt = _c_out_router(c_o, x2, l1_c_w_out.astype(BF16), l1_norm_ffn[None, :],
                                      r_hi, r_lo, tri, tr)

    counts = cnt[0, :N_EXPERTS].astype(jnp.int32)
    padded = ((counts + MOE_TM - 1) // MOE_TM) * MOE_TM
    ends = jnp.cumsum(padded)
    offs = ends - padded
    idx = meta[:, 0:2].astype(jnp.int32)
    pos = meta[:, 4:6].astype(jnp.int32)
    slots = offs[idx] + pos
    n_slots = 2 * S + N_EXPERTS * MOE_TM
    nt = n_slots // MOE_TM
    tile_start = jnp.arange(nt, dtype=jnp.int32) * MOE_TM
    n_valid = jnp.broadcast_to(ends[-1] // MOE_TM, (N_EXPERTS,))
    tile_valid = (tile_start < ends[-1]).astype(jnp.int32)
    tile_expert = jnp.minimum(
        jnp.sum((tile_start[:, None] >= ends[None, :]).astype(jnp.int32), axis=1),
        N_EXPERTS - 1).astype(jnp.int32)
    pad_info = jnp.stack([offs + counts, padded - counts, n_valid]).astype(jnp.int32)

    ts = min(256, S)
    slots3 = slots.reshape(S // ts, 1, 2 * ts)
    xs = _dispatch(pad_info, slots3, h2, n_slots, ts)
    y = _moe(tile_expert, tile_valid, xs,
             l1_exp_w_gu.astype(BF16), l1_exp_w_down.astype(BF16))
    out = _combine(slots3, x3, meta, y, ts)
    return out.reshape(B, S, D_MODEL)
```

```python
import functools
import math

import jax
import jax.numpy as jnp
from jax import lax
from jax.experimental import pallas as pl
from jax.experimental.pallas import tpu as pltpu

F32 = jnp.float32
BF16 = jnp.bfloat16

D_MODEL = 1024
CHUNK = 64
ROPE_THETA = 10000.0
EPS = 1e-6
LANES = 128
ROW_TILES = D_MODEL // LANES

A_HEADS = 4
A_HEAD_DIM = 64
B_HEADS = 8
B_KV_HEADS = 2
B_HEAD_DIM = 64
A_W = A_HEADS * 2 * A_HEAD_DIM
B_QW = B_HEADS * B_HEAD_DIM
B_KW = B_KV_HEADS * B_HEAD_DIM

C_HEADS = 16
C_Q_RANK = 256
C_KV_RANK = 128
C_NOPE = 64
C_ROPE = 32
C_V = 64
C_QK = C_NOPE + C_ROPE

D_FF = 2816
N_EXPERTS = 8
D_FF_EXPERT = 3584

NEG_BIG = -1e30
LOG2E = 1.0 / math.log(2.0)
VMEM_LIMIT = 56 * 1024 * 1024


def _cparams(sem):
    return pltpu.CompilerParams(dimension_semantics=sem, vmem_limit_bytes=VMEM_LIMIT)


def _dot(a, b):
    return jnp.dot(a, b, preferred_element_type=F32)


def _dot_nt(a, b):
    return lax.dot_general(a, b, (((1,), (1,)), ((), ())), preferred_element_type=F32)


def _rms(x, g):
    ms = jnp.mean(x * x, axis=-1, keepdims=True)
    return x * lax.rsqrt(ms + EPS) * g


def _l0_in_kernel(x_ref, g_ref, w_ref, wvt_ref, vone_ref, gains_ref, ones_ref, cos_ref, sin_ref,
                  aq_o, ak_o, avt_o, bq_o, bk_o, bv_o):
    h = _rms(x_ref[...], g_ref[...]).astype(BF16)
    z = _dot(h, w_ref[...])
    avt_o[...] = (_dot_nt(wvt_ref[...], h) + vone_ref[...]).astype(BF16)
    cos = cos_ref[...]
    sin = sin_ref[...]
    ones_blk = ones_ref[...]
    lane = lax.broadcasted_iota(jnp.int32, (1, LANES), 1)
    first_half = (lane % 64) < 32

    def norm_rope(zs, gain, scale):
        ss = _dot((zs * zs).astype(BF16), ones_blk)
        r = lax.rsqrt(ss * (1.0 / 64.0) + EPS) * scale
        y = zs * gain
        sw = jnp.where(first_half, pltpu.roll(y, 96, 1), pltpu.roll(y, 32, 1))
        return ((y * cos + sw * sin) * r).astype(BF16)

    col = 0
    gcol = 0
    scale = A_HEAD_DIM ** -0.5
    scales = {"aq": scale * LOG2E, "ak": 1.0, "bq": scale, "bk": 1.0}
    outs = {"aq": aq_o, "ak": ak_o, "bq": bq_o, "bk": bk_o}
    for name, nslab in (("aq", 4), ("ak", 4), ("bq", 4), ("bk", 2), ("bv", 2)):
        if name == "bv":
            bv_o[...] = z[:, col:col + 2 * LANES].astype(BF16)
            col += 2 * LANES
            continue
        sc = scales[name]
        gain = gains_ref[gcol:gcol + 1, :]
        gcol += 1
        for s in range(nslab):
            zs = z[:, col:col + LANES]
            outs[name][:, s * LANES:(s + 1) * LANES] = norm_rope(zs, gain, sc)
            col += LANES


def _l0_in(x2d, g, w_in_p, w_avt, vone, gains, ones_blk, cos_t, sin_t, tm):
    S = x2d.shape[0]
    nw = w_in_p.shape[1]
    row = lambda i: (i, 0)
    const = lambda i: (0, 0)
    out_shapes = (
        jax.ShapeDtypeStruct((S, A_W), BF16), jax.ShapeDtypeStruct((S, A_W), BF16),
        jax.ShapeDtypeStruct((A_HEADS * A_V_ROWS, S), BF16), jax.ShapeDtypeStruct((S, B_QW), BF16),
        jax.ShapeDtypeStruct((S, 2 * B_KW), BF16), jax.ShapeDtypeStruct((S, 2 * B_KW), BF16))
    return pl.pallas_call(
        _l0_in_kernel,
        grid=(S // tm,),
        in_specs=[
            pl.BlockSpec((tm, D_MODEL), row),
            pl.BlockSpec((1, D_MODEL), const),
            pl.BlockSpec((D_MODEL, nw), const),
            pl.BlockSpec((A_HEADS * A_V_ROWS, D_MODEL), const),
            pl.BlockSpec((A_HEADS * A_V_ROWS, 1), const),
            pl.BlockSpec((8, LANES), const),
            pl.BlockSpec((LANES, LANES), const),
            pl.BlockSpec((tm, LANES), row),
            pl.BlockSpec((tm, LANES), row),
        ],
        out_specs=[
            pl.BlockSpec((tm, A_W), row), pl.BlockSpec((tm, A_W), row),
            pl.BlockSpec((A_HEADS * A_V_ROWS, tm), lambda i: (0, i)), pl.BlockSpec((tm, B_QW), row),
            pl.BlockSpec((tm, 2 * B_KW), row), pl.BlockSpec((tm, 2 * B_KW), row)],
        out_shape=out_shapes,
        compiler_params=_cparams(("arbitrary",)),
        name="l0_in_proj",
    )(x2d, g, w_in_p, w_avt, vone, gains, ones_blk, cos_t, sin_t)


ATT_TQ = 512
ATT_TK = 256
V_PAD = 16


def _softmax_pv(s_ref, s_max, vt, m, acc):
    m_new = jnp.maximum(m, s_max)
    alpha = jnp.exp2(m - m_new)
    p = jnp.exp2(s_ref[...] - m_new).astype(BF16)
    return m_new, alpha * acc + _dot(vt, p)


def _diag_masks():
    key_chunk = lax.broadcasted_iota(jnp.int32, (ATT_TK, ATT_TQ), 0) // CHUNK
    query_chunk = lax.broadcasted_iota(jnp.int32, (ATT_TK, ATT_TQ), 1) // CHUNK
    return [key_chunk + b * (ATT_TK // CHUNK) <= query_chunk for b in range(ATT_TQ // ATT_TK)]


def _attn_pipeline(i, k_ref, vt_ref, streams, v_rows):
    n = len(streams)

    for st in streams:
        st[5][...] = st[0].T

    def scores(st, blk, mask, dst):
        lanes, qt_ref = streams[st][1], streams[st][5]
        off = pl.multiple_of(blk * ATT_TK, ATT_TK)
        s = _dot(k_ref[pl.ds(off, ATT_TK), lanes], qt_ref[...])
        if mask is not None:
            s = jnp.where(mask, s, NEG_BIG)
        dst[...] = s
        return jnp.max(s, axis=0, keepdims=True)

    def consume(st, blk, src, s_max, state):
        off = pl.multiple_of(blk * ATT_TK, ATT_TK)
        return _softmax_pv(src, s_max, vt_ref[streams[st][2], pl.ds(off, ATT_TK)], *state)

    buf_a = [st[3] for st in streams]
    buf_b = [st[4] for st in streams]
    mask0, mask1 = _diag_masks()
    d0 = 2 * i
    d1 = d0 + 1
    state = [(jnp.full((1, ATT_TQ), NEG_BIG, F32), jnp.zeros((v_rows, ATT_TQ), F32))] * n
    max_a = [scores(st, d0, mask0, buf_a[st]) for st in range(n)]
    max_b = [None] * n
    for st in range(n):
        max_b[st] = scores(st, d1, mask1, buf_b[st])
        state[st] = consume(st, d0, buf_a[st], max_a[st], state[st])
    for st in range(n):
        max_a[st] = scores(st, 0, None, buf_a[st])
        state[st] = consume(st, d1, buf_b[st], max_b[st], state[st])

    def body(p, carry):
        carry, max_a = list(carry[0]), list(carry[1])
        max_b = [None] * n
        u0 = 2 * p
        u1 = u0 + 1
        nxt = jnp.minimum(u0 + 2, d0 - 1)
        for st in range(n):
            max_b[st] = scores(st, u1, None, buf_b[st])
            carry[st] = consume(st, u0, buf_a[st], max_a[st], carry[st])
        for st in range(n):
            max_a[st] = scores(st, nxt, None, buf_a[st])
            carry[st] = consume(st, u1, buf_b[st], max_b[st], carry[st])
        return tuple(carry), tuple(max_a)

    final, _ = lax.fori_loop(0, i, body, (tuple(state), tuple(max_a)))
    return [acc for _, acc in final]


_SCORE_BUF = pltpu.VMEM((ATT_TK, ATT_TQ), F32)
_QT_BUF = pltpu.VMEM((LANES, ATT_TQ), BF16)
_ATTN_SCRATCH = [_SCORE_BUF] * 4 + [_QT_BUF] * 2


A_V_ROWS = 2 * A_HEAD_DIM + V_PAD


A_HPS = 2


def _a_attn_kernel(lam_ref, q_ref, k_ref, vt_ref, subln_ref, o_ref, *scratch, lam_init):
    i = pl.program_id(1)
    lane = lax.broadcasted_iota(jnp.int32, (1, LANES), 1)
    streams = []
    for h in range(A_HPS):
        q = q_ref[:, h * LANES:(h + 1) * LANES]
        zero = jnp.zeros_like(q)
        lanes = slice(h * LANES, (h + 1) * LANES)
        rows = slice(h * A_V_ROWS, (h + 1) * A_V_ROWS)
        for mp, sel in enumerate((lane < 64, lane >= 64)):
            st = 2 * h + mp
            streams.append((jnp.where(sel, q, zero), lanes, rows, scratch[2 * st],
                            scratch[2 * st + 1], scratch[4 * A_HPS + st]))
    accs = _attn_pipeline(i, k_ref, vt_ref, tuple(streams), A_V_ROWS)
    dv = 2 * A_HEAD_DIM
    lf = lam_ref[...]
    lam = (jnp.exp(jnp.sum(lf[0:1] * lf[1:2], axis=1, keepdims=True))
           - jnp.exp(jnp.sum(lf[2:3] * lf[3:4], axis=1, keepdims=True)) + lam_init)
    for h in range(A_HPS):
        acc1, acc2 = accs[2 * h], accs[2 * h + 1]
        out = (acc1[:dv] / acc1[dv:dv + 1] - lam * (acc2[:dv] / acc2[dv:dv + 1])).T
        out = _rms(out, subln_ref[...]) * (1.0 - lam_init)
        o_ref[:, h * LANES:(h + 1) * LANES] = out.astype(BF16)


def _a_attn(lam_p, aq, ak, avt, subln, lam_init):
    S = aq.shape[0]
    once = pl.Buffered(1)
    return pl.pallas_call(
        functools.partial(_a_attn_kernel, lam_init=lam_init),
        grid=(A_HEADS // A_HPS, S // ATT_TQ),
        in_specs=[
            pl.BlockSpec((4, A_HEAD_DIM), lambda h, i: (0, 0)),
            pl.BlockSpec((ATT_TQ, A_HPS * LANES), lambda h, i: (i, h)),
            pl.BlockSpec((S, A_HPS * LANES), lambda h, i: (0, h), pipeline_mode=once),
            pl.BlockSpec((A_HPS * A_V_ROWS, S), lambda h, i: (h, 0), pipeline_mode=once),
            pl.BlockSpec((1, LANES), lambda h, i: (0, 0)),
        ],
        out_specs=pl.BlockSpec((ATT_TQ, A_HPS * LANES), lambda h, i: (i, h)),
        out_shape=jax.ShapeDtypeStruct((S, A_W), BF16),
        scratch_shapes=[_SCORE_BUF] * (4 * A_HPS) + [_QT_BUF] * (2 * A_HPS),
        compiler_params=_cparams(("arbitrary", "arbitrary")),
        name="a_diff_attn",
    )(lam_p, aq, ak, avt, subln)


B_BLK = 128


def _b_attn_kernel(sink_ref, q_ref, kp_ref, kc_ref, vp_ref, vc_ref, o_ref):
    i = pl.program_id(0)
    lane = lax.broadcasted_iota(jnp.int32, (1, LANES), 1)
    lo = lane < 64
    r = lax.broadcasted_iota(jnp.int32, (B_BLK, 2 * B_BLK), 0) // CHUNK
    c_idx = lax.broadcasted_iota(jnp.int32, (B_BLK, 2 * B_BLK), 1)
    c = c_idx // CHUNK
    mask = (c >= r) & (c <= r + 2) & ((c_idx >= B_BLK) | (i > 0))
    for slab in range(B_HEADS // 2):
        g = slab // 2
        k = jnp.concatenate([kp_ref[:, g * LANES:(g + 1) * LANES],
                             kc_ref[:, g * LANES:(g + 1) * LANES]], axis=0)
        v = jnp.concatenate([vp_ref[:, g * LANES:(g + 1) * LANES],
                             vc_ref[:, g * LANES:(g + 1) * LANES]], axis=0)
        qs = q_ref[:, slab * LANES:(slab + 1) * LANES]
        zq = jnp.zeros_like(qs)
        zv = jnp.zeros_like(v)
        out = jnp.zeros((B_BLK, LANES), F32)
        for half in range(2):
            sel = lo if half == 0 else jnp.logical_not(lo)
            sink = sink_ref[2 * slab + half]
            s = _dot_nt(jnp.where(sel, qs, zq), k)
            s = jnp.where(mask, s, NEG_BIG)
            m = jnp.maximum(jnp.max(s, axis=1, keepdims=True), sink)
            e = jnp.exp(s - m)
            denom = jnp.sum(e, axis=1, keepdims=True) + jnp.exp(sink - m)
            p = (e / denom).astype(BF16)
            out = out + _dot(p, jnp.where(sel, v, zv))
        o_ref[:, slab * LANES:(slab + 1) * LANES] = out.astype(BF16)


def _b_attn(sinks, bq, bk, bv):
    S = bq.shape[0]
    prev = lambda i: (jnp.maximum(i - 1, 0), 0)
    cur = lambda i: (i, 0)
    return pl.pallas_call(
        _b_attn_kernel,
        grid=(S // B_BLK,),
        in_specs=[
            pl.BlockSpec(memory_space=pltpu.SMEM),
            pl.BlockSpec((B_BLK, B_QW), cur),
            pl.BlockSpec((B_BLK, 2 * B_KW), prev),
            pl.BlockSpec((B_BLK, 2 * B_KW), cur),
            pl.BlockSpec((B_BLK, 2 * B_KW), prev),
            pl.BlockSpec((B_BLK, 2 * B_KW), cur),
        ],
        out_specs=pl.BlockSpec((B_BLK, B_QW), cur),
        out_shape=jax.ShapeDtypeStruct((S, B_QW), BF16),
        compiler_params=_cparams(("arbitrary",)),
        name="b_swa_attn",
    )(sinks, bq, bk, bk, bv, bv)


def _l0_ffn_kernel(a_ref, b_ref, x_ref, woa_ref, wob_ref, g_ref, wg_ref, wu_ref, wd_ref,
                   o_ref, h_sc, acc_sc):
    j = pl.program_id(1)

    @pl.when(j == 0)
    def _():
        x1 = x_ref[...] + _dot(a_ref[...], woa_ref[...]) + _dot(b_ref[...], wob_ref[...])
        acc_sc[...] = x1
        h_sc[...] = _rms(x1, g_ref[...]).astype(BF16)

    h = h_sc[...]
    gate = _dot(h, wg_ref[...])
    up = _dot(h, wu_ref[...])
    act = (gate * jax.nn.sigmoid(gate) * up).astype(BF16)
    acc_sc[...] += _dot(act, wd_ref[...])

    @pl.when(j == pl.num_programs(1) - 1)
    def _():
        o_ref[...] = acc_sc[...]


def _l0_ffn(a_out, b_out, x2d, wo_a, wo_b, g, w_gu, w_down, tm, tf):
    S = x2d.shape[0]
    nf = D_FF // tf
    row = lambda i, j: (i, 0)
    const = lambda i, j: (0, 0)
    return pl.pallas_call(
        _l0_ffn_kernel,
        grid=(S // tm, nf),
        in_specs=[
            pl.BlockSpec((tm, A_W), row),
            pl.BlockSpec((tm, B_QW), row),
            pl.BlockSpec((tm, D_MODEL), row),
            pl.BlockSpec((A_W, D_MODEL), const),
            pl.BlockSpec((B_QW, D_MODEL), const),
            pl.BlockSpec((1, D_MODEL), const),
            pl.BlockSpec((D_MODEL, tf), lambda i, j: (0, j)),
            pl.BlockSpec((D_MODEL, tf), lambda i, j: (0, nf + j)),
            pl.BlockSpec((tf, D_MODEL), lambda i, j: (j, 0)),
        ],
        out_specs=pl.BlockSpec((tm, D_MODEL), row),
        out_shape=jax.ShapeDtypeStruct((S, D_MODEL), F32),
        scratch_shapes=[pltpu.VMEM((tm, D_MODEL), BF16), pltpu.VMEM((tm, D_MODEL), F32)],
        compiler_params=_cparams(("arbitrary", "arbitrary")),
        name="l0_out_ffn",
    )(a_out, b_out, x2d, wo_a, wo_b, g, w_gu, w_gu, w_down)


def _c_in_kernel(x_ref, g_ref, wq_ref, wkv_ref, wkr_ref, wkrs_ref, gql_ref, gkvl_ref,
                 wuq_ref, wuqs_ref, wuk_ref, wuv_ref, qg_ref, qgs_ref, kg_ref, kgs_ref,
                 vone_ref, ones_ref, cos_ref, sin_ref, q_o, k_o, vt_o):
    h = _rms(x_ref[...], g_ref[...]).astype(BF16)
    cq = _rms(_dot(h, wq_ref[...]), gql_ref[...]).astype(BF16)
    ckv = _rms(_dot(h, wkv_ref[...]), gkvl_ref[...]).astype(BF16)
    kr = _dot(h, wkr_ref[...])
    krs = _dot(h, wkrs_ref[...])
    q = _dot(cq, wuq_ref[...])
    qs = _dot(cq, wuqs_ref[...])
    kn = _dot(ckv, wuk_ref[...])
    vt_o[...] = (_dot_nt(wuv_ref[...], ckv) + vone_ref[...]).astype(BF16)

    cos = cos_ref[...]
    sin = sin_ref[...]
    ones_blk = ones_ref[...]
    qg, qgs, kg, kgs = qg_ref[...], qgs_ref[...], kg_ref[...], kgs_ref[...]
    ss_kr = _dot((kr * kr).astype(BF16), ones_blk)
    kr_roped = kr * kg * cos + krs * kgs * sin
    scale = C_QK ** -0.5 * LOG2E
    inv = 1.0 / C_QK
    for hd in range(C_HEADS):
        sl = slice(hd * LANES, (hd + 1) * LANES)
        qh = q[:, sl]
        r = lax.rsqrt(_dot((qh * qh).astype(BF16), ones_blk) * inv + EPS) * scale
        q_o[:, sl] = ((qh * qg * cos + qs[:, sl] * qgs * sin) * r).astype(BF16)
        kh = kn[:, sl]
        rk = lax.rsqrt((_dot((kh * kh).astype(BF16), ones_blk) + ss_kr) * inv + EPS)
        k_o[:, sl] = ((kh * kg + kr_roped) * rk).astype(BF16)


def _c_in(x2d, g, wq, wkv, wkr, wkrs, gql, gkvl, wuq, wuqs, wuk, wuv, qg, qgs, kg, kgs,
          vone, ones_blk, cos_t, sin_t, tm):
    S = x2d.shape[0]
    row = lambda i: (i, 0)
    const = lambda i: (0, 0)
    full = lambda a: pl.BlockSpec(a.shape, const)
    W = C_HEADS * LANES
    return pl.pallas_call(
        _c_in_kernel,
        grid=(S // tm,),
        in_specs=[pl.BlockSpec((tm, D_MODEL), row), full(g), full(wq), full(wkv), full(wkr),
                  full(wkrs), full(gql), full(gkvl), full(wuq), full(wuqs), full(wuk),
                  full(wuv), full(qg), full(qgs), full(kg), full(kgs), full(vone),
                  full(ones_blk),
                  pl.BlockSpec((tm, LANES), row), pl.BlockSpec((tm, LANES), row)],
        out_specs=[pl.BlockSpec((tm, W), row), pl.BlockSpec((tm, W), row),
                   pl.BlockSpec((C_HEADS * V_ROWS, tm), lambda i: (0, i))],
        out_shape=(jax.ShapeDtypeStruct((S, W), BF16), jax.ShapeDtypeStruct((S, W), BF16),
                   jax.ShapeDtypeStruct((C_HEADS * V_ROWS, S), BF16)),
        compiler_params=_cparams(("arbitrary",)),
        name="c_in_proj",
    )(x2d, g, wq, wkv, wkr, wkrs, gql, gkvl, wuq, wuqs, wuk, wuv, qg, qgs, kg, kgs,
      vone, ones_blk, cos_t, sin_t)


V_ROWS = C_V + V_PAD
C_HPS = 4


def _c_attn_kernel(q_ref, k_ref, vt_ref, o_ref, *scratch):
    i = pl.program_id(1)
    streams = tuple(
        (q_ref[:, h * LANES:(h + 1) * LANES], slice(h * LANES, (h + 1) * LANES),
         slice(h * V_ROWS, (h + 1) * V_ROWS), scratch[2 * h], scratch[2 * h + 1],
         scratch[2 * C_HPS + h])
        for h in range(C_HPS))
    accs = _attn_pipeline(i, k_ref, vt_ref, streams, V_ROWS)
    for pair in range(C_HPS // 2):
        a, b = accs[2 * pair], accs[2 * pair + 1]
        out = jnp.concatenate([a[:C_V] / a[C_V:C_V + 1], b[:C_V] / b[C_V:C_V + 1]], axis=0)
        o_ref[:, pair * LANES:(pair + 1) * LANES] = out.T.astype(BF16)


def _c_attn(q, k, vt):
    S = q.shape[0]
    once = pl.Buffered(1)
    return pl.pallas_call(
        _c_attn_kernel,
        grid=(C_HEADS // C_HPS, S // ATT_TQ),
        in_specs=[
            pl.BlockSpec((ATT_TQ, C_HPS * LANES), lambda h, i: (i, h)),
            pl.BlockSpec((S, C_HPS * LANES), lambda h, i: (0, h), pipeline_mode=once),
            pl.BlockSpec((C_HPS * V_ROWS, S), lambda h, i: (h, 0), pipeline_mode=once),
        ],
        out_specs=pl.BlockSpec((ATT_TQ, C_HPS * C_V), lambda h, i: (i, h)),
        out_shape=jax.ShapeDtypeStruct((S, C_HEADS * C_V), BF16),
        scratch_shapes=[_SCORE_BUF] * (2 * C_HPS) + [_QT_BUF] * C_HPS,
        compiler_params=_cparams(("arbitrary", "arbitrary")),
        name="c_mla_attn",
    )(q, k, vt)


def _c_out_router_kernel(o_ref, x_ref, wo_ref, g_ref, rhi_ref, rlo_ref, tri_ref,
                         x3_o, h_o, meta_o, cnt_o, run_sc):
    i = pl.program_id(0)

    @pl.when(i == 0)
    def _():
        run_sc[...] = jnp.zeros_like(run_sc)

    x3 = x_ref[...] + _dot(o_ref[...], wo_ref[...])
    x3_o[...] = x3
    h = _rms(x3, g_ref[...])
    for c in range(ROW_TILES):
        h_o[:, c, :] = h[:, c * LANES:(c + 1) * LANES]
    h_hi = h.astype(BF16)
    h_lo = (h - h_hi.astype(F32)).astype(BF16)
    logits = (_dot(h_hi, rhi_ref[...]) + _dot(h_hi, rlo_ref[...])) + _dot(h_lo, rhi_ref[...])
    tm = logits.shape[0]
    lane = lax.broadcasted_iota(jnp.int32, (tm, LANES), 1)
    lanef = lane.astype(F32)
    logits = jnp.where(lane < N_EXPERTS, logits, -jnp.inf)
    v1 = jnp.max(logits, axis=1, keepdims=True)
    i1 = jnp.min(jnp.where(logits == v1, lanef, float(LANES)), axis=1, keepdims=True)
    m1 = lanef == i1
    rest = jnp.where(m1, -jnp.inf, logits)
    v2 = jnp.max(rest, axis=1, keepdims=True)
    i2 = jnp.min(jnp.where(rest == v2, lanef, float(LANES)), axis=1, keepdims=True)
    m2 = lanef == i2
    e = jnp.exp(v2 - v1)
    g1 = 1.0 / (1.0 + e)
    g2 = e / (1.0 + e)
    chosen = jnp.where(m1 | m2, 1.0, 0.0)
    before = _dot(tri_ref[...], chosen.astype(BF16)) + run_sc[0:1, :]
    p1 = jnp.sum(jnp.where(m1, before, 0.0), axis=1, keepdims=True)
    p2 = jnp.sum(jnp.where(m2, before, 0.0), axis=1, keepdims=True)
    run_sc[...] = run_sc[...] + jnp.sum(chosen, axis=0, keepdims=True)
    meta = jnp.where(lane == 0, i1, 0.0)
    meta = jnp.where(lane == 1, i2, meta)
    meta = jnp.where(lane == 2, g1, meta)
    meta = jnp.where(lane == 3, g2, meta)
    meta = jnp.where(lane == 4, p1, meta)
    meta = jnp.where(lane == 5, p2, meta)
    meta_o[...] = meta
    cnt_o[...] = run_sc[...]


def _c_out_router(o, x2d, wo, g, rhi, rlo, tri, tm):
    S = x2d.shape[0]
    row = lambda i: (i, 0)
    const = lambda i: (0, 0)
    return pl.pallas_call(
        _c_out_router_kernel,
        grid=(S // tm,),
        in_specs=[pl.BlockSpec((tm, D_MODEL), row), pl.BlockSpec((tm, D_MODEL), row),
                  pl.BlockSpec((D_MODEL, D_MODEL), const), pl.BlockSpec((1, D_MODEL), const),
                  pl.BlockSpec((D_MODEL, LANES), const), pl.BlockSpec((D_MODEL, LANES), const),
                  pl.BlockSpec((tm, tm), const)],
        out_specs=[pl.BlockSpec((tm, D_MODEL), row),
                   pl.BlockSpec((tm, ROW_TILES, LANES), lambda i: (i, 0, 0)),
                   pl.BlockSpec((tm, LANES), row), pl.BlockSpec((8, LANES), const)],
        out_shape=(jax.ShapeDtypeStruct((S, D_MODEL), F32),
                   jax.ShapeDtypeStruct((S, ROW_TILES, LANES), F32),
                   jax.ShapeDtypeStruct((S, LANES), F32), jax.ShapeDtypeStruct((8, LANES), F32)),
        scratch_shapes=[pltpu.VMEM((8, LANES), F32)],
        compiler_params=_cparams(("arbitrary",)),
        name="c_out_router",
    )(o, x2d, wo, g, rhi, rlo, tri)


MOE_TM = 256
PAD_PIECES = (128, 64, 32, 16, 8, 4, 2, 1)


def _dispatch_kernel(pad_ref, slot_ref, h_ref, xs_ref, zero_sc, sem, zsem, *, ts):
    i = pl.program_id(0)

    @pl.when(i == 0)
    def _():
        zero_sc[...] = jnp.zeros_like(zero_sc)
        for e in range(N_EXPERTS):
            start = pad_ref[0, e]
            npad = pad_ref[1, e]
            for p in PAD_PIECES:
                hit = (npad & p) != 0

                @pl.when(hit)
                def _(start=start, p=p):
                    cp = pltpu.make_async_copy(zero_sc.at[pl.ds(0, p)],
                                               xs_ref.at[pl.ds(start, p)], zsem)
                    cp.start()
                    cp.wait()

                start = start + jnp.where(hit, p, 0)

        half = MOE_TM // 2

        def zero_tile(tile, c):
            for part in range(2):
                cp = pltpu.make_async_copy(
                    zero_sc, xs_ref.at[pl.ds(tile * MOE_TM + part * half, half)], zsem)
                cp.start()
                cp.wait()
            return c

        lax.fori_loop(pad_ref[2, 0], xs_ref.shape[0] // MOE_TM, zero_tile, 0)

    def row_copy(r, k):
        return pltpu.make_async_copy(h_ref.at[r], xs_ref.at[slot_ref[0, 0, 2 * r + k]], sem)

    def issue(r, c):
        row_copy(r, 0).start()
        row_copy(r, 1).start()
        return c

    lax.fori_loop(0, ts, issue, 0, unroll=8)
    for _ in range(2):
        pltpu.make_async_copy(h_ref, xs_ref.at[pl.ds(0, ts)], sem).wait()


def _dispatch(pad_info, slots3, h2, n_slots, ts):
    S = h2.shape[0]
    return pl.pallas_call(
        functools.partial(_dispatch_kernel, ts=ts),
        grid=(S // ts,),
        in_specs=[pl.BlockSpec(memory_space=pltpu.SMEM),
                  pl.BlockSpec((1, 1, 2 * ts), lambda i: (i, 0, 0), memory_space=pltpu.SMEM),
                  pl.BlockSpec((ts, ROW_TILES, LANES), lambda i: (i, 0, 0))],
        out_specs=pl.BlockSpec(memory_space=pl.ANY),
        out_shape=jax.ShapeDtypeStruct((n_slots, ROW_TILES, LANES), F32),
        scratch_shapes=[pltpu.VMEM((MOE_TM // 2, ROW_TILES, LANES), F32),
                        pltpu.SemaphoreType.DMA(()), pltpu.SemaphoreType.DMA(())],
        compiler_params=_cparams(("arbitrary",)),
        name="moe_dispatch",
    )(pad_info, slots3, h2)


MOE_FC = 512


def _moe_kernel(te_ref, tv_ref, xs_ref, wg_ref, wu_ref, wd_ref, y_ref, x_sc, act_sc):
    t = pl.program_id(0)

    @pl.when(tv_ref[t] == 0)
    def _():
        y_ref[...] = jnp.zeros_like(y_ref)

    @pl.when(tv_ref[t] != 0)
    def _():
        for c in range(ROW_TILES):
            x_sc[:, c * LANES:(c + 1) * LANES] = xs_ref[:, c, :].astype(BF16)
        x = x_sc[...]
        for c in range(D_FF_EXPERT // MOE_FC):
            sl = slice(c * MOE_FC, (c + 1) * MOE_FC)
            gate = _dot(x, wg_ref[:, sl])
            up = _dot(x, wu_ref[:, sl])
            act_sc[:, sl] = (gate * jax.nn.sigmoid(gate) * up).astype(BF16)
        acc = _dot(act_sc[...], wd_ref[...])
        for c in range(ROW_TILES):
            y_ref[:, c, :] = acc[:, c * LANES:(c + 1) * LANES]


def _moe(tile_expert, tile_valid, xs, w_gu, w_down):
    n_slots = xs.shape[0]
    nt = n_slots // MOE_TM
    grid_spec = pltpu.PrefetchScalarGridSpec(
        num_scalar_prefetch=2,
        grid=(nt,),
        in_specs=[
            pl.BlockSpec((MOE_TM, ROW_TILES, LANES), lambda t, te, tv: (t, 0, 0)),
            pl.BlockSpec((None, D_MODEL, D_FF_EXPERT), lambda t, te, tv: (te[t], 0, 0),
                         pipeline_mode=pl.Buffered(1)),
            pl.BlockSpec((None, D_MODEL, D_FF_EXPERT), lambda t, te, tv: (te[t], 0, 1),
                         pipeline_mode=pl.Buffered(1)),
            pl.BlockSpec((None, D_FF_EXPERT, D_MODEL), lambda t, te, tv: (te[t], 0, 0),
                         pipeline_mode=pl.Buffered(1)),
        ],
        out_specs=pl.BlockSpec((MOE_TM, ROW_TILES, LANES), lambda t, te, tv: (t, 0, 0)),
        scratch_shapes=[pltpu.VMEM((MOE_TM, D_MODEL), BF16),
                        pltpu.VMEM((MOE_TM, D_FF_EXPERT), BF16)],
    )
    return pl.pallas_call(
        _moe_kernel,
        grid_spec=grid_spec,
        out_shape=jax.ShapeDtypeStruct((n_slots, ROW_TILES, LANES), F32),
        compiler_params=_cparams(("arbitrary",)),
        name="moe_experts",
    )(tile_expert, tile_valid, xs, w_gu, w_gu, w_down)


def _combine_kernel(slot_ref, x3_ref, meta_ref, y_ref, o_ref, buf, sem, *, ts):
    def row_copy(r, k):
        return pltpu.make_async_copy(y_ref.at[slot_ref[0, 0, 2 * r + k]], buf.at[k, r], sem)

    def issue(r, c):
        row_copy(r, 0).start()
        row_copy(r, 1).start()
        return c

    lax.fori_loop(0, ts, issue, 0, unroll=8)
    for k in range(2):
        pltpu.make_async_copy(y_ref.at[pl.ds(0, ts)], buf.at[k], sem).wait()
    meta = meta_ref[...]
    g1 = meta[:, 2:3]
    g2 = meta[:, 3:4]
    for c in range(ROW_TILES):
        sl = slice(c * LANES, (c + 1) * LANES)
        o_ref[:, sl] = x3_ref[:, sl] + g1 * buf[0, :, c, :] + g2 * buf[1, :, c, :]


def _combine(slots3, x3, meta, y, ts):
    S = x3.shape[0]
    return pl.pallas_call(
        functools.partial(_combine_kernel, ts=ts),
        grid=(S // ts,),
        in_specs=[pl.BlockSpec((1, 1, 2 * ts), lambda i: (i, 0, 0), memory_space=pltpu.SMEM),
                  pl.BlockSpec((ts, D_MODEL), lambda i: (i, 0)),
                  pl.BlockSpec((ts, LANES), lambda i: (i, 0)),
                  pl.BlockSpec(memory_space=pl.ANY)],
        out_specs=pl.BlockSpec((ts, D_MODEL), lambda i: (i, 0)),
        out_shape=jax.ShapeDtypeStruct((S, D_MODEL), F32),
        scratch_shapes=[pltpu.VMEM((2, ts, ROW_TILES, LANES), F32), pltpu.SemaphoreType.DMA(())],
        compiler_params=_cparams(("arbitrary",)),
        name="moe_combine",
    )(slots3, x3, meta, y)


def _tile_lanes(v, reps):
    return jnp.tile(v, reps)[None, :].astype(F32)


def _rope_tables_64(S):
    pos = jnp.arange(S, dtype=F32)
    inv = ROPE_THETA ** (-jnp.arange(0, A_HEAD_DIM, 2, dtype=F32) / A_HEAD_DIM)
    ang = pos[:, None] * inv[None, :]
    c, s = jnp.cos(ang), jnp.sin(ang)
    return jnp.concatenate([c, c, c, c], axis=1), jnp.concatenate([-s, s, -s, s], axis=1)


def _rope_tables_32(S):
    pos = jnp.arange(S, dtype=F32)
    inv = ROPE_THETA ** (-jnp.arange(0, C_ROPE, 2, dtype=F32) / C_ROPE)
    ang = pos[:, None] * inv[None, :]
    c, s = jnp.cos(ang), jnp.sin(ang)
    one = jnp.ones((S, C_NOPE), F32)
    zero = jnp.zeros((S, C_NOPE), F32)
    pad = jnp.zeros((S, LANES - C_QK), F32)
    return (jnp.concatenate([one, c, c, pad], axis=1),
            jnp.concatenate([zero, -s, s, pad], axis=1))


def _pad_heads(w, n_heads, width):
    k = w.shape[0]
    w = w.reshape(k, n_heads, width)
    return jnp.pad(w, ((0, 0), (0, 0), (0, LANES - width))).reshape(k, n_heads * LANES)


def _swap_rope_cols(w_p):
    k = w_p.shape[0]
    w = w_p.reshape(k, -1, LANES)
    half = C_ROPE // 2
    sw = jnp.concatenate([jnp.zeros_like(w[:, :, :C_NOPE]),
                          w[:, :, C_NOPE + half:C_QK], w[:, :, C_NOPE:C_NOPE + half],
                          jnp.zeros_like(w[:, :, C_QK:])], axis=2)
    return sw.reshape(k, -1)


def kernel(x, l0_norm_attn, l0_w_in, l0_a_q_norm, l0_a_k_norm, l0_a_lambda, l0_a_subln, l0_b_q_norm, l0_b_k_norm, l0_b_sinks, l0_w_out, l0_norm_ffn, l0_ffn_w_gu, l0_ffn_w_down, l1_norm_attn, l1_c_w_in, l1_c_q_lora_norm, l1_c_kv_lora_norm, l1_c_w_uq, l1_c_w_ukv, l1_c_q_norm, l1_c_k_norm, l1_c_w_out, l1_norm_ffn, l1_router, l1_exp_w_gu, l1_exp_w_down):
    B, S, _ = x.shape
    assert B == 1
    x2d = x.reshape(S, D_MODEL)
    tm = min(512, S)

    o_bk = 3 * A_W + B_QW
    o_bv = o_bk + B_KW
    bk_w = l0_w_in[:, o_bk:o_bv].reshape(D_MODEL, B_KV_HEADS, B_HEAD_DIM)
    bv_w = l0_w_in[:, o_bv:].reshape(D_MODEL, B_KV_HEADS, B_HEAD_DIM)
    dup = lambda w: jnp.concatenate([w, w], axis=2).reshape(D_MODEL, 2 * B_KW)
    w_in_p = jnp.concatenate([l0_w_in[:, :2 * A_W], l0_w_in[:, 3 * A_W:o_bk], dup(bk_w),
                              dup(bv_w)], axis=1).astype(BF16)
    w_avt = l0_w_in[:, 2 * A_W:3 * A_W].T.reshape(A_HEADS, 2 * A_HEAD_DIM, D_MODEL)
    w_avt = jnp.pad(w_avt, ((0, 0), (0, V_PAD), (0, 0))).reshape(-1, D_MODEL).astype(BF16)
    a_vone = jnp.zeros((A_HEADS, A_V_ROWS, 1), F32).at[:, 2 * A_HEAD_DIM, 0].set(1.0).reshape(-1, 1)
    gains = jnp.concatenate([_tile_lanes(l0_a_q_norm, 2), _tile_lanes(l0_a_k_norm, 2),
                             _tile_lanes(l0_b_q_norm, 2), _tile_lanes(l0_b_k_norm, 2),
                             jnp.zeros((4, LANES), F32)], axis=0)
    lane = jnp.arange(LANES)
    ones64 = (lane[:, None] // 64 == lane[None, :] // 64).astype(BF16)
    cos64, sin64 = _rope_tables_64(S)
    aq, ak, avt, bq, bk, bv = _l0_in(x2d, l0_norm_attn[None, :], w_in_p, w_avt, a_vone, gains,
                                     ones64, cos64, sin64, tm)
    lam_init = 0.8 - 0.6 * math.exp(-0.3 * 0)
    a_out = _a_attn(l0_a_lambda.astype(F32), aq, ak, avt, l0_a_subln[None, :].astype(F32),
                    lam_init)
    b_out = _b_attn(l0_b_sinks.astype(F32), bq, bk, bv)
    w_out = l0_w_out.astype(BF16)
    x2 = _l0_ffn(a_out, b_out, x2d, w_out[:A_W], w_out[A_W:], l0_norm_ffn[None, :],
                 l0_ffn_w_gu.astype(BF16), l0_ffn_w_down.astype(BF16), tm, D_FF // 2)

    wq = l1_c_w_in[:, :C_Q_RANK].astype(BF16)
    wkv = l1_c_w_in[:, C_Q_RANK:C_Q_RANK + C_KV_RANK].astype(BF16)
    wkr = jnp.pad(l1_c_w_in[:, C_Q_RANK + C_KV_RANK:], ((0, 0), (C_NOPE, LANES - C_QK)))
    wkrs = _swap_rope_cols(wkr).astype(BF16)
    wkr = wkr.astype(BF16)
    wuq = _pad_heads(l1_c_w_uq, C_HEADS, C_QK)
    wuqs = _swap_rope_cols(wuq).astype(BF16)
    wuq = wuq.astype(BF16)
    ukv = l1_c_w_ukv.reshape(C_KV_RANK, C_HEADS, C_NOPE + C_V)
    wuk = _pad_heads(ukv[:, :, :C_NOPE].reshape(C_KV_RANK, -1), C_HEADS, C_NOPE).astype(BF16)
    wuv = jnp.pad(jnp.transpose(ukv[:, :, C_NOPE:], (1, 2, 0)), ((0, 0), (0, V_PAD), (0, 0)))
    wuv = wuv.reshape(C_HEADS * V_ROWS, C_KV_RANK).astype(BF16)
    vone = jnp.zeros((C_HEADS, V_ROWS, 1), F32).at[:, C_V, 0].set(1.0).reshape(-1, 1)
    pad_gain = lambda gvec: jnp.pad(gvec.astype(F32), (0, LANES - C_QK))[None, :]
    qg, kg = pad_gain(l1_c_q_norm), pad_gain(l1_c_k_norm)
    qgs, kgs = _swap_rope_cols(qg), _swap_rope_cols(kg)
    ones128 = jnp.ones((LANES, LANES), BF16)
    cos32, sin32 = _rope_tables_32(S)
    cq, ck, cvt = _c_in(x2, l1_norm_attn[None, :], wq, wkv, wkr, wkrs,
                       l1_c_q_lora_norm[None, :], l1_c_kv_lora_norm[None, :],
                       wuq, wuqs, wuk, wuv, qg, qgs, kg, kgs, vone, ones128, cos32, sin32, tm)
    c_o = _c_attn(cq, ck, cvt)

    r_pad = jnp.pad(l1_router.astype(F32), ((0, 0), (0, LANES - N_EXPERTS)))
    r_hi = r_pad.astype(BF16)
    r_lo = (r_pad - r_hi.astype(F32)).astype(BF16)
    tr = min(256, S)
    ridx = jnp.arange(tr)
    tri = (ridx[None, :] < ridx[:, None]).astype(BF16)
    x3, h2, meta, cnt = _c_out_router(c_o, x2, l1_c_w_out.astype(BF16), l1_norm_ffn[None, :],
                                      r_hi, r_lo, tri, tr)

    counts = cnt[0, :N_EXPERTS].astype(jnp.int32)
    padded = ((counts + MOE_TM - 1) // MOE_TM) * MOE_TM
    ends = jnp.cumsum(padded)
    offs = ends - padded
    idx = meta[:, 0:2].astype(jnp.int32)
    pos = meta[:, 4:6].astype(jnp.int32)
    slots = offs[idx] + pos
    n_slots = 2 * S + N_EXPERTS * MOE_TM
    nt = n_slots // MOE_TM
    tile_start = jnp.arange(nt, dtype=jnp.int32) * MOE_TM
    n_valid = jnp.broadcast_to(ends[-1] // MOE_TM, (N_EXPERTS,))
    tile_valid = (tile_start < ends[-1]).astype(jnp.int32)
    tile_expert = jnp.minimum(
        jnp.sum((tile_start[:, None] >= ends[None, :]).astype(jnp.int32), axis=1),
        N_EXPERTS - 1).astype(jnp.int32)
    pad_info = jnp.stack([offs + counts, padded - counts, n_valid]).astype(jnp.int32)

    ts = min(256, S)
    slots3 = slots.reshape(S // ts, 1, 2 * ts)
    xs = _dispatch(pad_info, slots3, h2, n_slots, ts)
    y = _moe(tile_expert, tile_valid, xs,
             l1_exp_w_gu.astype(BF16), l1_exp_w_down.astype(BF16))
    out = _combine(slots3, x3, meta, y, ts)
    return out.reshape(B, S, D_MODEL)
```

```python
import functools
import math

import jax
import jax.numpy as jnp
from jax import lax
from jax.experimental import pallas as pl
from jax.experimental.pallas import tpu as pltpu

F32 = jnp.float32
BF16 = jnp.bfloat16

D_MODEL = 1024
CHUNK = 64
ROPE_THETA = 10000.0
EPS = 1e-6
LANES = 128
ROW_TILES = D_MODEL // LANES

A_HEADS = 4
A_HEAD_DIM = 64
B_HEADS = 8
B_KV_HEADS = 2
B_HEAD_DIM = 64
A_W = A_HEADS * 2 * A_HEAD_DIM
B_QW = B_HEADS * B_HEAD_DIM
B_KW = B_KV_HEADS * B_HEAD_DIM

C_HEADS = 16
C_Q_RANK = 256
C_KV_RANK = 128
C_NOPE = 64
C_ROPE = 32
C_V = 64
C_QK = C_NOPE + C_ROPE

D_FF = 2816
N_EXPERTS = 8
D_FF_EXPERT = 3584

NEG_BIG = -1e30
LOG2E = 1.0 / math.log(2.0)
VMEM_LIMIT = 56 * 1024 * 1024


def _cparams(sem):
    return pltpu.CompilerParams(dimension_semantics=sem, vmem_limit_bytes=VMEM_LIMIT)


def _dot(a, b):
    return jnp.dot(a, b, preferred_element_type=F32)


def _dot_nt(a, b):
    return lax.dot_general(a, b, (((1,), (1,)), ((), ())), preferred_element_type=F32)


def _rms(x, g):
    ms = jnp.mean(x * x, axis=-1, keepdims=True)
    return x * lax.rsqrt(ms + EPS) * g


def _l0_in_kernel(x_ref, g_ref, w_ref, wvt_ref, vone_ref, gains_ref, ones_ref, cos_ref, sin_ref,
                  aq_o, ak_o, avt_o, bq_o, bk_o, bv_o):
    h = _rms(x_ref[...], g_ref[...]).astype(BF16)
    z = _dot(h, w_ref[...])
    avt_o[...] = (_dot_nt(wvt_ref[...], h) + vone_ref[...]).astype(BF16)
    cos = cos_ref[...]
    sin = sin_ref[...]
    ones_blk = ones_ref[...]
    lane = lax.broadcasted_iota(jnp.int32, (1, LANES), 1)
    first_half = (lane % 64) < 32

    def norm_rope(zs, gain, scale):
        ss = _dot((zs * zs).astype(BF16), ones_blk)
        r = lax.rsqrt(ss * (1.0 / 64.0) + EPS) * scale
        y = zs * gain
        sw = jnp.where(first_half, pltpu.roll(y, 96, 1), pltpu.roll(y, 32, 1))
        return ((y * cos + sw * sin) * r).astype(BF16)

    col = 0
    gcol = 0
    scale = A_HEAD_DIM ** -0.5
    scales = {"aq": scale * LOG2E, "ak": 1.0, "bq": scale, "bk": 1.0}
    outs = {"aq": aq_o, "ak": ak_o, "bq": bq_o, "bk": bk_o}
    for name, nslab in (("aq", 4), ("ak", 4), ("bq", 4), ("bk", 2), ("bv", 2)):
        if name == "bv":
            bv_o[...] = z[:, col:col + 2 * LANES].astype(BF16)
            col += 2 * LANES
            continue
        sc = scales[name]
        gain = gains_ref[gcol:gcol + 1, :]
        gcol += 1
        for s in range(nslab):
            zs = z[:, col:col + LANES]
            outs[name][:, s * LANES:(s + 1) * LANES] = norm_rope(zs, gain, sc)
            col += LANES


def _l0_in(x2d, g, w_in_p, w_avt, vone, gains, ones_blk, cos_t, sin_t, tm):
    S = x2d.shape[0]
    nw = w_in_p.shape[1]
    row = lambda i: (i, 0)
    const = lambda i: (0, 0)
    out_shapes = (
        jax.ShapeDtypeStruct((S, A_W), BF16), jax.ShapeDtypeStruct((S, A_W), BF16),
        jax.ShapeDtypeStruct((A_HEADS * A_V_ROWS, S), BF16), jax.ShapeDtypeStruct((S, B_QW), BF16),
        jax.ShapeDtypeStruct((S, 2 * B_KW), BF16), jax.ShapeDtypeStruct((S, 2 * B_KW), BF16))
    return pl.pallas_call(
        _l0_in_kernel,
        grid=(S // tm,),
        in_specs=[
            pl.BlockSpec((tm, D_MODEL), row),
            pl.BlockSpec((1, D_MODEL), const),
            pl.BlockSpec((D_MODEL, nw), const),
            pl.BlockSpec((A_HEADS * A_V_ROWS, D_MODEL), const),
            pl.BlockSpec((A_HEADS * A_V_ROWS, 1), const),
            pl.BlockSpec((8, LANES), const),
            pl.BlockSpec((LANES, LANES), const),
            pl.BlockSpec((tm, LANES), row),
            pl.BlockSpec((tm, LANES), row),
        ],
        out_specs=[
            pl.BlockSpec((tm, A_W), row), pl.BlockSpec((tm, A_W), row),
            pl.BlockSpec((A_HEADS * A_V_ROWS, tm), lambda i: (0, i)), pl.BlockSpec((tm, B_QW), row),
            pl.BlockSpec((tm, 2 * B_KW), row), pl.BlockSpec((tm, 2 * B_KW), row)],
        out_shape=out_shapes,
        compiler_params=_cparams(("arbitrary",)),
        name="l0_in_proj",
    )(x2d, g, w_in_p, w_avt, vone, gains, ones_blk, cos_t, sin_t)


ATT_TQ = 512
ATT_TK = 256
V_PAD = 16


def _softmax_pv(s_ref, s_max, vt, m, acc):
    m_new = jnp.maximum(m, s_max)
    alpha = jnp.exp2(m - m_new)
    p = jnp.exp2(s_ref[...] - m_new).astype(BF16)
    return m_new, alpha * acc + _dot(vt, p)


def _diag_masks():
    key_chunk = lax.broadcasted_iota(jnp.int32, (ATT_TK, ATT_TQ), 0) // CHUNK
    query_chunk = lax.broadcasted_iota(jnp.int32, (ATT_TK, ATT_TQ), 1) // CHUNK
    return [key_chunk + b * (ATT_TK // CHUNK) <= query_chunk for b in range(ATT_TQ // ATT_TK)]


def _attn_pipeline(i, k_ref, vt_ref, streams, v_rows):
    n = len(streams)

    for st in streams:
        st[5][...] = st[0].T

    def scores(st, blk, mask, dst):
        lanes, qt_ref = streams[st][1], streams[st][5]
        off = pl.multiple_of(blk * ATT_TK, ATT_TK)
        s = _dot(k_ref[pl.ds(off, ATT_TK), lanes], qt_ref[...])
        if mask is not None:
            s = jnp.where(mask, s, NEG_BIG)
        dst[...] = s
        return jnp.max(s, axis=0, keepdims=True)

    def consume(st, blk, src, s_max, state):
        off = pl.multiple_of(blk * ATT_TK, ATT_TK)
        return _softmax_pv(src, s_max, vt_ref[streams[st][2], pl.ds(off, ATT_TK)], *state)

    buf_a = [st[3] for st in streams]
    buf_b = [st[4] for st in streams]
    mask0, mask1 = _diag_masks()
    d0 = 2 * i
    d1 = d0 + 1
    state = [(jnp.full((1, ATT_TQ), NEG_BIG, F32), jnp.zeros((v_rows, ATT_TQ), F32))] * n
    max_a = [scores(st, d0, mask0, buf_a[st]) for st in range(n)]
    max_b = [None] * n
    for st in range(n):
        max_b[st] = scores(st, d1, mask1, buf_b[st])
        state[st] = consume(st, d0, buf_a[st], max_a[st], state[st])
    for st in range(n):
        max_a[st] = scores(st, 0, None, buf_a[st])
        state[st] = consume(st, d1, buf_b[st], max_b[st], state[st])

    def body(p, carry):
        carry, max_a = list(carry[0]), list(carry[1])
        max_b = [None] * n
        u0 = 2 * p
        u1 = u0 + 1
        nxt = jnp.minimum(u0 + 2, d0 - 1)
        for st in range(n):
            max_b[st] = scores(st, u1, None, buf_b[st])
            carry[st] = consume(st, u0, buf_a[st], max_a[st], carry[st])
        for st in range(n):
            max_a[st] = scores(st, nxt, None, buf_a[st])
            carry[st] = consume(st, u1, buf_b[st], max_b[st], carry[st])
        return tuple(carry), tuple(max_a)

    final, _ = lax.fori_loop(0, i, body, (tuple(state), tuple(max_a)))
    return [acc for _, acc in final]


_SCORE_BUF = pltpu.VMEM((ATT_TK, ATT_TQ), F32)
_QT_BUF = pltpu.VMEM((LANES, ATT_TQ), BF16)
_ATTN_SCRATCH = [_SCORE_BUF] * 4 + [_QT_BUF] * 2


A_V_ROWS = 2 * A_HEAD_DIM + V_PAD


A_HPS = 2


def _a_attn_kernel(lam_ref, q_ref, k_ref, vt_ref, subln_ref, o_ref, *scratch, lam_init):
    i = pl.program_id(1)
    lane = lax.broadcasted_iota(jnp.int32, (1, LANES), 1)
    streams = []
    for h in range(A_HPS):
        q = q_ref[:, h * LANES:(h + 1) * LANES]
        zero = jnp.zeros_like(q)
        lanes = slice(h * LANES, (h + 1) * LANES)
        rows = slice(h * A_V_ROWS, (h + 1) * A_V_ROWS)
        for mp, sel in enumerate((lane < 64, lane >= 64)):
            st = 2 * h + mp
            streams.append((jnp.where(sel, q, zero), lanes, rows, scratch[2 * st],
                            scratch[2 * st + 1], scratch[4 * A_HPS + st]))
    accs = _attn_pipeline(i, k_ref, vt_ref, tuple(streams), A_V_ROWS)
    dv = 2 * A_HEAD_DIM
    lf = lam_ref[...]
    lam = (jnp.exp(jnp.sum(lf[0:1] * lf[1:2], axis=1, keepdims=True))
           - jnp.exp(jnp.sum(lf[2:3] * lf[3:4], axis=1, keepdims=True)) + lam_init)
    for h in range(A_HPS):
        acc1, acc2 = accs[2 * h], accs[2 * h + 1]
        out = (acc1[:dv] / acc1[dv:dv + 1] - lam * (acc2[:dv] / acc2[dv:dv + 1])).T
        out = _rms(out, subln_ref[...]) * (1.0 - lam_init)
        o_ref[:, h * LANES:(h + 1) * LANES] = out.astype(BF16)


def _a_attn(lam_p, aq, ak, avt, subln, lam_init):
    S = aq.shape[0]
    once = pl.Buffered(1)
    return pl.pallas_call(
        functools.partial(_a_attn_kernel, lam_init=lam_init),
        grid=(A_HEADS // A_HPS, S // ATT_TQ),
        in_specs=[
            pl.BlockSpec((4, A_HEAD_DIM), lambda h, i: (0, 0)),
            pl.BlockSpec((ATT_TQ, A_HPS * LANES), lambda h, i: (i, h)),
            pl.BlockSpec((S, A_HPS * LANES), lambda h, i: (0, h), pipeline_mode=once),
            pl.BlockSpec((A_HPS * A_V_ROWS, S), lambda h, i: (h, 0), pipeline_mode=once),
            pl.BlockSpec((1, LANES), lambda h, i: (0, 0)),
        ],
        out_specs=pl.BlockSpec((ATT_TQ, A_HPS * LANES), lambda h, i: (i, h)),
        out_shape=jax.ShapeDtypeStruct((S, A_W), BF16),
        scratch_shapes=[_SCORE_BUF] * (4 * A_HPS) + [_QT_BUF] * (2 * A_HPS),
        compiler_params=_cparams(("arbitrary", "arbitrary")),
        name="a_diff_attn",
    )(lam_p, aq, ak, avt, subln)


B_BLK = 256
B_BACK = 128


def _b_attn_kernel(sink_ref, q_ref, kp_ref, kc_ref, vp_ref, vc_ref, o_ref):
    i = pl.program_id(0)
    lane = lax.broadcasted_iota(jnp.int32, (1, LANES), 1)
    lo = lane < 64
    r = lax.broadcasted_iota(jnp.int32, (B_BLK, B_BACK + B_BLK), 0) // CHUNK
    c_idx = lax.broadcasted_iota(jnp.int32, (B_BLK, B_BACK + B_BLK), 1)
    c = c_idx // CHUNK
    mask = (c >= r) & (c <= r + 2) & ((c_idx >= B_BACK) | (i > 0))
    for slab in range(B_HEADS // 2):
        g = slab // 2
        k = jnp.concatenate([kp_ref[:, g * LANES:(g + 1) * LANES],
                             kc_ref[:, g * LANES:(g + 1) * LANES]], axis=0)
        v = jnp.concatenate([vp_ref[:, g * LANES:(g + 1) * LANES],
                             vc_ref[:, g * LANES:(g + 1) * LANES]], axis=0)
        qs = q_ref[:, slab * LANES:(slab + 1) * LANES]
        zq = jnp.zeros_like(qs)
        zv = jnp.zeros_like(v)
        out = jnp.zeros((B_BLK, LANES), F32)
        for half in range(2):
            sel = lo if half == 0 else jnp.logical_not(lo)
            sink = sink_ref[2 * slab + half]
            s = _dot_nt(jnp.where(sel, qs, zq), k)
            s = jnp.where(mask, s, NEG_BIG)
            m = jnp.maximum(jnp.max(s, axis=1, keepdims=True), sink)
            e = jnp.exp(s - m)
            denom = jnp.sum(e, axis=1, keepdims=True) + jnp.exp(sink - m)
            p = (e / denom).astype(BF16)
            out = out + _dot(p, jnp.where(sel, v, zv))
        o_ref[:, slab * LANES:(slab + 1) * LANES] = out.astype(BF16)


def _b_attn(sinks, bq, bk, bv):
    S = bq.shape[0]
    prev = lambda i: (jnp.maximum(i * (B_BLK // B_BACK) - 1, 0), 0)
    cur = lambda i: (i, 0)
    return pl.pallas_call(
        _b_attn_kernel,
        grid=(S // B_BLK,),
        in_specs=[
            pl.BlockSpec(memory_space=pltpu.SMEM),
            pl.BlockSpec((B_BLK, B_QW), cur),
            pl.BlockSpec((B_BACK, 2 * B_KW), prev),
            pl.BlockSpec((B_BLK, 2 * B_KW), cur),
            pl.BlockSpec((B_BACK, 2 * B_KW), prev),
            pl.BlockSpec((B_BLK, 2 * B_KW), cur),
        ],
        out_specs=pl.BlockSpec((B_BLK, B_QW), cur),
        out_shape=jax.ShapeDtypeStruct((S, B_QW), BF16),
        compiler_params=_cparams(("arbitrary",)),
        name="b_swa_attn",
    )(sinks, bq, bk, bk, bv, bv)


def _l0_ffn_kernel(a_ref, b_ref, x_ref, woa_ref, wob_ref, g_ref, wg_ref, wu_ref, wd_ref,
                   o_ref, h_sc, acc_sc):
    j = pl.program_id(1)

    @pl.when(j == 0)
    def _():
        x1 = x_ref[...] + _dot(a_ref[...], woa_ref[...]) + _dot(b_ref[...], wob_ref[...])
        acc_sc[...] = x1
        h_sc[...] = _rms(x1, g_ref[...]).astype(BF16)

    h = h_sc[...]
    gate = _dot(h, wg_ref[...])
    up = _dot(h, wu_ref[...])
    act = (gate * jax.nn.sigmoid(gate) * up).astype(BF16)
    acc_sc[...] += _dot(act, wd_ref[...])

    @pl.when(j == pl.num_programs(1) - 1)
    def _():
        o_ref[...] = acc_sc[...]


def _l0_ffn(a_out, b_out, x2d, wo_a, wo_b, g, w_gu, w_down, tm, tf):
    S = x2d.shape[0]
    nf = D_FF // tf
    row = lambda i, j: (i, 0)
    const = lambda i, j: (0, 0)
    return pl.pallas_call(
        _l0_ffn_kernel,
        grid=(S // tm, nf),
        in_specs=[
            pl.BlockSpec((tm, A_W), row),
            pl.BlockSpec((tm, B_QW), row),
            pl.BlockSpec((tm, D_MODEL), row),
            pl.BlockSpec((A_W, D_MODEL), const),
            pl.BlockSpec((B_QW, D_MODEL), const),
            pl.BlockSpec((1, D_MODEL), const),
            pl.BlockSpec((D_MODEL, tf), lambda i, j: (0, j)),
            pl.BlockSpec((D_MODEL, tf), lambda i, j: (0, nf + j)),
            pl.BlockSpec((tf, D_MODEL), lambda i, j: (j, 0)),
        ],
        out_specs=pl.BlockSpec((tm, D_MODEL), row),
        out_shape=jax.ShapeDtypeStruct((S, D_MODEL), F32),
        scratch_shapes=[pltpu.VMEM((tm, D_MODEL), BF16), pltpu.VMEM((tm, D_MODEL), F32)],
        compiler_params=_cparams(("arbitrary", "arbitrary")),
        name="l0_out_ffn",
    )(a_out, b_out, x2d, wo_a, wo_b, g, w_gu, w_gu, w_down)


def _c_in_kernel(x_ref, g_ref, wq_ref, wkv_ref, wkr_ref, wkrs_ref, gql_ref, gkvl_ref,
                 wuq_ref, wuqs_ref, wuk_ref, wuv_ref, qg_ref, qgs_ref, kg_ref, kgs_ref,
                 vone_ref, ones_ref, cos_ref, sin_ref, q_o, k_o, vt_o):
    h = _rms(x_ref[...], g_ref[...]).astype(BF16)
    cq = _rms(_dot(h, wq_ref[...]), gql_ref[...]).astype(BF16)
    ckv = _rms(_dot(h, wkv_ref[...]), gkvl_ref[...]).astype(BF16)
    kr = _dot(h, wkr_ref[...])
    krs = _dot(h, wkrs_ref[...])
    q = _dot(cq, wuq_ref[...])
    qs = _dot(cq, wuqs_ref[...])
    kn = _dot(ckv, wuk_ref[...])
    vt_o[...] = (_dot_nt(wuv_ref[...], ckv) + vone_ref[...]).astype(BF16)

    cos = cos_ref[...]
    sin = sin_ref[...]
    ones_blk = ones_ref[...]
    qg, qgs, kg, kgs = qg_ref[...], qgs_ref[...], kg_ref[...], kgs_ref[...]
    ss_kr = _dot((kr * kr).astype(BF16), ones_blk)
    kr_roped = kr * kg * cos + krs * kgs * sin
    scale = C_QK ** -0.5 * LOG2E
    inv = 1.0 / C_QK
    for hd in range(C_HEADS):
        sl = slice(hd * LANES, (hd + 1) * LANES)
        qh = q[:, sl]
        r = lax.rsqrt(_dot((qh * qh).astype(BF16), ones_blk) * inv + EPS) * scale
        q_o[:, sl] = ((qh * qg * cos + qs[:, sl] * qgs * sin) * r).astype(BF16)
        kh = kn[:, sl]
        rk = lax.rsqrt((_dot((kh * kh).astype(BF16), ones_blk) + ss_kr) * inv + EPS)
        k_o[:, sl] = ((kh * kg + kr_roped) * rk).astype(BF16)


def _c_in(x2d, g, wq, wkv, wkr, wkrs, gql, gkvl, wuq, wuqs, wuk, wuv, qg, qgs, kg, kgs,
          vone, ones_blk, cos_t, sin_t, tm):
    S = x2d.shape[0]
    row = lambda i: (i, 0)
    const = lambda i: (0, 0)
    full = lambda a: pl.BlockSpec(a.shape, const)
    W = C_HEADS * LANES
    return pl.pallas_call(
        _c_in_kernel,
        grid=(S // tm,),
        in_specs=[pl.BlockSpec((tm, D_MODEL), row), full(g), full(wq), full(wkv), full(wkr),
                  full(wkrs), full(gql), full(gkvl), full(wuq), full(wuqs), full(wuk),
                  full(wuv), full(qg), full(qgs), full(kg), full(kgs), full(vone),
                  full(ones_blk),
                  pl.BlockSpec((tm, LANES), row), pl.BlockSpec((tm, LANES), row)],
        out_specs=[pl.BlockSpec((tm, W), row), pl.BlockSpec((tm, W), row),
                   pl.BlockSpec((C_HEADS * V_ROWS, tm), lambda i: (0, i))],
        out_shape=(jax.ShapeDtypeStruct((S, W), BF16), jax.ShapeDtypeStruct((S, W), BF16),
                   jax.ShapeDtypeStruct((C_HEADS * V_ROWS, S), BF16)),
        compiler_params=_cparams(("arbitrary",)),
        name="c_in_proj",
    )(x2d, g, wq, wkv, wkr, wkrs, gql, gkvl, wuq, wuqs, wuk, wuv, qg, qgs, kg, kgs,
      vone, ones_blk, cos_t, sin_t)


V_ROWS = C_V + V_PAD
C_HPS = 4


def _c_attn_kernel(q_ref, k_ref, vt_ref, o_ref, *scratch):
    i = pl.program_id(1)
    streams = tuple(
        (q_ref[:, h * LANES:(h + 1) * LANES], slice(h * LANES, (h + 1) * LANES),
         slice(h * V_ROWS, (h + 1) * V_ROWS), scratch[2 * h], scratch[2 * h + 1],
         scratch[2 * C_HPS + h])
        for h in range(C_HPS))
    accs = _attn_pipeline(i, k_ref, vt_ref, streams, V_ROWS)
    for pair in range(C_HPS // 2):
        a, b = accs[2 * pair], accs[2 * pair + 1]
        out = jnp.concatenate([a[:C_V] / a[C_V:C_V + 1], b[:C_V] / b[C_V:C_V + 1]], axis=0)
        o_ref[:, pair * LANES:(pair + 1) * LANES] = out.T.astype(BF16)


def _c_attn(q, k, vt):
    S = q.shape[0]
    once = pl.Buffered(1)
    return pl.pallas_call(
        _c_attn_kernel,
        grid=(C_HEADS // C_HPS, S // ATT_TQ),
        in_specs=[
            pl.BlockSpec((ATT_TQ, C_HPS * LANES), lambda h, i: (i, h)),
            pl.BlockSpec((S, C_HPS * LANES), lambda h, i: (0, h), pipeline_mode=once),
            pl.BlockSpec((C_HPS * V_ROWS, S), lambda h, i: (h, 0), pipeline_mode=once),
        ],
        out_specs=pl.BlockSpec((ATT_TQ, C_HPS * C_V), lambda h, i: (i, h)),
        out_shape=jax.ShapeDtypeStruct((S, C_HEADS * C_V), BF16),
        scratch_shapes=[_SCORE_BUF] * (2 * C_HPS) + [_QT_BUF] * C_HPS,
        compiler_params=_cparams(("arbitrary", "arbitrary")),
        name="c_mla_attn",
    )(q, k, vt)


def _c_out_router_kernel(o_ref, x_ref, wo_ref, g_ref, rhi_ref, rlo_ref, tri_ref,
                         x3_o, h_o, meta_o, cnt_o, run_sc):
    i = pl.program_id(0)

    @pl.when(i == 0)
    def _():
        run_sc[...] = jnp.zeros_like(run_sc)

    x3 = x_ref[...] + _dot(o_ref[...], wo_ref[...])
    x3_o[...] = x3
    h = _rms(x3, g_ref[...])
    for c in range(ROW_TILES):
        h_o[:, c, :] = h[:, c * LANES:(c + 1) * LANES]
    h_hi = h.astype(BF16)
    h_lo = (h - h_hi.astype(F32)).astype(BF16)
    logits = (_dot(h_hi, rhi_ref[...]) + _dot(h_hi, rlo_ref[...])) + _dot(h_lo, rhi_ref[...])
    tm = logits.shape[0]
    lane = lax.broadcasted_iota(jnp.int32, (tm, LANES), 1)
    lanef = lane.astype(F32)
    logits = jnp.where(lane < N_EXPERTS, logits, -jnp.inf)
    v1 = jnp.max(logits, axis=1, keepdims=True)
    i1 = jnp.min(jnp.where(logits == v1, lanef, float(LANES)), axis=1, keepdims=True)
    m1 = lanef == i1
    rest = jnp.where(m1, -jnp.inf, logits)
    v2 = jnp.max(rest, axis=1, keepdims=True)
    i2 = jnp.min(jnp.where(rest == v2, lanef, float(LANES)), axis=1, keepdims=True)
    m2 = lanef == i2
    e = jnp.exp(v2 - v1)
    g1 = 1.0 / (1.0 + e)
    g2 = e / (1.0 + e)
    chosen = jnp.where(m1 | m2, 1.0, 0.0)
    before = _dot(tri_ref[...], chosen.astype(BF16)) + run_sc[0:1, :]
    p1 = jnp.sum(jnp.where(m1, before, 0.0), axis=1, keepdims=True)
    p2 = jnp.sum(jnp.where(m2, before, 0.0), axis=1, keepdims=True)
    run_sc[...] = run_sc[...] + jnp.sum(chosen, axis=0, keepdims=True)
    meta = jnp.where(lane == 0, i1, 0.0)
    meta = jnp.where(lane == 1, i2, meta)
    meta = jnp.where(lane == 2, g1, meta)
    meta = jnp.where(lane == 3, g2, meta)
    meta = jnp.where(lane == 4, p1, meta)
    meta = jnp.where(lane == 5, p2, meta)
    meta_o[...] = meta
    cnt_o[...] = run_sc[...]


def _c_out_router(o, x2d, wo, g, rhi, rlo, tri, tm):
    S = x2d.shape[0]
    row = lambda i: (i, 0)
    const = lambda i: (0, 0)
    return pl.pallas_call(
        _c_out_router_kernel,
        grid=(S // tm,),
        in_specs=[pl.BlockSpec((tm, D_MODEL), row), pl.BlockSpec((tm, D_MODEL), row),
                  pl.BlockSpec((D_MODEL, D_MODEL), const), pl.BlockSpec((1, D_MODEL), const),
                  pl.BlockSpec((D_MODEL, LANES), const), pl.BlockSpec((D_MODEL, LANES), const),
                  pl.BlockSpec((tm, tm), const)],
        out_specs=[pl.BlockSpec((tm, D_MODEL), row),
                   pl.BlockSpec((tm, ROW_TILES, LANES), lambda i: (i, 0, 0)),
                   pl.BlockSpec((tm, LANES), row), pl.BlockSpec((8, LANES), const)],
        out_shape=(jax.ShapeDtypeStruct((S, D_MODEL), F32),
                   jax.ShapeDtypeStruct((S, ROW_TILES, LANES), F32),
                   jax.ShapeDtypeStruct((S, LANES), F32), jax.ShapeDtypeStruct((8, LANES), F32)),
        scratch_shapes=[pltpu.VMEM((8, LANES), F32)],
        compiler_params=_cparams(("arbitrary",)),
        name="c_out_router",
    )(o, x2d, wo, g, rhi, rlo, tri)


MOE_TM = 256
PAD_PIECES = (128, 64, 32, 16, 8, 4, 2, 1)


def _dispatch_kernel(pad_ref, slot_ref, h_ref, xs_ref, zero_sc, sem, zsem, *, ts):
    i = pl.program_id(0)

    @pl.when(i == 0)
    def _():
        zero_sc[...] = jnp.zeros_like(zero_sc)
        for e in range(N_EXPERTS):
            start = pad_ref[0, e]
            npad = pad_ref[1, e]
            for p in PAD_PIECES:
                hit = (npad & p) != 0

                @pl.when(hit)
                def _(start=start, p=p):
                    cp = pltpu.make_async_copy(zero_sc.at[pl.ds(0, p)],
                                               xs_ref.at[pl.ds(start, p)], zsem)
                    cp.start()
                    cp.wait()

                start = start + jnp.where(hit, p, 0)

        half = MOE_TM // 2

        def zero_tile(tile, c):
            for part in range(2):
                cp = pltpu.make_async_copy(
                    zero_sc, xs_ref.at[pl.ds(tile * MOE_TM + part * half, half)], zsem)
                cp.start()
                cp.wait()
            return c

        lax.fori_loop(pad_ref[2, 0], xs_ref.shape[0] // MOE_TM, zero_tile, 0)

    def row_copy(r, k):
        return pltpu.make_async_copy(h_ref.at[r], xs_ref.at[slot_ref[0, 0, 2 * r + k]], sem)

    def issue(r, c):
        row_copy(r, 0).start(priority=0)
        row_copy(r, 1).start(priority=1)
        return c

    lax.fori_loop(0, ts, issue, 0, unroll=8)
    for _ in range(2):
        pltpu.make_async_copy(h_ref, xs_ref.at[pl.ds(0, ts)], sem).wait()


def _dispatch(pad_info, slots3, h2, n_slots, ts):
    S = h2.shape[0]
    return pl.pallas_call(
        functools.partial(_dispatch_kernel, ts=ts),
        grid=(S // ts,),
        in_specs=[pl.BlockSpec(memory_space=pltpu.SMEM),
                  pl.BlockSpec((1, 1, 2 * ts), lambda i: (i, 0, 0), memory_space=pltpu.SMEM),
                  pl.BlockSpec((ts, ROW_TILES, LANES), lambda i: (i, 0, 0))],
        out_specs=pl.BlockSpec(memory_space=pl.ANY),
        out_shape=jax.ShapeDtypeStruct((n_slots, ROW_TILES, LANES), F32),
        scratch_shapes=[pltpu.VMEM((MOE_TM // 2, ROW_TILES, LANES), F32),
                        pltpu.SemaphoreType.DMA(()), pltpu.SemaphoreType.DMA(())],
        compiler_params=_cparams(("arbitrary",)),
        name="moe_dispatch",
    )(pad_info, slots3, h2)


MOE_FC = 512


def _moe_kernel(te_ref, tv_ref, xs_ref, wg_ref, wu_ref, wd_ref, y_ref, x_sc, act_sc):
    t = pl.program_id(0)

    @pl.when(tv_ref[t] == 0)
    def _():
        y_ref[...] = jnp.zeros_like(y_ref)

    @pl.when(tv_ref[t] != 0)
    def _():
        for c in range(ROW_TILES):
            x_sc[:, c * LANES:(c + 1) * LANES] = xs_ref[:, c, :].astype(BF16)
        x = x_sc[...]
        for c in range(D_FF_EXPERT // MOE_FC):
            sl = slice(c * MOE_FC, (c + 1) * MOE_FC)
            gate = _dot(x, wg_ref[:, sl])
            up = _dot(x, wu_ref[:, sl])
            act_sc[:, sl] = (gate * jax.nn.sigmoid(gate) * up).astype(BF16)
        acc = _dot(act_sc[...], wd_ref[...])
        for c in range(ROW_TILES):
            y_ref[:, c, :] = acc[:, c * LANES:(c + 1) * LANES]


def _moe(tile_expert, tile_valid, xs, w_gu, w_down):
    n_slots = xs.shape[0]
    nt = n_slots // MOE_TM
    grid_spec = pltpu.PrefetchScalarGridSpec(
        num_scalar_prefetch=2,
        grid=(nt,),
        in_specs=[
            pl.BlockSpec((MOE_TM, ROW_TILES, LANES), lambda t, te, tv: (t, 0, 0)),
            pl.BlockSpec((None, D_MODEL, D_FF_EXPERT), lambda t, te, tv: (te[t], 0, 0),
                         pipeline_mode=pl.Buffered(1)),
            pl.BlockSpec((None, D_MODEL, D_FF_EXPERT), lambda t, te, tv: (te[t], 0, 1),
                         pipeline_mode=pl.Buffered(1)),
            pl.BlockSpec((None, D_FF_EXPERT, D_MODEL), lambda t, te, tv: (te[t], 0, 0),
                         pipeline_mode=pl.Buffered(1)),
        ],
        out_specs=pl.BlockSpec((MOE_TM, ROW_TILES, LANES), lambda t, te, tv: (t, 0, 0)),
        scratch_shapes=[pltpu.VMEM((MOE_TM, D_MODEL), BF16),
                        pltpu.VMEM((MOE_TM, D_FF_EXPERT), BF16)],
    )
    return pl.pallas_call(
        _moe_kernel,
        grid_spec=grid_spec,
        out_shape=jax.ShapeDtypeStruct((n_slots, ROW_TILES, LANES), F32),
        compiler_params=_cparams(("arbitrary",)),
        name="moe_experts",
    )(tile_expert, tile_valid, xs, w_gu, w_gu, w_down)


def _combine_kernel(slot_ref, next_slot_ref, x3_ref, meta_ref, y_ref, o_ref, buf, sem, *, ts):
    i = pl.program_id(0)
    n = pl.num_programs(0)
    cur = lax.rem(i, 2)

    def gather(slots, half):
        def issue(r, c):
            for k in range(2):
                pltpu.make_async_copy(y_ref.at[slots[0, 0, 2 * r + k]], buf.at[half, k, r],
                                      sem.at[half]).start(priority=k)
            return c

        lax.fori_loop(0, ts, issue, 0, unroll=8)

    @pl.when(i == 0)
    def _():
        gather(slot_ref, 0)

    @pl.when(i + 1 < n)
    def _():
        gather(next_slot_ref, 1 - cur)

    for k in range(2):
        pltpu.make_async_copy(y_ref.at[pl.ds(0, ts)], buf.at[cur, k], sem.at[cur]).wait()
    meta = meta_ref[...]
    g1 = jnp.broadcast_to(meta[:, 2:3], (ts, LANES))
    g2 = jnp.broadcast_to(meta[:, 3:4], (ts, LANES))
    for c in range(ROW_TILES):
        sl = slice(c * LANES, (c + 1) * LANES)
        o_ref[:, sl] = x3_ref[:, sl] + g1 * buf[cur, 0, :, c, :] + g2 * buf[cur, 1, :, c, :]


def _combine(slots3, x3, meta, y, ts):
    S = x3.shape[0]
    n = S // ts
    return pl.pallas_call(
        functools.partial(_combine_kernel, ts=ts),
        grid=(n,),
        in_specs=[pl.BlockSpec((1, 1, 2 * ts), lambda i: (i, 0, 0), memory_space=pltpu.SMEM),
                  pl.BlockSpec((1, 1, 2 * ts), lambda i: (jnp.minimum(i + 1, n - 1), 0, 0),
                               memory_space=pltpu.SMEM),
                  pl.BlockSpec((ts, D_MODEL), lambda i: (i, 0)),
                  pl.BlockSpec((ts, LANES), lambda i: (i, 0)),
                  pl.BlockSpec(memory_space=pl.ANY)],
        out_specs=pl.BlockSpec((ts, D_MODEL), lambda i: (i, 0)),
        out_shape=jax.ShapeDtypeStruct((S, D_MODEL), F32),
        scratch_shapes=[pltpu.VMEM((2, 2, ts, ROW_TILES, LANES), F32),
                        pltpu.SemaphoreType.DMA((2,))],
        compiler_params=_cparams(("arbitrary",)),
        name="moe_combine",
    )(slots3, slots3, x3, meta, y)


def _tile_lanes(v, reps):
    return jnp.tile(v, reps)[None, :].astype(F32)


def _rope_tables_64(S):
    pos = jnp.arange(S, dtype=F32)
    inv = ROPE_THETA ** (-jnp.arange(0, A_HEAD_DIM, 2, dtype=F32) / A_HEAD_DIM)
    ang = pos[:, None] * inv[None, :]
    c, s = jnp.cos(ang), jnp.sin(ang)
    return jnp.concatenate([c, c, c, c], axis=1), jnp.concatenate([-s, s, -s, s], axis=1)


def _rope_tables_32(S):
    pos = jnp.arange(S, dtype=F32)
    inv = ROPE_THETA ** (-jnp.arange(0, C_ROPE, 2, dtype=F32) / C_ROPE)
    ang = pos[:, None] * inv[None, :]
    c, s = jnp.cos(ang), jnp.sin(ang)
    one = jnp.ones((S, C_NOPE), F32)
    zero = jnp.zeros((S, C_NOPE), F32)
    pad = jnp.zeros((S, LANES - C_QK), F32)
    return (jnp.concatenate([one, c, c, pad], axis=1),
            jnp.concatenate([zero, -s, s, pad], axis=1))


def _pad_heads(w, n_heads, width):
    k = w.shape[0]
    w = w.reshape(k, n_heads, width)
    return jnp.pad(w, ((0, 0), (0, 0), (0, LANES - width))).reshape(k, n_heads * LANES)


def _swap_rope_cols(w_p):
    k = w_p.shape[0]
    w = w_p.reshape(k, -1, LANES)
    half = C_ROPE // 2
    sw = jnp.concatenate([jnp.zeros_like(w[:, :, :C_NOPE]),
                          w[:, :, C_NOPE + half:C_QK], w[:, :, C_NOPE:C_NOPE + half],
                          jnp.zeros_like(w[:, :, C_QK:])], axis=2)
    return sw.reshape(k, -1)


def kernel(x, l0_norm_attn, l0_w_in, l0_a_q_norm, l0_a_k_norm, l0_a_lambda, l0_a_subln, l0_b_q_norm, l0_b_k_norm, l0_b_sinks, l0_w_out, l0_norm_ffn, l0_ffn_w_gu, l0_ffn_w_down, l1_norm_attn, l1_c_w_in, l1_c_q_lora_norm, l1_c_kv_lora_norm, l1_c_w_uq, l1_c_w_ukv, l1_c_q_norm, l1_c_k_norm, l1_c_w_out, l1_norm_ffn, l1_router, l1_exp_w_gu, l1_exp_w_down):
    B, S, _ = x.shape
    assert B == 1
    x2d = x.reshape(S, D_MODEL)
    tm = min(512, S)

    o_bk = 3 * A_W + B_QW
    o_bv = o_bk + B_KW
    bk_w = l0_w_in[:, o_bk:o_bv].reshape(D_MODEL, B_KV_HEADS, B_HEAD_DIM)
    bv_w = l0_w_in[:, o_bv:].reshape(D_MODEL, B_KV_HEADS, B_HEAD_DIM)
    dup = lambda w: jnp.concatenate([w, w], axis=2).reshape(D_MODEL, 2 * B_KW)
    w_in_p = jnp.concatenate([l0_w_in[:, :2 * A_W], l0_w_in[:, 3 * A_W:o_bk], dup(bk_w),
                              dup(bv_w)], axis=1).astype(BF16)
    w_avt = l0_w_in[:, 2 * A_W:3 * A_W].T.reshape(A_HEADS, 2 * A_HEAD_DIM, D_MODEL)
    w_avt = jnp.pad(w_avt, ((0, 0), (0, V_PAD), (0, 0))).reshape(-1, D_MODEL).astype(BF16)
    a_vone = jnp.zeros((A_HEADS, A_V_ROWS, 1), F32).at[:, 2 * A_HEAD_DIM, 0].set(1.0).reshape(-1, 1)
    gains = jnp.concatenate([_tile_lanes(l0_a_q_norm, 2), _tile_lanes(l0_a_k_norm, 2),
                             _tile_lanes(l0_b_q_norm, 2), _tile_lanes(l0_b_k_norm, 2),
                             jnp.zeros((4, LANES), F32)], axis=0)
    lane = jnp.arange(LANES)
    ones64 = (lane[:, None] // 64 == lane[None, :] // 64).astype(BF16)
    cos64, sin64 = _rope_tables_64(S)
    aq, ak, avt, bq, bk, bv = _l0_in(x2d, l0_norm_attn[None, :], w_in_p, w_avt, a_vone, gains,
                                     ones64, cos64, sin64, tm)
    lam_init = 0.8 - 0.6 * math.exp(-0.3 * 0)
    a_out = _a_attn(l0_a_lambda.astype(F32), aq, ak, avt, l0_a_subln[None, :].astype(F32),
                    lam_init)
    b_out = _b_attn(l0_b_sinks.astype(F32), bq, bk, bv)
    w_out = l0_w_out.astype(BF16)
    x2 = _l0_ffn(a_out, b_out, x2d, w_out[:A_W], w_out[A_W:], l0_norm_ffn[None, :],
                 l0_ffn_w_gu.astype(BF16), l0_ffn_w_down.astype(BF16), tm, D_FF // 2)

    wq = l1_c_w_in[:, :C_Q_RANK].astype(BF16)
    wkv = l1_c_w_in[:, C_Q_RANK:C_Q_RANK + C_KV_RANK].astype(BF16)
    wkr = jnp.pad(l1_c_w_in[:, C_Q_RANK + C_KV_RANK:], ((0, 0), (C_NOPE, LANES - C_QK)))
    wkrs = _swap_rope_cols(wkr).astype(BF16)
    wkr = wkr.astype(BF16)
    wuq = _pad_heads(l1_c_w_uq, C_HEADS, C_QK)
    wuqs = _swap_rope_cols(wuq).astype(BF16)
    wuq = wuq.astype(BF16)
    ukv = l1_c_w_ukv.reshape(C_KV_RANK, C_HEADS, C_NOPE + C_V)
    wuk = _pad_heads(ukv[:, :, :C_NOPE].reshape(C_KV_RANK, -1), C_HEADS, C_NOPE).astype(BF16)
    wuv = jnp.pad(jnp.transpose(ukv[:, :, C_NOPE:], (1, 2, 0)), ((0, 0), (0, V_PAD), (0, 0)))
    wuv = wuv.reshape(C_HEADS * V_ROWS, C_KV_RANK).astype(BF16)
    vone = jnp.zeros((C_HEADS, V_ROWS, 1), F32).at[:, C_V, 0].set(1.0).reshape(-1, 1)
    pad_gain = lambda gvec: jnp.pad(gvec.astype(F32), (0, LANES - C_QK))[None, :]
    qg, kg = pad_gain(l1_c_q_norm), pad_gain(l1_c_k_norm)
    qgs, kgs = _swap_rope_cols(qg), _swap_rope_cols(kg)
    ones128 = jnp.ones((LANES, LANES), BF16)
    cos32, sin32 = _rope_tables_32(S)
    cq, ck, cvt = _c_in(x2, l1_norm_attn[None, :], wq, wkv, wkr, wkrs,
                       l1_c_q_lora_norm[None, :], l1_c_kv_lora_norm[None, :],
                       wuq, wuqs, wuk, wuv, qg, qgs, kg, kgs, vone, ones128, cos32, sin32, tm)
    c_o = _c_attn(cq, ck, cvt)

    r_pad = jnp.pad(l1_router.astype(F32), ((0, 0), (0, LANES - N_EXPERTS)))
    r_hi = r_pad.astype(BF16)
    r_lo = (r_pad - r_hi.astype(F32)).astype(BF16)
    tr = min(256, S)
    ridx = jnp.arange(tr)
    tri = (ridx[None, :] < ridx[:, None]).astype(BF16)
    x3, h2, meta, cnt = _c_out_router(c_o, x2, l1_c_w_out.astype(BF16), l1_norm_ffn[None, :],
                                      r_hi, r_lo, tri, tr)

    counts = cnt[0, :N_EXPERTS].astype(jnp.int32)
    padded = ((counts + MOE_TM - 1) // MOE_TM) * MOE_TM
    ends = jnp.cumsum(padded)
    offs = ends - padded
    idx = meta[:, 0:2].astype(jnp.int32)
    pos = meta[:, 4:6].astype(jnp.int32)
    slots = offs[idx] + pos
    n_slots = 2 * S + N_EXPERTS * MOE_TM
    nt = n_slots // MOE_TM
    tile_start = jnp.arange(nt, dtype=jnp.int32) * MOE_TM
    n_valid = jnp.broadcast_to(ends[-1] // MOE_TM, (N_EXPERTS,))
    tile_valid = (tile_start < ends[-1]).astype(jnp.int32)
    tile_expert = jnp.minimum(
        jnp.sum((tile_start[:, None] >= ends[None, :]).astype(jnp.int32), axis=1),
        N_EXPERTS - 1).astype(jnp.int32)
    pad_info = jnp.stack([offs + counts, padded - counts, n_valid]).astype(jnp.int32)

    ts = min(512, S)
    slots3 = slots.reshape(S // ts, 1, 2 * ts)
    xs = _dispatch(pad_info, slots3, h2, n_slots, ts)
    y = _moe(tile_expert, tile_valid, xs,
             l1_exp_w_gu.astype(BF16), l1_exp_w_down.astype(BF16))
    out = _combine(slots3, x3, meta, y, ts)
    return out.reshape(B, S, D_MODEL)
```

```python
import functools
import math

import jax
import jax.numpy as jnp
import numpy as np
from jax import lax
from jax.experimental import pallas as pl
from jax.experimental.pallas import tpu as pltpu

F32 = jnp.float32
BF16 = jnp.bfloat16

D_MODEL = 1024
CHUNK = 64
ROPE_THETA = 10000.0
EPS = 1e-6
LANES = 128
ROW_TILES = D_MODEL // LANES

A_HEADS = 4
A_HEAD_DIM = 64
B_HEADS = 8
B_KV_HEADS = 2
B_HEAD_DIM = 64
A_W = A_HEADS * 2 * A_HEAD_DIM
B_QW = B_HEADS * B_HEAD_DIM
B_KW = B_KV_HEADS * B_HEAD_DIM

C_HEADS = 16
C_Q_RANK = 256
C_KV_RANK = 128
C_NOPE = 64
C_ROPE = 32
C_V = 64
C_QK = C_NOPE + C_ROPE

D_FF = 2816
N_EXPERTS = 8
D_FF_EXPERT = 3584

NEG_BIG = -1e30
LOG2E = 1.0 / math.log(2.0)
VMEM_LIMIT = 56 * 1024 * 1024


def _cparams(sem):
    return pltpu.CompilerParams(dimension_semantics=sem, vmem_limit_bytes=VMEM_LIMIT)


def _dot(a, b):
    return jnp.dot(a, b, preferred_element_type=F32)


def _dot_nt(a, b):
    return lax.dot_general(a, b, (((1,), (1,)), ((), ())), preferred_element_type=F32)


def _rms(x, g):
    ms = jnp.mean(x * x, axis=-1, keepdims=True)
    return x * lax.rsqrt(ms + EPS) * g


def _l0_in_kernel(x_ref, g_ref, w_ref, wvt_ref, vone_ref, gains_ref, ones_ref, cos_ref, sin_ref,
                  aq_o, ak_o, avt_o, bq_o, bk_o, bv_o):
    h = _rms(x_ref[...], g_ref[...]).astype(BF16)
    z = _dot(h, w_ref[...])
    avt_o[...] = (_dot_nt(wvt_ref[...], h) + vone_ref[...]).astype(BF16)
    cos = cos_ref[...]
    sin = sin_ref[...]
    ones_blk = ones_ref[...]
    lane = lax.broadcasted_iota(jnp.int32, (1, LANES), 1)
    first_half = (lane % 64) < 32

    def norm_rope(zs, gain, scale):
        ss = _dot((zs * zs).astype(BF16), ones_blk)
        r = lax.rsqrt(ss * (1.0 / 64.0) + EPS) * scale
        y = zs * gain
        sw = jnp.where(first_half, pltpu.roll(y, 96, 1), pltpu.roll(y, 32, 1))
        return ((y * cos + sw * sin) * r).astype(BF16)

    col = 0
    gcol = 0
    scale = A_HEAD_DIM ** -0.5
    scales = {"aq": scale * LOG2E, "ak": 1.0, "bq": scale, "bk": 1.0}
    outs = {"aq": aq_o, "ak": ak_o, "bq": bq_o, "bk": bk_o}
    for name, nslab in (("aq", 4), ("ak", 4), ("bq", 4), ("bk", 2), ("bv", 2)):
        if name == "bv":
            bv_o[...] = z[:, col:col + 2 * LANES].astype(BF16)
            col += 2 * LANES
            continue
        sc = scales[name]
        gain = gains_ref[gcol:gcol + 1, :]
        gcol += 1
        for s in range(nslab):
            zs = z[:, col:col + LANES]
            outs[name][:, s * LANES:(s + 1) * LANES] = norm_rope(zs, gain, sc)
            col += LANES


def _l0_in(x2d, g, w_in_p, w_avt, vone, gains, ones_blk, cos_t, sin_t, tm):
    S = x2d.shape[0]
    nw = w_in_p.shape[1]
    row = lambda i: (i, 0)
    const = lambda i: (0, 0)
    out_shapes = (
        jax.ShapeDtypeStruct((S, A_W), BF16), jax.ShapeDtypeStruct((S, A_W), BF16),
        jax.ShapeDtypeStruct((A_HEADS * A_V_ROWS, S), BF16), jax.ShapeDtypeStruct((S, B_QW), BF16),
        jax.ShapeDtypeStruct((S, 2 * B_KW), BF16), jax.ShapeDtypeStruct((S, 2 * B_KW), BF16))
    return pl.pallas_call(
        _l0_in_kernel,
        grid=(S // tm,),
        in_specs=[
            pl.BlockSpec((tm, D_MODEL), row),
            pl.BlockSpec((1, D_MODEL), const),
            pl.BlockSpec((D_MODEL, nw), const),
            pl.BlockSpec((A_HEADS * A_V_ROWS, D_MODEL), const),
            pl.BlockSpec((A_HEADS * A_V_ROWS, 1), const),
            pl.BlockSpec((8, LANES), const),
            pl.BlockSpec((LANES, LANES), const),
            pl.BlockSpec((tm, LANES), row),
            pl.BlockSpec((tm, LANES), row),
        ],
        out_specs=[
            pl.BlockSpec((tm, A_W), row), pl.BlockSpec((tm, A_W), row),
            pl.BlockSpec((A_HEADS * A_V_ROWS, tm), lambda i: (0, i)), pl.BlockSpec((tm, B_QW), row),
            pl.BlockSpec((tm, 2 * B_KW), row), pl.BlockSpec((tm, 2 * B_KW), row)],
        out_shape=out_shapes,
        compiler_params=_cparams(("arbitrary",)),
        name="l0_in_proj",
    )(x2d, g, w_in_p, w_avt, vone, gains, ones_blk, cos_t, sin_t)


ATT_TQ = 512
ATT_TK = 256
V_PAD = 16


def _softmax_pv(s_ref, s_max, vt, m, acc):
    m_new = jnp.maximum(m, s_max)
    alpha = jnp.exp2(m - m_new)
    p = jnp.exp2(s_ref[...] - m_new).astype(BF16)
    return m_new, alpha * acc + _dot(vt, p)


def _diag_masks():
    key_chunk = lax.broadcasted_iota(jnp.int32, (ATT_TK, ATT_TQ), 0) // CHUNK
    query_chunk = lax.broadcasted_iota(jnp.int32, (ATT_TK, ATT_TQ), 1) // CHUNK
    return [key_chunk + b * (ATT_TK // CHUNK) <= query_chunk for b in range(ATT_TQ // ATT_TK)]


def _attn_pipeline(i, k_ref, vt_ref, streams, v_rows):
    n = len(streams)

    for st in streams:
        st[5][...] = st[0].T

    def scores(st, blk, mask, dst):
        lanes, qt_ref = streams[st][1], streams[st][5]
        off = pl.multiple_of(blk * ATT_TK, ATT_TK)
        s = _dot(k_ref[pl.ds(off, ATT_TK), lanes], qt_ref[...])
        if mask is not None:
            s = jnp.where(mask, s, NEG_BIG)
        dst[...] = s
        return jnp.max(s, axis=0, keepdims=True)

    def consume(st, blk, src, s_max, state):
        off = pl.multiple_of(blk * ATT_TK, ATT_TK)
        return _softmax_pv(src, s_max, vt_ref[streams[st][2], pl.ds(off, ATT_TK)], *state)

    buf_a = [st[3] for st in streams]
    buf_b = [st[4] for st in streams]
    mask0, mask1 = _diag_masks()
    d0 = 2 * i
    d1 = d0 + 1
    state = [(jnp.full((1, ATT_TQ), NEG_BIG, F32), jnp.zeros((v_rows, ATT_TQ), F32))] * n
    max_a = [scores(st, d0, mask0, buf_a[st]) for st in range(n)]
    max_b = [None] * n
    for st in range(n):
        max_b[st] = scores(st, d1, mask1, buf_b[st])
        state[st] = consume(st, d0, buf_a[st], max_a[st], state[st])
    for st in range(n):
        max_a[st] = scores(st, 0, None, buf_a[st])
        state[st] = consume(st, d1, buf_b[st], max_b[st], state[st])

    def body(p, carry):
        carry, max_a = list(carry[0]), list(carry[1])
        max_b = [None] * n
        u0 = 2 * p
        u1 = u0 + 1
        nxt = jnp.minimum(u0 + 2, d0 - 1)
        for st in range(n):
            max_b[st] = scores(st, u1, None, buf_b[st])
            carry[st] = consume(st, u0, buf_a[st], max_a[st], carry[st])
        for st in range(n):
            max_a[st] = scores(st, nxt, None, buf_a[st])
            carry[st] = consume(st, u1, buf_b[st], max_b[st], carry[st])
        return tuple(carry), tuple(max_a)

    final, _ = lax.fori_loop(0, i, body, (tuple(state), tuple(max_a)))
    return [acc for _, acc in final]


_SCORE_BUF = pltpu.VMEM((ATT_TK, ATT_TQ), F32)
_QT_BUF = pltpu.VMEM((LANES, ATT_TQ), BF16)
_ATTN_SCRATCH = [_SCORE_BUF] * 4 + [_QT_BUF] * 2


A_V_ROWS = 2 * A_HEAD_DIM + V_PAD


A_HPS = 2


def _a_attn_kernel(lam_ref, q_ref, k_ref, vt_ref, subln_ref, o_ref, *scratch, lam_init):
    i = pl.program_id(1)
    lane = lax.broadcasted_iota(jnp.int32, (1, LANES), 1)
    streams = []
    for h in range(A_HPS):
        q = q_ref[:, h * LANES:(h + 1) * LANES]
        zero = jnp.zeros_like(q)
        lanes = slice(h * LANES, (h + 1) * LANES)
        rows = slice(h * A_V_ROWS, (h + 1) * A_V_ROWS)
        for mp, sel in enumerate((lane < 64, lane >= 64)):
            st = 2 * h + mp
            streams.append((jnp.where(sel, q, zero), lanes, rows, scratch[2 * st],
                            scratch[2 * st + 1], scratch[4 * A_HPS + st]))
    accs = _attn_pipeline(i, k_ref, vt_ref, tuple(streams), A_V_ROWS)
    dv = 2 * A_HEAD_DIM
    lf = lam_ref[...]
    lam = (jnp.exp(jnp.sum(lf[0:1] * lf[1:2], axis=1, keepdims=True))
           - jnp.exp(jnp.sum(lf[2:3] * lf[3:4], axis=1, keepdims=True)) + lam_init)
    for h in range(A_HPS):
        acc1, acc2 = accs[2 * h], accs[2 * h + 1]
        out = (acc1[:dv] / acc1[dv:dv + 1] - lam * (acc2[:dv] / acc2[dv:dv + 1])).T
        out = _rms(out, subln_ref[...]) * (1.0 - lam_init)
        o_ref[:, h * LANES:(h + 1) * LANES] = out.astype(BF16)


def _a_attn(lam_p, aq, ak, avt, subln, lam_init):
    S = aq.shape[0]
    once = pl.Buffered(1)
    return pl.pallas_call(
        functools.partial(_a_attn_kernel, lam_init=lam_init),
        grid=(A_HEADS // A_HPS, S // ATT_TQ),
        in_specs=[
            pl.BlockSpec((4, A_HEAD_DIM), lambda h, i: (0, 0)),
            pl.BlockSpec((ATT_TQ, A_HPS * LANES), lambda h, i: (i, h)),
            pl.BlockSpec((S, A_HPS * LANES), lambda h, i: (0, h), pipeline_mode=once),
            pl.BlockSpec((A_HPS * A_V_ROWS, S), lambda h, i: (h, 0), pipeline_mode=once),
            pl.BlockSpec((1, LANES), lambda h, i: (0, 0)),
        ],
        out_specs=pl.BlockSpec((ATT_TQ, A_HPS * LANES), lambda h, i: (i, h)),
        out_shape=jax.ShapeDtypeStruct((S, A_W), BF16),
        scratch_shapes=[_SCORE_BUF] * (4 * A_HPS) + [_QT_BUF] * (2 * A_HPS),
        compiler_params=_cparams(("arbitrary", "arbitrary")),
        name="a_diff_attn",
    )(lam_p, aq, ak, avt, subln)


B_BLK = 256
B_BACK = 128


def _b_attn_kernel(sink_ref, q_ref, kp_ref, kc_ref, vp_ref, vc_ref, o_ref):
    i = pl.program_id(0)
    lane = lax.broadcasted_iota(jnp.int32, (1, LANES), 1)
    lo = lane < 64
    r = lax.broadcasted_iota(jnp.int32, (B_BLK, B_BACK + B_BLK), 0) // CHUNK
    c_idx = lax.broadcasted_iota(jnp.int32, (B_BLK, B_BACK + B_BLK), 1)
    c = c_idx // CHUNK
    mask = (c >= r) & (c <= r + 2) & ((c_idx >= B_BACK) | (i > 0))
    for slab in range(B_HEADS // 2):
        g = slab // 2
        k = jnp.concatenate([kp_ref[:, g * LANES:(g + 1) * LANES],
                             kc_ref[:, g * LANES:(g + 1) * LANES]], axis=0)
        v = jnp.concatenate([vp_ref[:, g * LANES:(g + 1) * LANES],
                             vc_ref[:, g * LANES:(g + 1) * LANES]], axis=0)
        qs = q_ref[:, slab * LANES:(slab + 1) * LANES]
        zq = jnp.zeros_like(qs)
        zv = jnp.zeros_like(v)
        out = jnp.zeros((B_BLK, LANES), F32)
        for half in range(2):
            sel = lo if half == 0 else jnp.logical_not(lo)
            sink = sink_ref[2 * slab + half]
            s = _dot_nt(jnp.where(sel, qs, zq), k)
            s = jnp.where(mask, s, NEG_BIG)
            m = jnp.maximum(jnp.max(s, axis=1, keepdims=True), sink)
            e = jnp.exp(s - m)
            denom = jnp.sum(e, axis=1, keepdims=True) + jnp.exp(sink - m)
            p = (e / denom).astype(BF16)
            out = out + _dot(p, jnp.where(sel, v, zv))
        o_ref[:, slab * LANES:(slab + 1) * LANES] = out.astype(BF16)


def _b_attn(sinks, bq, bk, bv):
    S = bq.shape[0]
    prev = lambda i: (jnp.maximum(i * (B_BLK // B_BACK) - 1, 0), 0)
    cur = lambda i: (i, 0)
    return pl.pallas_call(
        _b_attn_kernel,
        grid=(S // B_BLK,),
        in_specs=[
            pl.BlockSpec(memory_space=pltpu.SMEM),
            pl.BlockSpec((B_BLK, B_QW), cur),
            pl.BlockSpec((B_BACK, 2 * B_KW), prev),
            pl.BlockSpec((B_BLK, 2 * B_KW), cur),
            pl.BlockSpec((B_BACK, 2 * B_KW), prev),
            pl.BlockSpec((B_BLK, 2 * B_KW), cur),
        ],
        out_specs=pl.BlockSpec((B_BLK, B_QW), cur),
        out_shape=jax.ShapeDtypeStruct((S, B_QW), BF16),
        compiler_params=_cparams(("arbitrary",)),
        name="b_swa_attn",
    )(sinks, bq, bk, bk, bv, bv)


def _l0_ffn_kernel(a_ref, b_ref, x_ref, woa_ref, wob_ref, g_ref, wg_ref, wu_ref, wd_ref,
                   o_ref, h_sc, acc_sc):
    j = pl.program_id(1)

    @pl.when(j == 0)
    def _():
        x1 = x_ref[...] + _dot(a_ref[...], woa_ref[...]) + _dot(b_ref[...], wob_ref[...])
        acc_sc[...] = x1
        h_sc[...] = _rms(x1, g_ref[...]).astype(BF16)

    h = h_sc[...]
    gate = _dot(h, wg_ref[...])
    up = _dot(h, wu_ref[...])
    act = (gate * jax.nn.sigmoid(gate) * up).astype(BF16)
    acc_sc[...] += _dot(act, wd_ref[...])

    @pl.when(j == pl.num_programs(1) - 1)
    def _():
        o_ref[...] = acc_sc[...]


def _l0_ffn(a_out, b_out, x2d, wo_a, wo_b, g, w_gu, w_down, tm, tf):
    S = x2d.shape[0]
    nf = D_FF // tf
    row = lambda i, j: (i, 0)
    const = lambda i, j: (0, 0)
    return pl.pallas_call(
        _l0_ffn_kernel,
        grid=(S // tm, nf),
        in_specs=[
            pl.BlockSpec((tm, A_W), row),
            pl.BlockSpec((tm, B_QW), row),
            pl.BlockSpec((tm, D_MODEL), row),
            pl.BlockSpec((A_W, D_MODEL), const),
            pl.BlockSpec((B_QW, D_MODEL), const),
            pl.BlockSpec((1, D_MODEL), const),
            pl.BlockSpec((D_MODEL, tf), lambda i, j: (0, j)),
            pl.BlockSpec((D_MODEL, tf), lambda i, j: (0, nf + j)),
            pl.BlockSpec((tf, D_MODEL), lambda i, j: (j, 0)),
        ],
        out_specs=pl.BlockSpec((tm, D_MODEL), row),
        out_shape=jax.ShapeDtypeStruct((S, D_MODEL), F32),
        scratch_shapes=[pltpu.VMEM((tm, D_MODEL), BF16), pltpu.VMEM((tm, D_MODEL), F32)],
        compiler_params=_cparams(("arbitrary", "arbitrary")),
        name="l0_out_ffn",
    )(a_out, b_out, x2d, wo_a, wo_b, g, w_gu, w_gu, w_down)


def _c_in_kernel(x_ref, g_ref, wq_ref, wkv_ref, wkr_ref, wkrs_ref, gql_ref, gkvl_ref,
                 wuq_ref, wuqs_ref, wuk_ref, wuv_ref, qg_ref, qgs_ref, kg_ref, kgs_ref,
                 vone_ref, ones_ref, cos_ref, sin_ref, q_o, k_o, vt_o):
    h = _rms(x_ref[...], g_ref[...]).astype(BF16)
    cq = _rms(_dot(h, wq_ref[...]), gql_ref[...]).astype(BF16)
    ckv = _rms(_dot(h, wkv_ref[...]), gkvl_ref[...]).astype(BF16)
    kr = _dot(h, wkr_ref[...])
    krs = _dot(h, wkrs_ref[...])
    q = _dot(cq, wuq_ref[...])
    qs = _dot(cq, wuqs_ref[...])
    kn = _dot(ckv, wuk_ref[...])
    vt_o[...] = (_dot_nt(wuv_ref[...], ckv) + vone_ref[...]).astype(BF16)

    cos = cos_ref[...]
    sin = sin_ref[...]
    ones_blk = ones_ref[...]
    qg, qgs, kg, kgs = qg_ref[...], qgs_ref[...], kg_ref[...], kgs_ref[...]
    ss_kr = _dot((kr * kr).astype(BF16), ones_blk)
    kr_roped = kr * kg * cos + krs * kgs * sin
    scale = C_QK ** -0.5 * LOG2E
    inv = 1.0 / C_QK
    for hd in range(C_HEADS):
        sl = slice(hd * LANES, (hd + 1) * LANES)
        qh = q[:, sl]
        r = lax.rsqrt(_dot((qh * qh).astype(BF16), ones_blk) * inv + EPS) * scale
        q_o[:, sl] = ((qh * qg * cos + qs[:, sl] * qgs * sin) * r).astype(BF16)
        kh = kn[:, sl]
        rk = lax.rsqrt((_dot((kh * kh).astype(BF16), ones_blk) + ss_kr) * inv + EPS)
        k_o[:, sl] = ((kh * kg + kr_roped) * rk).astype(BF16)


def _c_in(x2d, g, wq, wkv, wkr, wkrs, gql, gkvl, wuq, wuqs, wuk, wuv, qg, qgs, kg, kgs,
          vone, ones_blk, cos_t, sin_t, tm):
    S = x2d.shape[0]
    row = lambda i: (i, 0)
    const = lambda i: (0, 0)
    full = lambda a: pl.BlockSpec(a.shape, const)
    W = C_HEADS * LANES
    return pl.pallas_call(
        _c_in_kernel,
        grid=(S // tm,),
        in_specs=[pl.BlockSpec((tm, D_MODEL), row), full(g), full(wq), full(wkv), full(wkr),
                  full(wkrs), full(gql), full(gkvl), full(wuq), full(wuqs), full(wuk),
                  full(wuv), full(qg), full(qgs), full(kg), full(kgs), full(vone),
                  full(ones_blk),
                  pl.BlockSpec((tm, LANES), row), pl.BlockSpec((tm, LANES), row)],
        out_specs=[pl.BlockSpec((tm, W), row), pl.BlockSpec((tm, W), row),
                   pl.BlockSpec((C_HEADS * V_ROWS, tm), lambda i: (0, i))],
        out_shape=(jax.ShapeDtypeStruct((S, W), BF16), jax.ShapeDtypeStruct((S, W), BF16),
                   jax.ShapeDtypeStruct((C_HEADS * V_ROWS, S), BF16)),
        compiler_params=_cparams(("arbitrary",)),
        name="c_in_proj",
    )(x2d, g, wq, wkv, wkr, wkrs, gql, gkvl, wuq, wuqs, wuk, wuv, qg, qgs, kg, kgs,
      vone, ones_blk, cos_t, sin_t)


V_ROWS = C_V + V_PAD
C_HPS = 4


def _c_attn_kernel(q_ref, k_ref, vt_ref, o_ref, *scratch):
    i = pl.program_id(1)
    streams = tuple(
        (q_ref[:, h * LANES:(h + 1) * LANES], slice(h * LANES, (h + 1) * LANES),
         slice(h * V_ROWS, (h + 1) * V_ROWS), scratch[2 * h], scratch[2 * h + 1],
         scratch[2 * C_HPS + h])
        for h in range(C_HPS))
    accs = _attn_pipeline(i, k_ref, vt_ref, streams, V_ROWS)
    for pair in range(C_HPS // 2):
        a, b = accs[2 * pair], accs[2 * pair + 1]
        out = jnp.concatenate([a[:C_V] / a[C_V:C_V + 1], b[:C_V] / b[C_V:C_V + 1]], axis=0)
        o_ref[:, pair * LANES:(pair + 1) * LANES] = out.T.astype(BF16)


def _c_attn(q, k, vt):
    S = q.shape[0]
    once = pl.Buffered(1)
    return pl.pallas_call(
        _c_attn_kernel,
        grid=(C_HEADS // C_HPS, S // ATT_TQ),
        in_specs=[
            pl.BlockSpec((ATT_TQ, C_HPS * LANES), lambda h, i: (i, h)),
            pl.BlockSpec((S, C_HPS * LANES), lambda h, i: (0, h), pipeline_mode=once),
            pl.BlockSpec((C_HPS * V_ROWS, S), lambda h, i: (h, 0), pipeline_mode=once),
        ],
        out_specs=pl.BlockSpec((ATT_TQ, C_HPS * C_V), lambda h, i: (i, h)),
        out_shape=jax.ShapeDtypeStruct((S, C_HEADS * C_V), BF16),
        scratch_shapes=[_SCORE_BUF] * (2 * C_HPS) + [_QT_BUF] * C_HPS,
        compiler_params=_cparams(("arbitrary", "arbitrary")),
        name="c_mla_attn",
    )(q, k, vt)


def _c_out_router_kernel(o_ref, x_ref, wo_ref, g_ref, rhi_ref, rlo_ref, tri_ref,
                         x3_o, h_o, meta_o, cnt_o, run_sc):
    i = pl.program_id(0)

    @pl.when(i == 0)
    def _():
        run_sc[...] = jnp.zeros_like(run_sc)

    x3 = x_ref[...] + _dot(o_ref[...], wo_ref[...])
    x3_o[...] = x3
    h = _rms(x3, g_ref[...])
    for c in range(ROW_TILES):
        h_o[:, c, :] = h[:, c * LANES:(c + 1) * LANES]
    h_hi = h.astype(BF16)
    h_lo = (h - h_hi.astype(F32)).astype(BF16)
    logits = (_dot(h_hi, rhi_ref[...]) + _dot(h_hi, rlo_ref[...])) + _dot(h_lo, rhi_ref[...])
    tm = logits.shape[0]
    lane = lax.broadcasted_iota(jnp.int32, (tm, LANES), 1)
    lanef = lane.astype(F32)
    logits = jnp.where(lane < N_EXPERTS, logits, -jnp.inf)
    v1 = jnp.max(logits, axis=1, keepdims=True)
    i1 = jnp.min(jnp.where(logits == v1, lanef, float(LANES)), axis=1, keepdims=True)
    m1 = lanef == i1
    rest = jnp.where(m1, -jnp.inf, logits)
    v2 = jnp.max(rest, axis=1, keepdims=True)
    i2 = jnp.min(jnp.where(rest == v2, lanef, float(LANES)), axis=1, keepdims=True)
    m2 = lanef == i2
    e = jnp.exp(v2 - v1)
    g1 = 1.0 / (1.0 + e)
    g2 = e / (1.0 + e)
    chosen = jnp.where(m1 | m2, 1.0, 0.0)
    before = _dot(tri_ref[...], chosen.astype(BF16)) + run_sc[0:1, :]
    p1 = jnp.sum(jnp.where(m1, before, 0.0), axis=1, keepdims=True)
    p2 = jnp.sum(jnp.where(m2, before, 0.0), axis=1, keepdims=True)
    run_sc[...] = run_sc[...] + jnp.sum(chosen, axis=0, keepdims=True)
    meta = jnp.where(lane == 0, i1, 0.0)
    meta = jnp.where(lane == 1, i2, meta)
    meta = jnp.where(lane == 2, g1, meta)
    meta = jnp.where(lane == 3, g2, meta)
    meta = jnp.where(lane == 4, p1, meta)
    meta = jnp.where(lane == 5, p2, meta)
    meta_o[...] = meta
    cnt_o[...] = run_sc[...]


def _c_out_router(o, x2d, wo, g, rhi, rlo, tri, tm):
    S = x2d.shape[0]
    row = lambda i: (i, 0)
    const = lambda i: (0, 0)
    return pl.pallas_call(
        _c_out_router_kernel,
        grid=(S // tm,),
        in_specs=[pl.BlockSpec((tm, D_MODEL), row), pl.BlockSpec((tm, D_MODEL), row),
                  pl.BlockSpec((D_MODEL, D_MODEL), const), pl.BlockSpec((1, D_MODEL), const),
                  pl.BlockSpec((D_MODEL, LANES), const), pl.BlockSpec((D_MODEL, LANES), const),
                  pl.BlockSpec((tm, tm), const)],
        out_specs=[pl.BlockSpec((tm, D_MODEL), row),
                   pl.BlockSpec((tm, ROW_TILES, LANES), lambda i: (i, 0, 0)),
                   pl.BlockSpec((tm, LANES), row), pl.BlockSpec((8, LANES), const)],
        out_shape=(jax.ShapeDtypeStruct((S, D_MODEL), F32),
                   jax.ShapeDtypeStruct((S, ROW_TILES, LANES), F32),
                   jax.ShapeDtypeStruct((S, LANES), F32), jax.ShapeDtypeStruct((8, LANES), F32)),
        scratch_shapes=[pltpu.VMEM((8, LANES), F32)],
        compiler_params=_cparams(("arbitrary",)),
        name="c_out_router",
    )(o, x2d, wo, g, rhi, rlo, tri)


MOE_TM = 256
PAD_PIECES = (128, 64, 32, 16, 8, 4, 2, 1)


def _dispatch_kernel(pad_ref, slot_ref, h_ref, xs_ref, zero_sc, sem, zsem, *, ts):
    i = pl.program_id(0)

    @pl.when(i == 0)
    def _():
        zero_sc[...] = jnp.zeros_like(zero_sc)
        for e in range(N_EXPERTS):
            start = pad_ref[0, e]
            npad = pad_ref[1, e]
            for p in PAD_PIECES:
                hit = (npad & p) != 0

                @pl.when(hit)
                def _(start=start, p=p):
                    cp = pltpu.make_async_copy(zero_sc.at[pl.ds(0, p)],
                                               xs_ref.at[pl.ds(start, p)], zsem)
                    cp.start()
                    cp.wait()

                start = start + jnp.where(hit, p, 0)

        half = MOE_TM // 2

        def zero_tile(tile, c):
            for part in range(2):
                cp = pltpu.make_async_copy(
                    zero_sc, xs_ref.at[pl.ds(tile * MOE_TM + part * half, half)], zsem)
                cp.start()
                cp.wait()
            return c

        lax.fori_loop(pad_ref[2, 0], xs_ref.shape[0] // MOE_TM, zero_tile, 0)

    def row_copy(r, k):
        return pltpu.make_async_copy(h_ref.at[r], xs_ref.at[slot_ref[0, 0, 2 * r + k]], sem)

    def issue(r, c):
        row_copy(r, 0).start(priority=0)
        row_copy(r, 1).start(priority=1)
        return c

    lax.fori_loop(0, ts, issue, 0, unroll=8)
    for _ in range(2):
        pltpu.make_async_copy(h_ref, xs_ref.at[pl.ds(0, ts)], sem).wait()


def _dispatch(pad_info, slots3, h2, n_slots, ts):
    S = h2.shape[0]
    return pl.pallas_call(
        functools.partial(_dispatch_kernel, ts=ts),
        grid=(S // ts,),
        in_specs=[pl.BlockSpec(memory_space=pltpu.SMEM),
                  pl.BlockSpec((1, 1, 2 * ts), lambda i: (i, 0, 0), memory_space=pltpu.SMEM),
                  pl.BlockSpec((ts, ROW_TILES, LANES), lambda i: (i, 0, 0))],
        out_specs=pl.BlockSpec(memory_space=pl.ANY),
        out_shape=jax.ShapeDtypeStruct((n_slots, ROW_TILES, LANES), F32),
        scratch_shapes=[pltpu.VMEM((MOE_TM // 2, ROW_TILES, LANES), F32),
                        pltpu.SemaphoreType.DMA(()), pltpu.SemaphoreType.DMA(())],
        compiler_params=_cparams(("arbitrary",)),
        name="moe_dispatch",
    )(pad_info, slots3, h2)


MOE_FC = 512


def _moe_kernel(te_ref, tv_ref, xs_ref, wg_ref, wu_ref, wd_ref, y_ref, x_sc, act_sc):
    t = pl.program_id(0)

    @pl.when(tv_ref[t] == 0)
    def _():
        y_ref[...] = jnp.zeros_like(y_ref)

    @pl.when(tv_ref[t] != 0)
    def _():
        for c in range(ROW_TILES):
            x_sc[:, c * LANES:(c + 1) * LANES] = xs_ref[:, c, :].astype(BF16)
        x = x_sc[...]
        for c in range(D_FF_EXPERT // MOE_FC):
            sl = slice(c * MOE_FC, (c + 1) * MOE_FC)
            gate = _dot(x, wg_ref[:, sl])
            up = _dot(x, wu_ref[:, sl])
            act_sc[:, sl] = (gate * jax.nn.sigmoid(gate) * up).astype(BF16)
        acc = _dot(act_sc[...], wd_ref[...])
        for c in range(ROW_TILES):
            y_ref[:, c, :] = acc[:, c * LANES:(c + 1) * LANES]


def _moe(tile_expert, tile_valid, xs, w_gu, w_down):
    n_slots = xs.shape[0]
    nt = n_slots // MOE_TM
    grid_spec = pltpu.PrefetchScalarGridSpec(
        num_scalar_prefetch=2,
        grid=(nt,),
        in_specs=[
            pl.BlockSpec((MOE_TM, ROW_TILES, LANES), lambda t, te, tv: (t, 0, 0)),
            pl.BlockSpec((None, D_MODEL, D_FF_EXPERT), lambda t, te, tv: (te[t], 0, 0),
                         pipeline_mode=pl.Buffered(1)),
            pl.BlockSpec((None, D_MODEL, D_FF_EXPERT), lambda t, te, tv: (te[t], 0, 1),
                         pipeline_mode=pl.Buffered(1)),
            pl.BlockSpec((None, D_FF_EXPERT, D_MODEL), lambda t, te, tv: (te[t], 0, 0),
                         pipeline_mode=pl.Buffered(1)),
        ],
        out_specs=pl.BlockSpec((MOE_TM, ROW_TILES, LANES), lambda t, te, tv: (t, 0, 0)),
        scratch_shapes=[pltpu.VMEM((MOE_TM, D_MODEL), BF16),
                        pltpu.VMEM((MOE_TM, D_FF_EXPERT), BF16)],
    )
    return pl.pallas_call(
        _moe_kernel,
        grid_spec=grid_spec,
        out_shape=jax.ShapeDtypeStruct((n_slots, ROW_TILES, LANES), F32),
        compiler_params=_cparams(("arbitrary",)),
        name="moe_experts",
    )(tile_expert, tile_valid, xs, w_gu, w_gu, w_down)


def _combine_kernel(slot_ref, next_slot_ref, x3_ref, meta_ref, y_ref, o_ref, buf, sem, *, ts):
    i = pl.program_id(0)
    n = pl.num_programs(0)
    cur = lax.rem(i, 2)

    def gather(slots, half):
        def issue(r, c):
            for k in range(2):
                pltpu.make_async_copy(y_ref.at[slots[0, 0, 2 * r + k]], buf.at[half, k, r],
                                      sem.at[half]).start(priority=k)
            return c

        lax.fori_loop(0, ts, issue, 0, unroll=8)

    @pl.when(i == 0)
    def _():
        gather(slot_ref, 0)

    @pl.when(i + 1 < n)
    def _():
        gather(next_slot_ref, 1 - cur)

    for k in range(2):
        pltpu.make_async_copy(y_ref.at[pl.ds(0, ts)], buf.at[cur, k], sem.at[cur]).wait()
    meta = meta_ref[...]
    g1 = jnp.broadcast_to(meta[:, 2:3], (ts, LANES))
    g2 = jnp.broadcast_to(meta[:, 3:4], (ts, LANES))
    for c in range(ROW_TILES):
        sl = slice(c * LANES, (c + 1) * LANES)
        o_ref[:, sl] = x3_ref[:, sl] + g1 * buf[cur, 0, :, c, :] + g2 * buf[cur, 1, :, c, :]


def _combine(slots3, x3, meta, y, ts):
    S = x3.shape[0]
    n = S // ts
    return pl.pallas_call(
        functools.partial(_combine_kernel, ts=ts),
        grid=(n,),
        in_specs=[pl.BlockSpec((1, 1, 2 * ts), lambda i: (i, 0, 0), memory_space=pltpu.SMEM),
                  pl.BlockSpec((1, 1, 2 * ts), lambda i: (jnp.minimum(i + 1, n - 1), 0, 0),
                               memory_space=pltpu.SMEM),
                  pl.BlockSpec((ts, D_MODEL), lambda i: (i, 0)),
                  pl.BlockSpec((ts, LANES), lambda i: (i, 0)),
                  pl.BlockSpec(memory_space=pl.ANY)],
        out_specs=pl.BlockSpec((ts, D_MODEL), lambda i: (i, 0)),
        out_shape=jax.ShapeDtypeStruct((S, D_MODEL), F32),
        scratch_shapes=[pltpu.VMEM((2, 2, ts, ROW_TILES, LANES), F32),
                        pltpu.SemaphoreType.DMA((2,))],
        compiler_params=_cparams(("arbitrary",)),
        name="moe_combine",
    )(slots3, slots3, x3, meta, y)


def _tile_lanes(v, reps):
    return jnp.tile(v, reps)[None, :].astype(F32)


def _rope_cos_sin(S, dim):
    pos = np.arange(S, dtype=np.float64)
    inv = ROPE_THETA ** (-np.arange(0, dim, 2, dtype=np.float64) / dim)
    ang = pos[:, None] * inv[None, :]
    return np.cos(ang), np.sin(ang)


def _rope_tables_64(S):
    c, s = _rope_cos_sin(S, A_HEAD_DIM)
    return (jnp.asarray(np.concatenate([c, c, c, c], axis=1), F32),
            jnp.asarray(np.concatenate([-s, s, -s, s], axis=1), F32))


def _rope_tables_32(S):
    c, s = _rope_cos_sin(S, C_ROPE)
    one = np.ones((S, C_NOPE))
    zero = np.zeros((S, C_NOPE))
    pad = np.zeros((S, LANES - C_QK))
    return (jnp.asarray(np.concatenate([one, c, c, pad], axis=1), F32),
            jnp.asarray(np.concatenate([zero, -s, s, pad], axis=1), F32))


def _pad_heads(w, n_heads, width):
    k = w.shape[0]
    w = w.reshape(k, n_heads, width)
    return jnp.pad(w, ((0, 0), (0, 0), (0, LANES - width))).reshape(k, n_heads * LANES)


def _swap_rope_cols(w_p):
    k = w_p.shape[0]
    w = w_p.reshape(k, -1, LANES)
    half = C_ROPE // 2
    sw = jnp.concatenate([jnp.zeros_like(w[:, :, :C_NOPE]),
                          w[:, :, C_NOPE + half:C_QK], w[:, :, C_NOPE:C_NOPE + half],
                          jnp.zeros_like(w[:, :, C_QK:])], axis=2)
    return sw.reshape(k, -1)


def kernel(x, l0_norm_attn, l0_w_in, l0_a_q_norm, l0_a_k_norm, l0_a_lambda, l0_a_subln, l0_b_q_norm, l0_b_k_norm, l0_b_sinks, l0_w_out, l0_norm_ffn, l0_ffn_w_gu, l0_ffn_w_down, l1_norm_attn, l1_c_w_in, l1_c_q_lora_norm, l1_c_kv_lora_norm, l1_c_w_uq, l1_c_w_ukv, l1_c_q_norm, l1_c_k_norm, l1_c_w_out, l1_norm_ffn, l1_router, l1_exp_w_gu, l1_exp_w_down):
    B, S, _ = x.shape
    assert B == 1
    x2d = x.reshape(S, D_MODEL)
    tm = min(512, S)

    o_bk = 3 * A_W + B_QW
    o_bv = o_bk + B_KW
    bk_w = l0_w_in[:, o_bk:o_bv].reshape(D_MODEL, B_KV_HEADS, B_HEAD_DIM)
    bv_w = l0_w_in[:, o_bv:].reshape(D_MODEL, B_KV_HEADS, B_HEAD_DIM)
    dup = lambda w: jnp.concatenate([w, w], axis=2).reshape(D_MODEL, 2 * B_KW)
    w_in_p = jnp.concatenate([l0_w_in[:, :2 * A_W], l0_w_in[:, 3 * A_W:o_bk], dup(bk_w),
                              dup(bv_w)], axis=1).astype(BF16)
    w_avt = l0_w_in[:, 2 * A_W:3 * A_W].T.reshape(A_HEADS, 2 * A_HEAD_DIM, D_MODEL)
    w_avt = jnp.pad(w_avt, ((0, 0), (0, V_PAD), (0, 0))).reshape(-1, D_MODEL).astype(BF16)
    a_vone = jnp.zeros((A_HEADS, A_V_ROWS, 1), F32).at[:, 2 * A_HEAD_DIM, 0].set(1.0).reshape(-1, 1)
    gains = jnp.concatenate([_tile_lanes(l0_a_q_norm, 2), _tile_lanes(l0_a_k_norm, 2),
                             _tile_lanes(l0_b_q_norm, 2), _tile_lanes(l0_b_k_norm, 2),
                             jnp.zeros((4, LANES), F32)], axis=0)
    lane = jnp.arange(LANES)
    ones64 = (lane[:, None] // 64 == lane[None, :] // 64).astype(BF16)
    cos64, sin64 = _rope_tables_64(S)
    aq, ak, avt, bq, bk, bv = _l0_in(x2d, l0_norm_attn[None, :], w_in_p, w_avt, a_vone, gains,
                                     ones64, cos64, sin64, tm)
    lam_init = 0.8 - 0.6 * math.exp(-0.3 * 0)
    a_out = _a_attn(l0_a_lambda.astype(F32), aq, ak, avt, l0_a_subln[None, :].astype(F32),
                    lam_init)
    b_out = _b_attn(l0_b_sinks.astype(F32), bq, bk, bv)
    w_out = l0_w_out.astype(BF16)
    x2 = _l0_ffn(a_out, b_out, x2d, w_out[:A_W], w_out[A_W:], l0_norm_ffn[None, :],
                 l0_ffn_w_gu.astype(BF16), l0_ffn_w_down.astype(BF16), tm, D_FF // 2)

    wq = l1_c_w_in[:, :C_Q_RANK].astype(BF16)
    wkv = l1_c_w_in[:, C_Q_RANK:C_Q_RANK + C_KV_RANK].astype(BF16)
    wkr = jnp.pad(l1_c_w_in[:, C_Q_RANK + C_KV_RANK:], ((0, 0), (C_NOPE, LANES - C_QK)))
    wkrs = _swap_rope_cols(wkr).astype(BF16)
    wkr = wkr.astype(BF16)
    wuq = _pad_heads(l1_c_w_uq, C_HEADS, C_QK)
    wuqs = _swap_rope_cols(wuq).astype(BF16)
    wuq = wuq.astype(BF16)
    ukv = l1_c_w_ukv.reshape(C_KV_RANK, C_HEADS, C_NOPE + C_V)
    wuk = _pad_heads(ukv[:, :, :C_NOPE].reshape(C_KV_RANK, -1), C_HEADS, C_NOPE).astype(BF16)
    wuv = jnp.pad(jnp.transpose(ukv[:, :, C_NOPE:], (1, 2, 0)), ((0, 0), (0, V_PAD), (0, 0)))
    wuv = wuv.reshape(C_HEADS * V_ROWS, C_KV_RANK).astype(BF16)
    vone = jnp.zeros((C_HEADS, V_ROWS, 1), F32).at[:, C_V, 0].set(1.0).reshape(-1, 1)
    pad_gain = lambda gvec: jnp.pad(gvec.astype(F32), (0, LANES - C_QK))[None, :]
    qg, kg = pad_gain(l1_c_q_norm), pad_gain(l1_c_k_norm)
    qgs, kgs = _swap_rope_cols(qg), _swap_rope_cols(kg)
    ones128 = jnp.ones((LANES, LANES), BF16)
    cos32, sin32 = _rope_tables_32(S)
    cq, ck, cvt = _c_in(x2, l1_norm_attn[None, :], wq, wkv, wkr, wkrs,
                       l1_c_q_lora_norm[None, :], l1_c_kv_lora_norm[None, :],
                       wuq, wuqs, wuk, wuv, qg, qgs, kg, kgs, vone, ones128, cos32, sin32, tm)
    c_o = _c_attn(cq, ck, cvt)

    r_pad = jnp.pad(l1_router.astype(F32), ((0, 0), (0, LANES - N_EXPERTS)))
    r_hi = r_pad.astype(BF16)
    r_lo = (r_pad - r_hi.astype(F32)).astype(BF16)
    tr = min(256, S)
    ridx = jnp.arange(tr)
    tri = (ridx[None, :] < ridx[:, None]).astype(BF16)
    x3, h2, meta, cnt = _c_out_router(c_o, x2, l1_c_w_out.astype(BF16), l1_norm_ffn[None, :],
                                      r_hi, r_lo, tri, tr)

    counts = cnt[0, :N_EXPERTS].astype(jnp.int32)
    padded = ((counts + MOE_TM - 1) // MOE_TM) * MOE_TM
    ends = jnp.cumsum(padded)
    offs = ends - padded
    idx = meta[:, 0:2].astype(jnp.int32)
    pos = meta[:, 4:6].astype(jnp.int32)
    slots = offs[idx] + pos
    n_slots = 2 * S + N_EXPERTS * MOE_TM
    nt = n_slots // MOE_TM
    tile_start = jnp.arange(nt, dtype=jnp.int32) * MOE_TM
    n_valid = jnp.broadcast_to(ends[-1] // MOE_TM, (N_EXPERTS,))
    tile_valid = (tile_start < ends[-1]).astype(jnp.int32)
    tile_expert = jnp.minimum(
        jnp.sum((tile_start[:, None] >= ends[None, :]).astype(jnp.int32), axis=1),
        N_EXPERTS - 1).astype(jnp.int32)
    pad_info = jnp.stack([offs + counts, padded - counts, n_valid]).astype(jnp.int32)

    ts = min(512, S)
    slots3 = slots.reshape(S // ts, 1, 2 * ts)
    xs = _dispatch(pad_info, slots3, h2, n_slots, ts)
    y = _moe(tile_expert, tile_valid, xs,
             l1_exp_w_gu.astype(BF16), l1_exp_w_down.astype(BF16))
    out = _combine(slots3, x3, meta, y, ts)
    return out.reshape(B, S, D_MODEL)
```

```python
import functools
import math

import jax
import jax.numpy as jnp
import numpy as np
from jax import lax
from jax.experimental import pallas as pl
from jax.experimental.pallas import tpu as pltpu

F32 = jnp.float32
BF16 = jnp.bfloat16

D_MODEL = 1024
CHUNK = 64
ROPE_THETA = 10000.0
EPS = 1e-6
LANES = 128
ROW_TILES = D_MODEL // LANES

A_HEADS = 4
A_HEAD_DIM = 64
B_HEADS = 8
B_KV_HEADS = 2
B_HEAD_DIM = 64
A_W = A_HEADS * 2 * A_HEAD_DIM
B_QW = B_HEADS * B_HEAD_DIM
B_KW = B_KV_HEADS * B_HEAD_DIM

C_HEADS = 16
C_Q_RANK = 256
C_KV_RANK = 128
C_NOPE = 64
C_ROPE = 32
C_V = 64
C_QK = C_NOPE + C_ROPE

D_FF = 2816
N_EXPERTS = 8
D_FF_EXPERT = 3584

NEG_BIG = -1e30
LOG2E = 1.0 / math.log(2.0)
VMEM_LIMIT = 56 * 1024 * 1024


def _cparams(sem):
    return pltpu.CompilerParams(dimension_semantics=sem, vmem_limit_bytes=VMEM_LIMIT)


def _dot(a, b):
    return jnp.dot(a, b, preferred_element_type=F32)


def _dot_nt(a, b):
    return lax.dot_general(a, b, (((1,), (1,)), ((), ())), preferred_element_type=F32)


def _rms(x, g):
    ms = jnp.mean(x * x, axis=-1, keepdims=True)
    return x * lax.rsqrt(ms + EPS) * g


def _l0_in_kernel(x_ref, g_ref, w_ref, wvt_ref, vone_ref, gains_ref, ones_ref, cos_ref, sin_ref,
                  aq_o, ak_o, avt_o, bq_o, bk_o, bv_o):
    h = _rms(x_ref[...], g_ref[...]).astype(BF16)
    z = _dot(h, w_ref[...])
    avt_o[...] = (_dot_nt(wvt_ref[...], h) + vone_ref[...]).astype(BF16)
    cos = cos_ref[...]
    sin = sin_ref[...]
    ones_blk = ones_ref[...]
    lane = lax.broadcasted_iota(jnp.int32, (1, LANES), 1)
    first_half = (lane % 64) < 32

    def norm_rope(zs, gain, scale):
        ss = _dot((zs * zs).astype(BF16), ones_blk)
        r = lax.rsqrt(ss * (1.0 / 64.0) + EPS) * scale
        y = zs * gain
        sw = jnp.where(first_half, pltpu.roll(y, 96, 1), pltpu.roll(y, 32, 1))
        return ((y * cos + sw * sin) * r).astype(BF16)

    col = 0
    gcol = 0
    scale = A_HEAD_DIM ** -0.5
    scales = {"aq": scale * LOG2E, "ak": 1.0, "bq": scale, "bk": 1.0}
    outs = {"aq": aq_o, "ak": ak_o, "bq": bq_o, "bk": bk_o}
    for name, nslab in (("aq", 4), ("ak", 4), ("bq", 4), ("bk", 2), ("bv", 2)):
        if name == "bv":
            bv_o[...] = z[:, col:col + 2 * LANES].astype(BF16)
            col += 2 * LANES
            continue
        sc = scales[name]
        gain = gains_ref[gcol:gcol + 1, :]
        gcol += 1
        for s in range(nslab):
            zs = z[:, col:col + LANES]
            outs[name][:, s * LANES:(s + 1) * LANES] = norm_rope(zs, gain, sc)
            col += LANES


def _l0_in(x2d, g, w_in_p, w_avt, vone, gains, ones_blk, cos_t, sin_t, tm):
    S = x2d.shape[0]
    nw = w_in_p.shape[1]
    row = lambda i: (i, 0)
    const = lambda i: (0, 0)
    out_shapes = (
        jax.ShapeDtypeStruct((S, A_W), BF16), jax.ShapeDtypeStruct((S, A_W), BF16),
        jax.ShapeDtypeStruct((A_HEADS * A_V_ROWS, S), BF16), jax.ShapeDtypeStruct((S, B_QW), BF16),
        jax.ShapeDtypeStruct((S, 2 * B_KW), BF16), jax.ShapeDtypeStruct((S, 2 * B_KW), BF16))
    return pl.pallas_call(
        _l0_in_kernel,
        grid=(S // tm,),
        in_specs=[
            pl.BlockSpec((tm, D_MODEL), row),
            pl.BlockSpec((1, D_MODEL), const),
            pl.BlockSpec((D_MODEL, nw), const),
            pl.BlockSpec((A_HEADS * A_V_ROWS, D_MODEL), const),
            pl.BlockSpec((A_HEADS * A_V_ROWS, 1), const),
            pl.BlockSpec((8, LANES), const),
            pl.BlockSpec((LANES, LANES), const),
            pl.BlockSpec((tm, LANES), row),
            pl.BlockSpec((tm, LANES), row),
        ],
        out_specs=[
            pl.BlockSpec((tm, A_W), row), pl.BlockSpec((tm, A_W), row),
            pl.BlockSpec((A_HEADS * A_V_ROWS, tm), lambda i: (0, i)), pl.BlockSpec((tm, B_QW), row),
            pl.BlockSpec((tm, 2 * B_KW), row), pl.BlockSpec((tm, 2 * B_KW), row)],
        out_shape=out_shapes,
        compiler_params=_cparams(("arbitrary",)),
        name="l0_in_proj",
    )(x2d, g, w_in_p, w_avt, vone, gains, ones_blk, cos_t, sin_t)


ATT_TQ = 512
ATT_TK = 256
V_PAD = 16


def _softmax_pv(s_ref, s_max, vt, m, acc):
    m_new = jnp.maximum(m, s_max)
    alpha = jnp.exp2(m - m_new)
    p = jnp.exp2(s_ref[...] - m_new).astype(BF16)
    return m_new, alpha * acc + _dot(vt, p)


def _diag_masks():
    key_chunk = lax.broadcasted_iota(jnp.int32, (ATT_TK, ATT_TQ), 0) // CHUNK
    query_chunk = lax.broadcasted_iota(jnp.int32, (ATT_TK, ATT_TQ), 1) // CHUNK
    return [key_chunk + b * (ATT_TK // CHUNK) <= query_chunk for b in range(ATT_TQ // ATT_TK)]


def _attn_pipeline(i, k_ref, vt_ref, streams, v_rows):
    n = len(streams)

    for st in streams:
        st[5][...] = st[0].T

    def scores(st, blk, mask, dst):
        lanes, qt_ref = streams[st][1], streams[st][5]
        off = pl.multiple_of(blk * ATT_TK, ATT_TK)
        s = _dot(k_ref[pl.ds(off, ATT_TK), lanes], qt_ref[...])
        if mask is not None:
            s = jnp.where(mask, s, NEG_BIG)
        dst[...] = s
        return jnp.max(s, axis=0, keepdims=True)

    def consume(st, blk, src, s_max, state):
        off = pl.multiple_of(blk * ATT_TK, ATT_TK)
        return _softmax_pv(src, s_max, vt_ref[streams[st][2], pl.ds(off, ATT_TK)], *state)

    buf_a = [st[3] for st in streams]
    buf_b = [st[4] for st in streams]
    mask0, mask1 = _diag_masks()
    d0 = 2 * i
    d1 = d0 + 1
    state = [(jnp.full((1, ATT_TQ), NEG_BIG, F32), jnp.zeros((v_rows, ATT_TQ), F32))] * n
    max_a = [scores(st, d0, mask0, buf_a[st]) for st in range(n)]
    max_b = [None] * n
    for st in range(n):
        max_b[st] = scores(st, d1, mask1, buf_b[st])
        state[st] = consume(st, d0, buf_a[st], max_a[st], state[st])
    for st in range(n):
        max_a[st] = scores(st, 0, None, buf_a[st])
        state[st] = consume(st, d1, buf_b[st], max_b[st], state[st])

    def body(p, carry):
        carry, max_a = list(carry[0]), list(carry[1])
        max_b = [None] * n
        u0 = 2 * p
        u1 = u0 + 1
        nxt = jnp.minimum(u0 + 2, d0 - 1)
        for st in range(n):
            max_b[st] = scores(st, u1, None, buf_b[st])
            carry[st] = consume(st, u0, buf_a[st], max_a[st], carry[st])
        for st in range(n):
            max_a[st] = scores(st, nxt, None, buf_a[st])
            carry[st] = consume(st, u1, buf_b[st], max_b[st], carry[st])
        return tuple(carry), tuple(max_a)

    final, _ = lax.fori_loop(0, i, body, (tuple(state), tuple(max_a)))
    return [acc for _, acc in final]


_SCORE_BUF = pltpu.VMEM((ATT_TK, ATT_TQ), F32)
_QT_BUF = pltpu.VMEM((LANES, ATT_TQ), BF16)
_ATTN_SCRATCH = [_SCORE_BUF] * 4 + [_QT_BUF] * 2


A_V_ROWS = 2 * A_HEAD_DIM + V_PAD


A_HPS = 2


def _a_attn_kernel(lam_ref, q_ref, k_ref, vt_ref, subln_ref, o_ref, *scratch, lam_init):
    i = pl.program_id(1)
    lane = lax.broadcasted_iota(jnp.int32, (1, LANES), 1)
    streams = []
    for h in range(A_HPS):
        q = q_ref[:, h * LANES:(h + 1) * LANES]
        zero = jnp.zeros_like(q)
        lanes = slice(h * LANES, (h + 1) * LANES)
        rows = slice(h * A_V_ROWS, (h + 1) * A_V_ROWS)
        for mp, sel in enumerate((lane < 64, lane >= 64)):
            st = 2 * h + mp
            streams.append((jnp.where(sel, q, zero), lanes, rows, scratch[2 * st],
                            scratch[2 * st + 1], scratch[4 * A_HPS + st]))
    accs = _attn_pipeline(i, k_ref, vt_ref, tuple(streams), A_V_ROWS)
    dv = 2 * A_HEAD_DIM
    lf = lam_ref[...]
    lam = (jnp.exp(jnp.sum(lf[0:1] * lf[1:2], axis=1, keepdims=True))
           - jnp.exp(jnp.sum(lf[2:3] * lf[3:4], axis=1, keepdims=True)) + lam_init)
    for h in range(A_HPS):
        acc1, acc2 = accs[2 * h], accs[2 * h + 1]
        out = (acc1[:dv] / acc1[dv:dv + 1] - lam * (acc2[:dv] / acc2[dv:dv + 1])).T
        out = _rms(out, subln_ref[...]) * (1.0 - lam_init)
        o_ref[:, h * LANES:(h + 1) * LANES] = out.astype(BF16)


def _a_attn(lam_p, aq, ak, avt, subln, lam_init):
    S = aq.shape[0]
    once = pl.Buffered(1)
    return pl.pallas_call(
        functools.partial(_a_attn_kernel, lam_init=lam_init),
        grid=(A_HEADS // A_HPS, S // ATT_TQ),
        in_specs=[
            pl.BlockSpec((4, A_HEAD_DIM), lambda h, i: (0, 0)),
            pl.BlockSpec((ATT_TQ, A_HPS * LANES), lambda h, i: (i, h)),
            pl.BlockSpec((S, A_HPS * LANES), lambda h, i: (0, h), pipeline_mode=once),
            pl.BlockSpec((A_HPS * A_V_ROWS, S), lambda h, i: (h, 0), pipeline_mode=once),
            pl.BlockSpec((1, LANES), lambda h, i: (0, 0)),
        ],
        out_specs=pl.BlockSpec((ATT_TQ, A_HPS * LANES), lambda h, i: (i, h)),
        out_shape=jax.ShapeDtypeStruct((S, A_W), BF16),
        scratch_shapes=[_SCORE_BUF] * (4 * A_HPS) + [_QT_BUF] * (2 * A_HPS),
        compiler_params=_cparams(("arbitrary", "arbitrary")),
        name="a_diff_attn",
    )(lam_p, aq, ak, avt, subln)


B_BLK = 256
B_BACK = 128


def _b_attn_kernel(sink_ref, q_ref, kp_ref, kc_ref, vp_ref, vc_ref, o_ref):
    i = pl.program_id(0)
    lane = lax.broadcasted_iota(jnp.int32, (1, LANES), 1)
    lo = lane < 64
    r = lax.broadcasted_iota(jnp.int32, (B_BLK, B_BACK + B_BLK), 0) // CHUNK
    c_idx = lax.broadcasted_iota(jnp.int32, (B_BLK, B_BACK + B_BLK), 1)
    c = c_idx // CHUNK
    mask = (c >= r) & (c <= r + 2) & ((c_idx >= B_BACK) | (i > 0))
    for slab in range(B_HEADS // 2):
        g = slab // 2
        k = jnp.concatenate([kp_ref[:, g * LANES:(g + 1) * LANES],
                             kc_ref[:, g * LANES:(g + 1) * LANES]], axis=0)
        v = jnp.concatenate([vp_ref[:, g * LANES:(g + 1) * LANES],
                             vc_ref[:, g * LANES:(g + 1) * LANES]], axis=0)
        qs = q_ref[:, slab * LANES:(slab + 1) * LANES]
        zq = jnp.zeros_like(qs)
        zv = jnp.zeros_like(v)
        out = jnp.zeros((B_BLK, LANES), F32)
        for half in range(2):
            sel = lo if half == 0 else jnp.logical_not(lo)
            sink = sink_ref[2 * slab + half]
            s = _dot_nt(jnp.where(sel, qs, zq), k)
            s = jnp.where(mask, s, NEG_BIG)
            m = jnp.maximum(jnp.max(s, axis=1, keepdims=True), sink)
            e = jnp.exp(s - m)
            denom = jnp.sum(e, axis=1, keepdims=True) + jnp.exp(sink - m)
            p = (e / denom).astype(BF16)
            out = out + _dot(p, jnp.where(sel, v, zv))
        o_ref[:, slab * LANES:(slab + 1) * LANES] = out.astype(BF16)


def _b_attn(sinks, bq, bk, bv):
    S = bq.shape[0]
    prev = lambda i: (jnp.maximum(i * (B_BLK // B_BACK) - 1, 0), 0)
    cur = lambda i: (i, 0)
    return pl.pallas_call(
        _b_attn_kernel,
        grid=(S // B_BLK,),
        in_specs=[
            pl.BlockSpec(memory_space=pltpu.SMEM),
            pl.BlockSpec((B_BLK, B_QW), cur),
            pl.BlockSpec((B_BACK, 2 * B_KW), prev),
            pl.BlockSpec((B_BLK, 2 * B_KW), cur),
            pl.BlockSpec((B_BACK, 2 * B_KW), prev),
            pl.BlockSpec((B_BLK, 2 * B_KW), cur),
        ],
        out_specs=pl.BlockSpec((B_BLK, B_QW), cur),
        out_shape=jax.ShapeDtypeStruct((S, B_QW), BF16),
        compiler_params=_cparams(("arbitrary",)),
        name="b_swa_attn",
    )(sinks, bq, bk, bk, bv, bv)


def _l0_ffn_kernel(a_ref, b_ref, x_ref, woa_ref, wob_ref, g_ref, wg_ref, wu_ref, wd_ref,
                   o_ref, h_sc, acc_sc):
    j = pl.program_id(1)

    @pl.when(j == 0)
    def _():
        x1 = x_ref[...] + _dot(a_ref[...], woa_ref[...]) + _dot(b_ref[...], wob_ref[...])
        acc_sc[...] = x1
        h_sc[...] = _rms(x1, g_ref[...]).astype(BF16)

    h = h_sc[...]
    gate = _dot(h, wg_ref[...])
    up = _dot(h, wu_ref[...])
    act = (gate * jax.nn.sigmoid(gate) * up).astype(BF16)
    acc_sc[...] += _dot(act, wd_ref[...])

    @pl.when(j == pl.num_programs(1) - 1)
    def _():
        o_ref[...] = acc_sc[...]


def _l0_ffn(a_out, b_out, x2d, wo_a, wo_b, g, w_gu, w_down, tm, tf):
    S = x2d.shape[0]
    nf = D_FF // tf
    row = lambda i, j: (i, 0)
    const = lambda i, j: (0, 0)
    return pl.pallas_call(
        _l0_ffn_kernel,
        grid=(S // tm, nf),
        in_specs=[
            pl.BlockSpec((tm, A_W), row),
            pl.BlockSpec((tm, B_QW), row),
            pl.BlockSpec((tm, D_MODEL), row),
            pl.BlockSpec((A_W, D_MODEL), const),
            pl.BlockSpec((B_QW, D_MODEL), const),
            pl.BlockSpec((1, D_MODEL), const),
            pl.BlockSpec((D_MODEL, tf), lambda i, j: (0, j)),
            pl.BlockSpec((D_MODEL, tf), lambda i, j: (0, nf + j)),
            pl.BlockSpec((tf, D_MODEL), lambda i, j: (j, 0)),
        ],
        out_specs=pl.BlockSpec((tm, D_MODEL), row),
        out_shape=jax.ShapeDtypeStruct((S, D_MODEL), F32),
        scratch_shapes=[pltpu.VMEM((tm, D_MODEL), BF16), pltpu.VMEM((tm, D_MODEL), F32)],
        compiler_params=_cparams(("arbitrary", "arbitrary")),
        name="l0_out_ffn",
    )(a_out, b_out, x2d, wo_a, wo_b, g, w_gu, w_gu, w_down)


def _c_in_kernel(x_ref, g_ref, wq_ref, wkv_ref, wkr_ref, wkrs_ref, gql_ref, gkvl_ref,
                 wuq_ref, wuqs_ref, wuk_ref, wuv_ref, qg_ref, qgs_ref, kg_ref, kgs_ref,
                 vone_ref, ones_ref, ones2_ref, cos_ref, sin_ref, q_o, k_o, vt_o):
    h = _rms(x_ref[...], g_ref[...]).astype(BF16)
    cq = _rms(_dot(h, wq_ref[...]), gql_ref[...]).astype(BF16)
    ckv = _rms(_dot(h, wkv_ref[...]), gkvl_ref[...]).astype(BF16)
    kr = _dot(h, wkr_ref[...])
    krs = _dot(h, wkrs_ref[...])
    q = _dot(cq, wuq_ref[...])
    qs = _dot(cq, wuqs_ref[...])
    kn = _dot(ckv, wuk_ref[...])
    vt_o[...] = (_dot_nt(wuv_ref[...], ckv) + vone_ref[...]).astype(BF16)

    cos = cos_ref[...]
    sin = sin_ref[...]
    ones_blk = ones_ref[...]
    qg, qgs, kg, kgs = qg_ref[...], qgs_ref[...], kg_ref[...], kgs_ref[...]
    ss_kr = _dot((kr * kr).astype(BF16), ones_blk)
    kr_roped = kr * kg * cos + krs * kgs * sin
    scale = C_QK ** -0.5 * LOG2E
    inv = 1.0 / C_QK
    ones_pair = ones2_ref[...]
    for pair in range(C_HEADS // 2):
        both = slice(2 * pair * LANES, (2 * pair + 2) * LANES)
        ss_q = _dot((q[:, both] * q[:, both]).astype(BF16), ones_pair)
        ss_k = _dot((kn[:, both] * kn[:, both]).astype(BF16), ones_pair)
        for half in range(2):
            sl = slice((2 * pair + half) * LANES, (2 * pair + half + 1) * LANES)
            hs = slice(half * LANES, (half + 1) * LANES)
            qh = q[:, sl]
            r = lax.rsqrt(ss_q[:, hs] * inv + EPS) * scale
            q_o[:, sl] = ((qh * qg * cos + qs[:, sl] * qgs * sin) * r).astype(BF16)
            rk = lax.rsqrt((ss_k[:, hs] + ss_kr) * inv + EPS)
            k_o[:, sl] = ((kn[:, sl] * kg + kr_roped) * rk).astype(BF16)


def _c_in(x2d, g, wq, wkv, wkr, wkrs, gql, gkvl, wuq, wuqs, wuk, wuv, qg, qgs, kg, kgs,
          vone, ones_blk, ones_pair, cos_t, sin_t, tm):
    S = x2d.shape[0]
    row = lambda i: (i, 0)
    const = lambda i: (0, 0)
    full = lambda a: pl.BlockSpec(a.shape, const)
    W = C_HEADS * LANES
    return pl.pallas_call(
        _c_in_kernel,
        grid=(S // tm,),
        in_specs=[pl.BlockSpec((tm, D_MODEL), row), full(g), full(wq), full(wkv), full(wkr),
                  full(wkrs), full(gql), full(gkvl), full(wuq), full(wuqs), full(wuk),
                  full(wuv), full(qg), full(qgs), full(kg), full(kgs), full(vone),
                  full(ones_blk), full(ones_pair),
                  pl.BlockSpec((tm, LANES), row), pl.BlockSpec((tm, LANES), row)],
        out_specs=[pl.BlockSpec((tm, W), row), pl.BlockSpec((tm, W), row),
                   pl.BlockSpec((C_HEADS * V_ROWS, tm), lambda i: (0, i))],
        out_shape=(jax.ShapeDtypeStruct((S, W), BF16), jax.ShapeDtypeStruct((S, W), BF16),
                   jax.ShapeDtypeStruct((C_HEADS * V_ROWS, S), BF16)),
        compiler_params=_cparams(("arbitrary",)),
        name="c_in_proj",
    )(x2d, g, wq, wkv, wkr, wkrs, gql, gkvl, wuq, wuqs, wuk, wuv, qg, qgs, kg, kgs,
      vone, ones_blk, ones_pair, cos_t, sin_t)


V_ROWS = C_V + V_PAD
C_HPS = 4


def _c_attn_kernel(q_ref, k_ref, vt_ref, o_ref, *scratch):
    i = pl.program_id(1)
    streams = tuple(
        (q_ref[:, h * LANES:(h + 1) * LANES], slice(h * LANES, (h + 1) * LANES),
         slice(h * V_ROWS, (h + 1) * V_ROWS), scratch[2 * h], scratch[2 * h + 1],
         scratch[2 * C_HPS + h])
        for h in range(C_HPS))
    accs = _attn_pipeline(i, k_ref, vt_ref, streams, V_ROWS)
    for pair in range(C_HPS // 2):
        a, b = accs[2 * pair], accs[2 * pair + 1]
        out = jnp.concatenate([a[:C_V] / a[C_V:C_V + 1], b[:C_V] / b[C_V:C_V + 1]], axis=0)
        o_ref[:, pair * LANES:(pair + 1) * LANES] = out.T.astype(BF16)


def _c_attn(q, k, vt):
    S = q.shape[0]
    once = pl.Buffered(1)
    return pl.pallas_call(
        _c_attn_kernel,
        grid=(C_HEADS // C_HPS, S // ATT_TQ),
        in_specs=[
            pl.BlockSpec((ATT_TQ, C_HPS * LANES), lambda h, i: (i, h)),
            pl.BlockSpec((S, C_HPS * LANES), lambda h, i: (0, h), pipeline_mode=once),
            pl.BlockSpec((C_HPS * V_ROWS, S), lambda h, i: (h, 0), pipeline_mode=once),
        ],
        out_specs=pl.BlockSpec((ATT_TQ, C_HPS * C_V), lambda h, i: (i, h)),
        out_shape=jax.ShapeDtypeStruct((S, C_HEADS * C_V), BF16),
        scratch_shapes=[_SCORE_BUF] * (2 * C_HPS) + [_QT_BUF] * C_HPS,
        compiler_params=_cparams(("arbitrary", "arbitrary")),
        name="c_mla_attn",
    )(q, k, vt)


def _c_out_router_kernel(o_ref, x_ref, wo_ref, g_ref, rhi_ref, rlo_ref, tri_ref,
                         x3_o, h_o, meta_o, cnt_o, run_sc):
    i = pl.program_id(0)

    @pl.when(i == 0)
    def _():
        run_sc[...] = jnp.zeros_like(run_sc)

    x3 = x_ref[...] + _dot(o_ref[...], wo_ref[...])
    x3_o[...] = x3
    h = _rms(x3, g_ref[...])
    for c in range(ROW_TILES):
        h_o[:, c, :] = h[:, c * LANES:(c + 1) * LANES]
    h_hi = h.astype(BF16)
    h_lo = (h - h_hi.astype(F32)).astype(BF16)
    logits = (_dot(h_hi, rhi_ref[...]) + _dot(h_hi, rlo_ref[...])) + _dot(h_lo, rhi_ref[...])
    tm = logits.shape[0]
    lane = lax.broadcasted_iota(jnp.int32, (tm, LANES), 1)
    lanef = lane.astype(F32)
    logits = jnp.where(lane < N_EXPERTS, logits, -jnp.inf)
    v1 = jnp.max(logits, axis=1, keepdims=True)
    i1 = jnp.min(jnp.where(logits == v1, lanef, float(LANES)), axis=1, keepdims=True)
    m1 = lanef == i1
    rest = jnp.where(m1, -jnp.inf, logits)
    v2 = jnp.max(rest, axis=1, keepdims=True)
    i2 = jnp.min(jnp.where(rest == v2, lanef, float(LANES)), axis=1, keepdims=True)
    m2 = lanef == i2
    e = jnp.exp(v2 - v1)
    g1 = 1.0 / (1.0 + e)
    g2 = e / (1.0 + e)
    chosen = jnp.where(m1 | m2, 1.0, 0.0)
    before = _dot(tri_ref[...], chosen.astype(BF16)) + run_sc[0:1, :]
    p1 = jnp.sum(jnp.where(m1, before, 0.0), axis=1, keepdims=True)
    p2 = jnp.sum(jnp.where(m2, before, 0.0), axis=1, keepdims=True)
    run_sc[...] = run_sc[...] + jnp.sum(chosen, axis=0, keepdims=True)
    meta = jnp.where(lane == 0, i1, 0.0)
    meta = jnp.where(lane == 1, i2, meta)
    meta = jnp.where(lane == 2, g1, meta)
    meta = jnp.where(lane == 3, g2, meta)
    meta = jnp.where(lane == 4, p1, meta)
    meta = jnp.where(lane == 5, p2, meta)
    meta_o[...] = meta
    cnt_o[...] = run_sc[...]


def _c_out_router(o, x2d, wo, g, rhi, rlo, tri, tm):
    S = x2d.shape[0]
    row = lambda i: (i, 0)
    const = lambda i: (0, 0)
    return pl.pallas_call(
        _c_out_router_kernel,
        grid=(S // tm,),
        in_specs=[pl.BlockSpec((tm, D_MODEL), row), pl.BlockSpec((tm, D_MODEL), row),
                  pl.BlockSpec((D_MODEL, D_MODEL), const), pl.BlockSpec((1, D_MODEL), const),
                  pl.BlockSpec((D_MODEL, LANES), const), pl.BlockSpec((D_MODEL, LANES), const),
                  pl.BlockSpec((tm, tm), const)],
        out_specs=[pl.BlockSpec((tm, D_MODEL), row),
                   pl.BlockSpec((tm, ROW_TILES, LANES), lambda i: (i, 0, 0)),
                   pl.BlockSpec((tm, LANES), row), pl.BlockSpec((8, LANES), const)],
        out_shape=(jax.ShapeDtypeStruct((S, D_MODEL), F32),
                   jax.ShapeDtypeStruct((S, ROW_TILES, LANES), F32),
                   jax.ShapeDtypeStruct((S, LANES), F32), jax.ShapeDtypeStruct((8, LANES), F32)),
        scratch_shapes=[pltpu.VMEM((8, LANES), F32)],
        compiler_params=_cparams(("arbitrary",)),
        name="c_out_router",
    )(o, x2d, wo, g, rhi, rlo, tri)


MOE_TM = 256
PAD_PIECES = (128, 64, 32, 16, 8, 4, 2, 1)


def _dispatch_kernel(pad_ref, slot_ref, h_ref, xs_ref, zero_sc, sem, zsem, *, ts):
    i = pl.program_id(0)

    @pl.when(i == 0)
    def _():
        zero_sc[...] = jnp.zeros_like(zero_sc)
        for e in range(N_EXPERTS):
            start = pad_ref[0, e]
            npad = pad_ref[1, e]
            for p in PAD_PIECES:
                hit = (npad & p) != 0

                @pl.when(hit)
                def _(start=start, p=p):
                    cp = pltpu.make_async_copy(zero_sc.at[pl.ds(0, p)],
                                               xs_ref.at[pl.ds(start, p)], zsem)
                    cp.start()
                    cp.wait()

                start = start + jnp.where(hit, p, 0)

        half = MOE_TM // 2

        def zero_tile(tile, c):
            for part in range(2):
                cp = pltpu.make_async_copy(
                    zero_sc, xs_ref.at[pl.ds(tile * MOE_TM + part * half, half)], zsem)
                cp.start()
                cp.wait()
            return c

        lax.fori_loop(pad_ref[2, 0], xs_ref.shape[0] // MOE_TM, zero_tile, 0)

    def row_copy(r, k):
        return pltpu.make_async_copy(h_ref.at[r], xs_ref.at[slot_ref[0, 0, 2 * r + k]], sem)

    def issue(r, c):
        row_copy(r, 0).start(priority=0)
        row_copy(r, 1).start(priority=1)
        return c

    lax.fori_loop(0, ts, issue, 0, unroll=8)
    for _ in range(2):
        pltpu.make_async_copy(h_ref, xs_ref.at[pl.ds(0, ts)], sem).wait()


def _dispatch(pad_info, slots3, h2, n_slots, ts):
    S = h2.shape[0]
    return pl.pallas_call(
        functools.partial(_dispatch_kernel, ts=ts),
        grid=(S // ts,),
        in_specs=[pl.BlockSpec(memory_space=pltpu.SMEM),
                  pl.BlockSpec((1, 1, 2 * ts), lambda i: (i, 0, 0), memory_space=pltpu.SMEM),
                  pl.BlockSpec((ts, ROW_TILES, LANES), lambda i: (i, 0, 0))],
        out_specs=pl.BlockSpec(memory_space=pl.ANY),
        out_shape=jax.ShapeDtypeStruct((n_slots, ROW_TILES, LANES), F32),
        scratch_shapes=[pltpu.VMEM((MOE_TM // 2, ROW_TILES, LANES), F32),
                        pltpu.SemaphoreType.DMA(()), pltpu.SemaphoreType.DMA(())],
        compiler_params=_cparams(("arbitrary",)),
        name="moe_dispatch",
    )(pad_info, slots3, h2)


MOE_FC = 512


N_FC = D_FF_EXPERT // MOE_FC


def _moe_kernel(te_ref, tv_ref, xs_ref, wgu_ref, wdn_ref, y_ref, x_sc, act_sc, wg_sc, wu_sc,
                wd_sc, stage_in, stage_out, sem):
    t = pl.program_id(0)
    e = te_ref[t]
    valid = tv_ref[t] != 0

    @pl.when(valid & ((t == 0) | (e != te_ref[jnp.maximum(t - 1, 0)])))
    def _():
        def chunk(j):
            c = j % N_FC
            cols = pl.ds(c * MOE_FC, MOE_FC)
            if j < N_FC:
                return (wgu_ref.at[e, :, cols], stage_in, wg_sc.at[:, cols])
            if j < 2 * N_FC:
                return (wgu_ref.at[e, :, pl.ds(D_FF_EXPERT + c * MOE_FC, MOE_FC)], stage_in,
                        wu_sc.at[:, cols])
            return (wdn_ref.at[e, cols, :], stage_out, wd_sc.at[cols, :])

        def copy(j):
            src, stage, _ = chunk(j)
            return pltpu.make_async_copy(src, stage.at[j % 2], sem.at[j % 2])

        copy(0).start()
        for j in range(3 * N_FC):
            if j + 1 < 3 * N_FC:
                copy(j + 1).start()
            copy(j).wait()
            _, stage, dst = chunk(j)
            dst[...] = stage[j % 2].astype(BF16)

    @pl.when(jnp.logical_not(valid))
    def _():
        y_ref[...] = jnp.zeros_like(y_ref)

    @pl.when(valid)
    def _():
        for c in range(ROW_TILES):
            x_sc[:, c * LANES:(c + 1) * LANES] = xs_ref[:, c, :].astype(BF16)
        x = x_sc[...]
        for c in range(N_FC):
            sl = slice(c * MOE_FC, (c + 1) * MOE_FC)
            gate = _dot(x, wg_sc[:, sl])
            up = _dot(x, wu_sc[:, sl])
            act_sc[:, sl] = (gate * jax.nn.sigmoid(gate) * up).astype(BF16)
        acc = _dot(act_sc[...], wd_sc[...])
        for c in range(ROW_TILES):
            y_ref[:, c, :] = acc[:, c * LANES:(c + 1) * LANES]


def _moe(tile_expert, tile_valid, xs, w_gu, w_down):
    n_slots = xs.shape[0]
    nt = n_slots // MOE_TM
    grid_spec = pltpu.PrefetchScalarGridSpec(
        num_scalar_prefetch=2,
        grid=(nt,),
        in_specs=[
            pl.BlockSpec((MOE_TM, ROW_TILES, LANES), lambda t, te, tv: (t, 0, 0)),
            pl.BlockSpec(memory_space=pl.ANY),
            pl.BlockSpec(memory_space=pl.ANY),
        ],
        out_specs=pl.BlockSpec((MOE_TM, ROW_TILES, LANES), lambda t, te, tv: (t, 0, 0)),
        scratch_shapes=[pltpu.VMEM((MOE_TM, D_MODEL), BF16),
                        pltpu.VMEM((MOE_TM, D_FF_EXPERT), BF16),
                        pltpu.VMEM((D_MODEL, D_FF_EXPERT), BF16),
                        pltpu.VMEM((D_MODEL, D_FF_EXPERT), BF16),
                        pltpu.VMEM((D_FF_EXPERT, D_MODEL), BF16),
                        pltpu.VMEM((2, D_MODEL, MOE_FC), F32),
                        pltpu.VMEM((2, MOE_FC, D_MODEL), F32),
                        pltpu.SemaphoreType.DMA((2,))],
    )
    return pl.pallas_call(
        _moe_kernel,
        grid_spec=grid_spec,
        out_shape=jax.ShapeDtypeStruct((n_slots, ROW_TILES, LANES), F32),
        compiler_params=_cparams(("arbitrary",)),
        name="moe_experts",
    )(tile_expert, tile_valid, xs, w_gu, w_down)


def _combine_kernel(slot_ref, next_slot_ref, x3_ref, meta_ref, y_ref, o_ref, buf, sem, *, ts):
    i = pl.program_id(0)
    n = pl.num_programs(0)
    cur = lax.rem(i, 2)

    def gather(slots, half):
        def issue(r, c):
            for k in range(2):
                pltpu.make_async_copy(y_ref.at[slots[0, 0, 2 * r + k]], buf.at[half, k, r],
                                      sem.at[half]).start(priority=k)
            return c

        lax.fori_loop(0, ts, issue, 0, unroll=8)

    @pl.when(i == 0)
    def _():
        gather(slot_ref, 0)

    @pl.when(i + 1 < n)
    def _():
        gather(next_slot_ref, 1 - cur)

    for k in range(2):
        pltpu.make_async_copy(y_ref.at[pl.ds(0, ts)], buf.at[cur, k], sem.at[cur]).wait()
    meta = meta_ref[...]
    g1 = jnp.broadcast_to(meta[:, 2:3], (ts, LANES))
    g2 = jnp.broadcast_to(meta[:, 3:4], (ts, LANES))
    for c in range(ROW_TILES):
        sl = slice(c * LANES, (c + 1) * LANES)
        o_ref[:, sl] = x3_ref[:, sl] + g1 * buf[cur, 0, :, c, :] + g2 * buf[cur, 1, :, c, :]


def _combine(slots3, x3, meta, y, ts):
    S = x3.shape[0]
    n = S // ts
    return pl.pallas_call(
        functools.partial(_combine_kernel, ts=ts),
        grid=(n,),
        in_specs=[pl.BlockSpec((1, 1, 2 * ts), lambda i: (i, 0, 0), memory_space=pltpu.SMEM),
                  pl.BlockSpec((1, 1, 2 * ts), lambda i: (jnp.minimum(i + 1, n - 1), 0, 0),
                               memory_space=pltpu.SMEM),
                  pl.BlockSpec((ts, D_MODEL), lambda i: (i, 0)),
                  pl.BlockSpec((ts, LANES), lambda i: (i, 0)),
                  pl.BlockSpec(memory_space=pl.ANY)],
        out_specs=pl.BlockSpec((ts, D_MODEL), lambda i: (i, 0)),
        out_shape=jax.ShapeDtypeStruct((S, D_MODEL), F32),
        scratch_shapes=[pltpu.VMEM((2, 2, ts, ROW_TILES, LANES), F32),
                        pltpu.SemaphoreType.DMA((2,))],
        compiler_params=_cparams(("arbitrary",)),
        name="moe_combine",
    )(slots3, slots3, x3, meta, y)


def _tile_lanes(v, reps):
    return jnp.tile(v, reps)[None, :].astype(F32)


def _rope_cos_sin(S, dim):
    pos = np.arange(S, dtype=np.float64)
    inv = ROPE_THETA ** (-np.arange(0, dim, 2, dtype=np.float64) / dim)
    ang = pos[:, None] * inv[None, :]
    return np.cos(ang), np.sin(ang)


def _rope_tables_64(S):
    c, s = _rope_cos_sin(S, A_HEAD_DIM)
    return (jnp.asarray(np.concatenate([c, c, c, c], axis=1), F32),
            jnp.asarray(np.concatenate([-s, s, -s, s], axis=1), F32))


def _rope_tables_32(S):
    c, s = _rope_cos_sin(S, C_ROPE)
    one = np.ones((S, C_NOPE))
    zero = np.zeros((S, C_NOPE))
    pad = np.zeros((S, LANES - C_QK))
    return (jnp.asarray(np.concatenate([one, c, c, pad], axis=1), F32),
            jnp.asarray(np.concatenate([zero, -s, s, pad], axis=1), F32))


def _pad_heads(w, n_heads, width):
    k = w.shape[0]
    w = w.reshape(k, n_heads, width)
    return jnp.pad(w, ((0, 0), (0, 0), (0, LANES - width))).reshape(k, n_heads * LANES)


def _swap_rope_cols(w_p):
    k = w_p.shape[0]
    w = w_p.reshape(k, -1, LANES)
    half = C_ROPE // 2
    sw = jnp.concatenate([jnp.zeros_like(w[:, :, :C_NOPE]),
                          w[:, :, C_NOPE + half:C_QK], w[:, :, C_NOPE:C_NOPE + half],
                          jnp.zeros_like(w[:, :, C_QK:])], axis=2)
    return sw.reshape(k, -1)


def kernel(x, l0_norm_attn, l0_w_in, l0_a_q_norm, l0_a_k_norm, l0_a_lambda, l0_a_subln, l0_b_q_norm, l0_b_k_norm, l0_b_sinks, l0_w_out, l0_norm_ffn, l0_ffn_w_gu, l0_ffn_w_down, l1_norm_attn, l1_c_w_in, l1_c_q_lora_norm, l1_c_kv_lora_norm, l1_c_w_uq, l1_c_w_ukv, l1_c_q_norm, l1_c_k_norm, l1_c_w_out, l1_norm_ffn, l1_router, l1_exp_w_gu, l1_exp_w_down):
    B, S, _ = x.shape
    assert B == 1
    x2d = x.reshape(S, D_MODEL)
    tm = min(512, S)

    o_bk = 3 * A_W + B_QW
    o_bv = o_bk + B_KW
    bk_w = l0_w_in[:, o_bk:o_bv].reshape(D_MODEL, B_KV_HEADS, B_HEAD_DIM)
    bv_w = l0_w_in[:, o_bv:].reshape(D_MODEL, B_KV_HEADS, B_HEAD_DIM)
    dup = lambda w: jnp.concatenate([w, w], axis=2).reshape(D_MODEL, 2 * B_KW)
    w_in_p = jnp.concatenate([l0_w_in[:, :2 * A_W], l0_w_in[:, 3 * A_W:o_bk], dup(bk_w),
                              dup(bv_w)], axis=1).astype(BF16)
    w_avt = l0_w_in[:, 2 * A_W:3 * A_W].T.reshape(A_HEADS, 2 * A_HEAD_DIM, D_MODEL)
    w_avt = jnp.pad(w_avt, ((0, 0), (0, V_PAD), (0, 0))).reshape(-1, D_MODEL).astype(BF16)
    a_vone = jnp.zeros((A_HEADS, A_V_ROWS, 1), F32).at[:, 2 * A_HEAD_DIM, 0].set(1.0).reshape(-1, 1)
    gains = jnp.concatenate([_tile_lanes(l0_a_q_norm, 2), _tile_lanes(l0_a_k_norm, 2),
                             _tile_lanes(l0_b_q_norm, 2), _tile_lanes(l0_b_k_norm, 2),
                             jnp.zeros((4, LANES), F32)], axis=0)
    lane = jnp.arange(LANES)
    ones64 = (lane[:, None] // 64 == lane[None, :] // 64).astype(BF16)
    cos64, sin64 = _rope_tables_64(S)
    aq, ak, avt, bq, bk, bv = _l0_in(x2d, l0_norm_attn[None, :], w_in_p, w_avt, a_vone, gains,
                                     ones64, cos64, sin64, tm)
    lam_init = 0.8 - 0.6 * math.exp(-0.3 * 0)
    a_out = _a_attn(l0_a_lambda.astype(F32), aq, ak, avt, l0_a_subln[None, :].astype(F32),
                    lam_init)
    b_out = _b_attn(l0_b_sinks.astype(F32), bq, bk, bv)
    w_out = l0_w_out.astype(BF16)
    x2 = _l0_ffn(a_out, b_out, x2d, w_out[:A_W], w_out[A_W:], l0_norm_ffn[None, :],
                 l0_ffn_w_gu.astype(BF16), l0_ffn_w_down.astype(BF16), tm, D_FF // 2)

    wq = l1_c_w_in[:, :C_Q_RANK].astype(BF16)
    wkv = l1_c_w_in[:, C_Q_RANK:C_Q_RANK + C_KV_RANK].astype(BF16)
    wkr = jnp.pad(l1_c_w_in[:, C_Q_RANK + C_KV_RANK:], ((0, 0), (C_NOPE, LANES - C_QK)))
    wkrs = _swap_rope_cols(wkr).astype(BF16)
    wkr = wkr.astype(BF16)
    wuq = _pad_heads(l1_c_w_uq, C_HEADS, C_QK)
    wuqs = _swap_rope_cols(wuq).astype(BF16)
    wuq = wuq.astype(BF16)
    ukv = l1_c_w_ukv.reshape(C_KV_RANK, C_HEADS, C_NOPE + C_V)
    wuk = _pad_heads(ukv[:, :, :C_NOPE].reshape(C_KV_RANK, -1), C_HEADS, C_NOPE).astype(BF16)
    wuv = jnp.pad(jnp.transpose(ukv[:, :, C_NOPE:], (1, 2, 0)), ((0, 0), (0, V_PAD), (0, 0)))
    wuv = wuv.reshape(C_HEADS * V_ROWS, C_KV_RANK).astype(BF16)
    vone = jnp.zeros((C_HEADS, V_ROWS, 1), F32).at[:, C_V, 0].set(1.0).reshape(-1, 1)
    pad_gain = lambda gvec: jnp.pad(gvec.astype(F32), (0, LANES - C_QK))[None, :]
    qg, kg = pad_gain(l1_c_q_norm), pad_gain(l1_c_k_norm)
    qgs, kgs = _swap_rope_cols(qg), _swap_rope_cols(kg)
    ones128 = jnp.ones((LANES, LANES), BF16)
    lane2 = jnp.arange(2 * LANES)
    ones_pair = (lane2[:, None] // LANES == lane2[None, :] // LANES).astype(BF16)
    cos32, sin32 = _rope_tables_32(S)
    cq, ck, cvt = _c_in(x2, l1_norm_attn[None, :], wq, wkv, wkr, wkrs,
                       l1_c_q_lora_norm[None, :], l1_c_kv_lora_norm[None, :],
                       wuq, wuqs, wuk, wuv, qg, qgs, kg, kgs, vone, ones128, ones_pair, cos32,
                       sin32, tm)
    c_o = _c_attn(cq, ck, cvt)

    r_pad = jnp.pad(l1_router.astype(F32), ((0, 0), (0, LANES - N_EXPERTS)))
    r_hi = r_pad.astype(BF16)
    r_lo = (r_pad - r_hi.astype(F32)).astype(BF16)
    tr = min(256, S)
    ridx = jnp.arange(tr)
    tri = (ridx[None, :] < ridx[:, None]).astype(BF16)
    x3, h2, meta, cnt = _c_out_router(c_o, x2, l1_c_w_out.astype(BF16), l1_norm_ffn[None, :],
                                      r_hi, r_lo, tri, tr)

    counts = cnt[0, :N_EXPERTS].astype(jnp.int32)
    padded = ((counts + MOE_TM - 1) // MOE_TM) * MOE_TM
    ends = jnp.cumsum(padded)
    offs = ends - padded
    idx = meta[:, 0:2].astype(jnp.int32)
    pos = meta[:, 4:6].astype(jnp.int32)
    slots = offs[idx] + pos
    n_slots = 2 * S + N_EXPERTS * MOE_TM
    nt = n_slots // MOE_TM
    tile_start = jnp.arange(nt, dtype=jnp.int32) * MOE_TM
    n_valid = jnp.broadcast_to(ends[-1] // MOE_TM, (N_EXPERTS,))
    tile_valid = (tile_start < ends[-1]).astype(jnp.int32)
    tile_expert = jnp.minimum(
        jnp.sum((tile_start[:, None] >= ends[None, :]).astype(jnp.int32), axis=1),
        N_EXPERTS - 1).astype(jnp.int32)
    pad_info = jnp.stack([offs + counts, padded - counts, n_valid]).astype(jnp.int32)

    ts = min(512, S)
    slots3 = slots.reshape(S // ts, 1, 2 * ts)
    xs = _dispatch(pad_info, slots3, h2, n_slots, ts)
    y = _moe(tile_expert, tile_valid, xs,
             l1_exp_w_gu.astype(F32), l1_exp_w_down.astype(F32))
    out = _combine(slots3, x3, meta, y, ts)
    return out.reshape(B, S, D_MODEL)
```

```python
import functools
import math

import jax
import jax.numpy as jnp
import numpy as np
from jax import lax
from jax.experimental import pallas as pl
from jax.experimental.pallas import tpu as pltpu

F32 = jnp.float32
BF16 = jnp.bfloat16

D_MODEL = 1024
CHUNK = 64
ROPE_THETA = 10000.0
EPS = 1e-6
LANES = 128
ROW_TILES = D_MODEL // LANES

A_HEADS = 4
A_HEAD_DIM = 64
B_HEADS = 8
B_KV_HEADS = 2
B_HEAD_DIM = 64
A_W = A_HEADS * 2 * A_HEAD_DIM
B_QW = B_HEADS * B_HEAD_DIM
B_KW = B_KV_HEADS * B_HEAD_DIM

C_HEADS = 16
C_Q_RANK = 256
C_KV_RANK = 128
C_NOPE = 64
C_ROPE = 32
C_V = 64
C_QK = C_NOPE + C_ROPE

D_FF = 2816
N_EXPERTS = 8
D_FF_EXPERT = 3584

NEG_BIG = -1e30
LOG2E = 1.0 / math.log(2.0)
VMEM_LIMIT = 56 * 1024 * 1024


def _cparams(sem):
    return pltpu.CompilerParams(dimension_semantics=sem, vmem_limit_bytes=VMEM_LIMIT)


def _dot(a, b):
    return jnp.dot(a, b, preferred_element_type=F32)


def _dot_nt(a, b):
    return lax.dot_general(a, b, (((1,), (1,)), ((), ())), preferred_element_type=F32)


def _rms(x, g):
    ms = jnp.mean(x * x, axis=-1, keepdims=True)
    return x * lax.rsqrt(ms + EPS) * g


def _l0_in_kernel(x_ref, g_ref, w_ref, wvt_ref, vone_ref, gains_ref, ones_ref, cos_ref, sin_ref,
                  aq_o, ak_o, avt_o, bq_o, bk_o, bv_o):
    h = _rms(x_ref[...], g_ref[...]).astype(BF16)
    z = _dot(h, w_ref[...])
    avt_o[...] = (_dot_nt(wvt_ref[...], h) + vone_ref[...]).astype(BF16)
    cos = cos_ref[...]
    sin = sin_ref[...]
    ones_blk = ones_ref[...]
    lane = lax.broadcasted_iota(jnp.int32, (1, LANES), 1)
    first_half = (lane % 64) < 32

    def norm_rope(zs, gain, scale):
        ss = _dot((zs * zs).astype(BF16), ones_blk)
        r = lax.rsqrt(ss * (1.0 / 64.0) + EPS) * scale
        y = zs * gain
        sw = jnp.where(first_half, pltpu.roll(y, 96, 1), pltpu.roll(y, 32, 1))
        return ((y * cos + sw * sin) * r).astype(BF16)

    col = 0
    gcol = 0
    scale = A_HEAD_DIM ** -0.5
    scales = {"aq": scale * LOG2E, "ak": 1.0, "bq": scale, "bk": 1.0}
    outs = {"aq": aq_o, "ak": ak_o, "bq": bq_o, "bk": bk_o}
    for name, nslab in (("aq", 4), ("ak", 4), ("bq", 4), ("bk", 2), ("bv", 2)):
        if name == "bv":
            bv_o[...] = z[:, col:col + 2 * LANES].astype(BF16)
            col += 2 * LANES
            continue
        sc = scales[name]
        gain = gains_ref[gcol:gcol + 1, :]
        gcol += 1
        for s in range(nslab):
            zs = z[:, col:col + LANES]
            outs[name][:, s * LANES:(s + 1) * LANES] = norm_rope(zs, gain, sc)
            col += LANES


def _l0_in(x2d, g, w_in_p, w_avt, vone, gains, ones_blk, cos_t, sin_t, tm):
    S = x2d.shape[0]
    nw = w_in_p.shape[1]
    row = lambda i: (i, 0)
    const = lambda i: (0, 0)
    out_shapes = (
        jax.ShapeDtypeStruct((S, A_W), BF16), jax.ShapeDtypeStruct((S, A_W), BF16),
        jax.ShapeDtypeStruct((A_HEADS * A_V_ROWS, S), BF16), jax.ShapeDtypeStruct((S, B_QW), BF16),
        jax.ShapeDtypeStruct((S, 2 * B_KW), BF16), jax.ShapeDtypeStruct((S, 2 * B_KW), BF16))
    return pl.pallas_call(
        _l0_in_kernel,
        grid=(S // tm,),
        in_specs=[
            pl.BlockSpec((tm, D_MODEL), row),
            pl.BlockSpec((1, D_MODEL), const),
            pl.BlockSpec((D_MODEL, nw), const),
            pl.BlockSpec((A_HEADS * A_V_ROWS, D_MODEL), const),
            pl.BlockSpec((A_HEADS * A_V_ROWS, 1), const),
            pl.BlockSpec((8, LANES), const),
            pl.BlockSpec((LANES, LANES), const),
            pl.BlockSpec((tm, LANES), row),
            pl.BlockSpec((tm, LANES), row),
        ],
        out_specs=[
            pl.BlockSpec((tm, A_W), row), pl.BlockSpec((tm, A_W), row),
            pl.BlockSpec((A_HEADS * A_V_ROWS, tm), lambda i: (0, i)), pl.BlockSpec((tm, B_QW), row),
            pl.BlockSpec((tm, 2 * B_KW), row), pl.BlockSpec((tm, 2 * B_KW), row)],
        out_shape=out_shapes,
        compiler_params=_cparams(("arbitrary",)),
        name="l0_in_proj",
    )(x2d, g, w_in_p, w_avt, vone, gains, ones_blk, cos_t, sin_t)


ATT_TQ = 512
ATT_TK = 256
V_PAD = 16


def _softmax_pv(s_ref, s_max, vt, m, acc):
    m_new = jnp.maximum(m, s_max)
    alpha = jnp.exp2(m - m_new)
    p = jnp.exp2(s_ref[...] - m_new).astype(BF16)
    return m_new, alpha * acc + _dot(vt, p)


def _diag_masks():
    key_chunk = lax.broadcasted_iota(jnp.int32, (ATT_TK, ATT_TQ), 0) // CHUNK
    query_chunk = lax.broadcasted_iota(jnp.int32, (ATT_TK, ATT_TQ), 1) // CHUNK
    return [key_chunk + b * (ATT_TK // CHUNK) <= query_chunk for b in range(ATT_TQ // ATT_TK)]


def _attn_pipeline(i, k_ref, vt_ref, streams, v_rows):
    n = len(streams)

    for st in streams:
        st[5][...] = st[0].T

    def scores(st, blk, mask, dst):
        lanes, qt_ref = streams[st][1], streams[st][5]
        off = pl.multiple_of(blk * ATT_TK, ATT_TK)
        s = _dot(k_ref[pl.ds(off, ATT_TK), lanes], qt_ref[...])
        if mask is not None:
            s = jnp.where(mask, s, NEG_BIG)
        dst[...] = s
        return jnp.max(s, axis=0, keepdims=True)

    def consume(st, blk, src, s_max, state):
        off = pl.multiple_of(blk * ATT_TK, ATT_TK)
        return _softmax_pv(src, s_max, vt_ref[streams[st][2], pl.ds(off, ATT_TK)], *state)

    buf_a = [st[3] for st in streams]
    buf_b = [st[4] for st in streams]
    mask0, mask1 = _diag_masks()
    d0 = 2 * i
    d1 = d0 + 1
    state = [(jnp.full((1, ATT_TQ), NEG_BIG, F32), jnp.zeros((v_rows, ATT_TQ), F32))] * n
    max_a = [scores(st, d0, mask0, buf_a[st]) for st in range(n)]
    max_b = [None] * n
    for st in range(n):
        max_b[st] = scores(st, d1, mask1, buf_b[st])
        state[st] = consume(st, d0, buf_a[st], max_a[st], state[st])
    for st in range(n):
        max_a[st] = scores(st, 0, None, buf_a[st])
        state[st] = consume(st, d1, buf_b[st], max_b[st], state[st])

    def body(p, carry):
        carry, max_a = list(carry[0]), list(carry[1])
        max_b = [None] * n
        u0 = 2 * p
        u1 = u0 + 1
        nxt = jnp.minimum(u0 + 2, d0 - 1)
        for st in range(n):
            max_b[st] = scores(st, u1, None, buf_b[st])
            carry[st] = consume(st, u0, buf_a[st], max_a[st], carry[st])
        for st in range(n):
            max_a[st] = scores(st, nxt, None, buf_a[st])
            carry[st] = consume(st, u1, buf_b[st], max_b[st], carry[st])
        return tuple(carry), tuple(max_a)

    final, _ = lax.fori_loop(0, i, body, (tuple(state), tuple(max_a)))
    return [acc for _, acc in final]


_SCORE_BUF = pltpu.VMEM((ATT_TK, ATT_TQ), F32)
_QT_BUF = pltpu.VMEM((LANES, ATT_TQ), BF16)
_ATTN_SCRATCH = [_SCORE_BUF] * 4 + [_QT_BUF] * 2


A_V_ROWS = 2 * A_HEAD_DIM + V_PAD


A_HPS = 2


def _a_attn_kernel(lam_ref, q_ref, k_ref, vt_ref, subln_ref, o_ref, *scratch, lam_init):
    i = pl.program_id(1)
    lane = lax.broadcasted_iota(jnp.int32, (1, LANES), 1)
    streams = []
    for h in range(A_HPS):
        q = q_ref[:, h * LANES:(h + 1) * LANES]
        zero = jnp.zeros_like(q)
        lanes = slice(h * LANES, (h + 1) * LANES)
        rows = slice(h * A_V_ROWS, (h + 1) * A_V_ROWS)
        for mp, sel in enumerate((lane < 64, lane >= 64)):
            st = 2 * h + mp
            streams.append((jnp.where(sel, q, zero), lanes, rows, scratch[2 * st],
                            scratch[2 * st + 1], scratch[4 * A_HPS + st]))
    accs = _attn_pipeline(i, k_ref, vt_ref, tuple(streams), A_V_ROWS)
    dv = 2 * A_HEAD_DIM
    lf = lam_ref[...]
    lam = (jnp.exp(jnp.sum(lf[0:1] * lf[1:2], axis=1, keepdims=True))
           - jnp.exp(jnp.sum(lf[2:3] * lf[3:4], axis=1, keepdims=True)) + lam_init)
    for h in range(A_HPS):
        acc1, acc2 = accs[2 * h], accs[2 * h + 1]
        out = (acc1[:dv] / acc1[dv:dv + 1] - lam * (acc2[:dv] / acc2[dv:dv + 1])).T
        out = _rms(out, subln_ref[...]) * (1.0 - lam_init)
        o_ref[:, h * LANES:(h + 1) * LANES] = out.astype(BF16)


def _a_attn(lam_p, aq, ak, avt, subln, lam_init):
    S = aq.shape[0]
    once = pl.Buffered(1)
    return pl.pallas_call(
        functools.partial(_a_attn_kernel, lam_init=lam_init),
        grid=(A_HEADS // A_HPS, S // ATT_TQ),
        in_specs=[
            pl.BlockSpec((4, A_HEAD_DIM), lambda h, i: (0, 0)),
            pl.BlockSpec((ATT_TQ, A_HPS * LANES), lambda h, i: (i, h)),
            pl.BlockSpec((S, A_HPS * LANES), lambda h, i: (0, h), pipeline_mode=once),
            pl.BlockSpec((A_HPS * A_V_ROWS, S), lambda h, i: (h, 0), pipeline_mode=once),
            pl.BlockSpec((1, LANES), lambda h, i: (0, 0)),
        ],
        out_specs=pl.BlockSpec((ATT_TQ, A_HPS * LANES), lambda h, i: (i, h)),
        out_shape=jax.ShapeDtypeStruct((S, A_W), BF16),
        scratch_shapes=[_SCORE_BUF] * (4 * A_HPS) + [_QT_BUF] * (2 * A_HPS),
        compiler_params=_cparams(("arbitrary", "arbitrary")),
        name="a_diff_attn",
    )(lam_p, aq, ak, avt, subln)


B_BLK = 256
B_BACK = 128


def _b_attn_kernel(sink_ref, q_ref, kp_ref, kc_ref, vp_ref, vc_ref, o_ref):
    i = pl.program_id(0)
    lane = lax.broadcasted_iota(jnp.int32, (1, LANES), 1)
    lo = lane < 64
    r = lax.broadcasted_iota(jnp.int32, (B_BLK, B_BACK + B_BLK), 0) // CHUNK
    c_idx = lax.broadcasted_iota(jnp.int32, (B_BLK, B_BACK + B_BLK), 1)
    c = c_idx // CHUNK
    mask = (c >= r) & (c <= r + 2) & ((c_idx >= B_BACK) | (i > 0))
    for slab in range(B_HEADS // 2):
        g = slab // 2
        k = jnp.concatenate([kp_ref[:, g * LANES:(g + 1) * LANES],
                             kc_ref[:, g * LANES:(g + 1) * LANES]], axis=0)
        v = jnp.concatenate([vp_ref[:, g * LANES:(g + 1) * LANES],
                             vc_ref[:, g * LANES:(g + 1) * LANES]], axis=0)
        qs = q_ref[:, slab * LANES:(slab + 1) * LANES]
        zq = jnp.zeros_like(qs)
        zv = jnp.zeros_like(v)
        out = jnp.zeros((B_BLK, LANES), F32)
        for half in range(2):
            sel = lo if half == 0 else jnp.logical_not(lo)
            sink = sink_ref[2 * slab + half]
            s = _dot_nt(jnp.where(sel, qs, zq), k)
            s = jnp.where(mask, s, NEG_BIG)
            m = jnp.maximum(jnp.max(s, axis=1, keepdims=True), sink)
            e = jnp.exp(s - m)
            denom = jnp.sum(e, axis=1, keepdims=True) + jnp.exp(sink - m)
            p = (e / denom).astype(BF16)
            out = out + _dot(p, jnp.where(sel, v, zv))
        o_ref[:, slab * LANES:(slab + 1) * LANES] = out.astype(BF16)


def _b_attn(sinks, bq, bk, bv):
    S = bq.shape[0]
    prev = lambda i: (jnp.maximum(i * (B_BLK // B_BACK) - 1, 0), 0)
    cur = lambda i: (i, 0)
    return pl.pallas_call(
        _b_attn_kernel,
        grid=(S // B_BLK,),
        in_specs=[
            pl.BlockSpec(memory_space=pltpu.SMEM),
            pl.BlockSpec((B_BLK, B_QW), cur),
            pl.BlockSpec((B_BACK, 2 * B_KW), prev),
            pl.BlockSpec((B_BLK, 2 * B_KW), cur),
            pl.BlockSpec((B_BACK, 2 * B_KW), prev),
            pl.BlockSpec((B_BLK, 2 * B_KW), cur),
        ],
        out_specs=pl.BlockSpec((B_BLK, B_QW), cur),
        out_shape=jax.ShapeDtypeStruct((S, B_QW), BF16),
        compiler_params=_cparams(("arbitrary",)),
        name="b_swa_attn",
    )(sinks, bq, bk, bk, bv, bv)


def _l0_ffn_kernel(a_ref, b_ref, x_ref, woa_ref, wob_ref, g_ref, wg_ref, wu_ref, wd_ref,
                   o_ref, h_sc, acc_sc):
    j = pl.program_id(1)

    @pl.when(j == 0)
    def _():
        x1 = x_ref[...] + _dot(a_ref[...], woa_ref[...]) + _dot(b_ref[...], wob_ref[...])
        acc_sc[...] = x1
        h_sc[...] = _rms(x1, g_ref[...]).astype(BF16)

    h = h_sc[...]
    gate = _dot(h, wg_ref[...])
    up = _dot(h, wu_ref[...])
    act = (gate * jax.nn.sigmoid(gate) * up).astype(BF16)
    acc_sc[...] += _dot(act, wd_ref[...])

    @pl.when(j == pl.num_programs(1) - 1)
    def _():
        o_ref[...] = acc_sc[...]


def _l0_ffn(a_out, b_out, x2d, wo_a, wo_b, g, w_gu, w_down, tm, tf):
    S = x2d.shape[0]
    nf = D_FF // tf
    row = lambda i, j: (i, 0)
    const = lambda i, j: (0, 0)
    return pl.pallas_call(
        _l0_ffn_kernel,
        grid=(S // tm, nf),
        in_specs=[
            pl.BlockSpec((tm, A_W), row),
            pl.BlockSpec((tm, B_QW), row),
            pl.BlockSpec((tm, D_MODEL), row),
            pl.BlockSpec((A_W, D_MODEL), const),
            pl.BlockSpec((B_QW, D_MODEL), const),
            pl.BlockSpec((1, D_MODEL), const),
            pl.BlockSpec((D_MODEL, tf), lambda i, j: (0, j)),
            pl.BlockSpec((D_MODEL, tf), lambda i, j: (0, nf + j)),
            pl.BlockSpec((tf, D_MODEL), lambda i, j: (j, 0)),
        ],
        out_specs=pl.BlockSpec((tm, D_MODEL), row),
        out_shape=jax.ShapeDtypeStruct((S, D_MODEL), F32),
        scratch_shapes=[pltpu.VMEM((tm, D_MODEL), BF16), pltpu.VMEM((tm, D_MODEL), F32)],
        compiler_params=_cparams(("arbitrary", "arbitrary")),
        name="l0_out_ffn",
    )(a_out, b_out, x2d, wo_a, wo_b, g, w_gu, w_gu, w_down)


def _c_in_kernel(x_ref, g_ref, wq_ref, wkv_ref, wkr_ref, wkrs_ref, gql_ref, gkvl_ref,
                 wuq_ref, wuqs_ref, wuk_ref, wuv_ref, qg_ref, qgs_ref, kg_ref, kgs_ref,
                 vone_ref, ones_ref, ones2_ref, cos_ref, sin_ref, q_o, k_o, vt_o):
    h = _rms(x_ref[...], g_ref[...]).astype(BF16)
    cq = _rms(_dot(h, wq_ref[...]), gql_ref[...]).astype(BF16)
    ckv = _rms(_dot(h, wkv_ref[...]), gkvl_ref[...]).astype(BF16)
    kr = _dot(h, wkr_ref[...])
    krs = _dot(h, wkrs_ref[...])
    q = _dot(cq, wuq_ref[...])
    qs = _dot(cq, wuqs_ref[...])
    kn = _dot(ckv, wuk_ref[...])
    vt_o[...] = (_dot_nt(wuv_ref[...], ckv) + vone_ref[...]).astype(BF16)

    cos = cos_ref[...]
    sin = sin_ref[...]
    ones_blk = ones_ref[...]
    qg, qgs, kg, kgs = qg_ref[...], qgs_ref[...], kg_ref[...], kgs_ref[...]
    ss_kr = _dot((kr * kr).astype(BF16), ones_blk)
    kr_roped = kr * kg * cos + krs * kgs * sin
    scale = C_QK ** -0.5 * LOG2E
    inv = 1.0 / C_QK
    ones_pair = ones2_ref[...]
    for pair in range(C_HEADS // 2):
        both = slice(2 * pair * LANES, (2 * pair + 2) * LANES)
        ss_q = _dot((q[:, both] * q[:, both]).astype(BF16), ones_pair)
        ss_k = _dot((kn[:, both] * kn[:, both]).astype(BF16), ones_pair)
        for half in range(2):
            sl = slice((2 * pair + half) * LANES, (2 * pair + half + 1) * LANES)
            hs = slice(half * LANES, (half + 1) * LANES)
            qh = q[:, sl]
            r = lax.rsqrt(ss_q[:, hs] * inv + EPS) * scale
            q_o[:, sl] = ((qh * qg * cos + qs[:, sl] * qgs * sin) * r).astype(BF16)
            rk = lax.rsqrt((ss_k[:, hs] + ss_kr) * inv + EPS)
            k_o[:, sl] = ((kn[:, sl] * kg + kr_roped) * rk).astype(BF16)


def _c_in(x2d, g, wq, wkv, wkr, wkrs, gql, gkvl, wuq, wuqs, wuk, wuv, qg, qgs, kg, kgs,
          vone, ones_blk, ones_pair, cos_t, sin_t, tm):
    S = x2d.shape[0]
    row = lambda i: (i, 0)
    const = lambda i: (0, 0)
    full = lambda a: pl.BlockSpec(a.shape, const)
    W = C_HEADS * LANES
    return pl.pallas_call(
        _c_in_kernel,
        grid=(S // tm,),
        in_specs=[pl.BlockSpec((tm, D_MODEL), row), full(g), full(wq), full(wkv), full(wkr),
                  full(wkrs), full(gql), full(gkvl), full(wuq), full(wuqs), full(wuk),
                  full(wuv), full(qg), full(qgs), full(kg), full(kgs), full(vone),
                  full(ones_blk), full(ones_pair),
                  pl.BlockSpec((tm, LANES), row), pl.BlockSpec((tm, LANES), row)],
        out_specs=[pl.BlockSpec((tm, W), row), pl.BlockSpec((tm, W), row),
                   pl.BlockSpec((C_HEADS * V_ROWS, tm), lambda i: (0, i))],
        out_shape=(jax.ShapeDtypeStruct((S, W), BF16), jax.ShapeDtypeStruct((S, W), BF16),
                   jax.ShapeDtypeStruct((C_HEADS * V_ROWS, S), BF16)),
        compiler_params=_cparams(("arbitrary",)),
        name="c_in_proj",
    )(x2d, g, wq, wkv, wkr, wkrs, gql, gkvl, wuq, wuqs, wuk, wuv, qg, qgs, kg, kgs,
      vone, ones_blk, ones_pair, cos_t, sin_t)


V_ROWS = C_V + V_PAD
C_HPS = 4


def _c_attn_kernel(q_ref, k_ref, vt_ref, o_ref, *scratch):
    i = pl.program_id(1)
    streams = tuple(
        (q_ref[:, h * LANES:(h + 1) * LANES], slice(h * LANES, (h + 1) * LANES),
         slice(h * V_ROWS, (h + 1) * V_ROWS), scratch[2 * h], scratch[2 * h + 1],
         scratch[2 * C_HPS + h])
        for h in range(C_HPS))
    accs = _attn_pipeline(i, k_ref, vt_ref, streams, V_ROWS)
    for pair in range(C_HPS // 2):
        a, b = accs[2 * pair], accs[2 * pair + 1]
        out = jnp.concatenate([a[:C_V] / a[C_V:C_V + 1], b[:C_V] / b[C_V:C_V + 1]], axis=0)
        o_ref[:, pair * LANES:(pair + 1) * LANES] = out.T.astype(BF16)


def _c_attn(q, k, vt):
    S = q.shape[0]
    once = pl.Buffered(1)
    return pl.pallas_call(
        _c_attn_kernel,
        grid=(C_HEADS // C_HPS, S // ATT_TQ),
        in_specs=[
            pl.BlockSpec((ATT_TQ, C_HPS * LANES), lambda h, i: (i, h)),
            pl.BlockSpec((S, C_HPS * LANES), lambda h, i: (0, h), pipeline_mode=once),
            pl.BlockSpec((C_HPS * V_ROWS, S), lambda h, i: (h, 0), pipeline_mode=once),
        ],
        out_specs=pl.BlockSpec((ATT_TQ, C_HPS * C_V), lambda h, i: (i, h)),
        out_shape=jax.ShapeDtypeStruct((S, C_HEADS * C_V), BF16),
        scratch_shapes=[_SCORE_BUF] * (2 * C_HPS) + [_QT_BUF] * C_HPS,
        compiler_params=_cparams(("arbitrary", "arbitrary")),
        name="c_mla_attn",
    )(q, k, vt)


def _c_out_router_kernel(o_ref, x_ref, wo_ref, g_ref, rhi_ref, rlo_ref, tri_ref,
                         x3_o, h_o, meta_o, cnt_o, run_sc):
    i = pl.program_id(0)

    @pl.when(i == 0)
    def _():
        run_sc[...] = jnp.zeros_like(run_sc)

    x3 = x_ref[...] + _dot(o_ref[...], wo_ref[...])
    x3_o[...] = x3
    h = _rms(x3, g_ref[...])
    for c in range(ROW_TILES):
        h_o[pl.ds(c, h.shape[0], stride=ROW_TILES), :] = h[:, c * LANES:(c + 1) * LANES]
    h_hi = h.astype(BF16)
    h_lo = (h - h_hi.astype(F32)).astype(BF16)
    logits = (_dot(h_hi, rhi_ref[...]) + _dot(h_hi, rlo_ref[...])) + _dot(h_lo, rhi_ref[...])
    tm = logits.shape[0]
    lane = lax.broadcasted_iota(jnp.int32, (tm, LANES), 1)
    lanef = lane.astype(F32)
    logits = jnp.where(lane < N_EXPERTS, logits, -jnp.inf)
    v1 = jnp.max(logits, axis=1, keepdims=True)
    i1 = jnp.min(jnp.where(logits == v1, lanef, float(LANES)), axis=1, keepdims=True)
    m1 = lanef == i1
    rest = jnp.where(m1, -jnp.inf, logits)
    v2 = jnp.max(rest, axis=1, keepdims=True)
    i2 = jnp.min(jnp.where(rest == v2, lanef, float(LANES)), axis=1, keepdims=True)
    m2 = lanef == i2
    e = jnp.exp(v2 - v1)
    g1 = 1.0 / (1.0 + e)
    g2 = e / (1.0 + e)
    chosen = jnp.where(m1 | m2, 1.0, 0.0)
    before = _dot(tri_ref[...], chosen.astype(BF16)) + run_sc[0:1, :]
    p1 = jnp.sum(jnp.where(m1, before, 0.0), axis=1, keepdims=True)
    p2 = jnp.sum(jnp.where(m2, before, 0.0), axis=1, keepdims=True)
    run_sc[...] = run_sc[...] + jnp.sum(chosen, axis=0, keepdims=True)
    meta = jnp.where(lane == 0, i1, 0.0)
    meta = jnp.where(lane == 1, i2, meta)
    meta = jnp.where(lane == 2, g1, meta)
    meta = jnp.where(lane == 3, g2, meta)
    meta = jnp.where(lane == 4, p1, meta)
    meta = jnp.where(lane == 5, p2, meta)
    meta_o[...] = meta
    cnt_o[...] = run_sc[...]


def _c_out_router(o, x2d, wo, g, rhi, rlo, tri, tm):
    S = x2d.shape[0]
    row = lambda i: (i, 0)
    const = lambda i: (0, 0)
    return pl.pallas_call(
        _c_out_router_kernel,
        grid=(S // tm,),
        in_specs=[pl.BlockSpec((tm, D_MODEL), row), pl.BlockSpec((tm, D_MODEL), row),
                  pl.BlockSpec((D_MODEL, D_MODEL), const), pl.BlockSpec((1, D_MODEL), const),
                  pl.BlockSpec((D_MODEL, LANES), const), pl.BlockSpec((D_MODEL, LANES), const),
                  pl.BlockSpec((tm, tm), const)],
        out_specs=[pl.BlockSpec((tm, D_MODEL), row),
                   pl.BlockSpec((tm * ROW_TILES, LANES), row),
                   pl.BlockSpec((tm, LANES), row), pl.BlockSpec((8, LANES), const)],
        out_shape=(jax.ShapeDtypeStruct((S, D_MODEL), F32),
                   jax.ShapeDtypeStruct((S * ROW_TILES, LANES), F32),
                   jax.ShapeDtypeStruct((S, LANES), F32), jax.ShapeDtypeStruct((8, LANES), F32)),
        scratch_shapes=[pltpu.VMEM((8, LANES), F32)],
        compiler_params=_cparams(("arbitrary",)),
        name="c_out_router",
    )(o, x2d, wo, g, rhi, rlo, tri)


MOE_TM = 256
PAD_PIECES = (128, 64, 32, 16, 8, 4, 2, 1)


def _dispatch_kernel(pad_ref, slot_ref, h_ref, xs_ref, zero_sc, sem, zsem, *, ts):
    i = pl.program_id(0)

    @pl.when(i == 0)
    def _():
        zero_sc[...] = jnp.zeros_like(zero_sc)
        for e in range(N_EXPERTS):
            start = pad_ref[0, e]
            npad = pad_ref[1, e]
            for p in PAD_PIECES:
                hit = (npad & p) != 0

                @pl.when(hit)
                def _(start=start, p=p):
                    cp = pltpu.make_async_copy(zero_sc.at[pl.ds(0, p)],
                                               xs_ref.at[pl.ds(start, p)], zsem)
                    cp.start()
                    cp.wait()

                start = start + jnp.where(hit, p, 0)

        half = MOE_TM // 2

        def zero_tile(tile, c):
            for part in range(2):
                cp = pltpu.make_async_copy(
                    zero_sc, xs_ref.at[pl.ds(tile * MOE_TM + part * half, half)], zsem)
                cp.start()
                cp.wait()
            return c

        lax.fori_loop(pad_ref[2, 0], xs_ref.shape[0] // MOE_TM, zero_tile, 0)

    def row_copy(r, k):
        return pltpu.make_async_copy(h_ref.at[r], xs_ref.at[slot_ref[0, 0, 2 * r + k]], sem)

    def issue(r, c):
        row_copy(r, 0).start(priority=0)
        row_copy(r, 1).start(priority=1)
        return c

    lax.fori_loop(0, ts, issue, 0, unroll=8)
    for _ in range(2):
        pltpu.make_async_copy(h_ref, xs_ref.at[pl.ds(0, ts)], sem).wait()


def _dispatch(pad_info, slots3, h2, n_slots, ts):
    S = h2.shape[0]
    return pl.pallas_call(
        functools.partial(_dispatch_kernel, ts=ts),
        grid=(S // ts,),
        in_specs=[pl.BlockSpec(memory_space=pltpu.SMEM),
                  pl.BlockSpec((1, 1, 2 * ts), lambda i: (i, 0, 0), memory_space=pltpu.SMEM),
                  pl.BlockSpec((ts, ROW_TILES, LANES), lambda i: (i, 0, 0))],
        out_specs=pl.BlockSpec(memory_space=pl.ANY),
        out_shape=jax.ShapeDtypeStruct((n_slots, ROW_TILES, LANES), F32),
        scratch_shapes=[pltpu.VMEM((MOE_TM // 2, ROW_TILES, LANES), F32),
                        pltpu.SemaphoreType.DMA(()), pltpu.SemaphoreType.DMA(())],
        compiler_params=_cparams(("arbitrary",)),
        name="moe_dispatch",
    )(pad_info, slots3, h2)


MOE_FC = 512


N_FC = D_FF_EXPERT // MOE_FC
MOE_WR = 64


def _moe_kernel(te_ref, tv_ref, xs_ref, wgu_ref, wdn_ref, y_ref, x_sc, act_sc, wg_sc, wu_sc,
                wd_sc, stage_in, stage_out, sem):
    t = pl.program_id(0)
    e = te_ref[t]
    valid = tv_ref[t] != 0

    @pl.when(valid & ((t == 0) | (e != te_ref[jnp.maximum(t - 1, 0)])))
    def _():
        n_in = D_MODEL // MOE_WR

        def copy(j):
            if j < n_in:
                src, stage = wgu_ref.at[e, pl.ds(j * MOE_WR, MOE_WR), :], stage_in
            else:
                src, stage = wdn_ref.at[e, pl.ds((j - n_in) * MOE_FC, MOE_FC), :], stage_out
            return pltpu.make_async_copy(src, stage.at[j % 2], sem.at[j % 2])

        copy(0).start()
        for j in range(n_in + N_FC):
            if j + 1 < n_in + N_FC:
                copy(j + 1).start()
            copy(j).wait()
            if j < n_in:
                rows = pl.ds(j * MOE_WR, MOE_WR)
                wg_sc[rows, :] = stage_in[j % 2, :, :D_FF_EXPERT].astype(BF16)
                wu_sc[rows, :] = stage_in[j % 2, :, D_FF_EXPERT:].astype(BF16)
            else:
                rows = pl.ds((j - n_in) * MOE_FC, MOE_FC)
                wd_sc[rows, :] = stage_out[j % 2].astype(BF16)

    @pl.when(jnp.logical_not(valid))
    def _():
        y_ref[...] = jnp.zeros_like(y_ref)

    @pl.when(valid)
    def _():
        for c in range(ROW_TILES):
            chunk = xs_ref[pl.ds(c, MOE_TM, stride=ROW_TILES), :]
            x_sc[:, c * LANES:(c + 1) * LANES] = chunk.astype(BF16)
        x = x_sc[...]
        for c in range(N_FC):
            sl = slice(c * MOE_FC, (c + 1) * MOE_FC)
            gate = _dot(x, wg_sc[:, sl])
            up = _dot(x, wu_sc[:, sl])
            act_sc[:, sl] = (gate * jax.nn.sigmoid(gate) * up).astype(BF16)
        acc = _dot(act_sc[...], wd_sc[...])
        for c in range(ROW_TILES):
            y_ref[pl.ds(c, MOE_TM, stride=ROW_TILES), :] = acc[:, c * LANES:(c + 1) * LANES]


def _moe(tile_expert, tile_valid, xs, w_gu, w_down):
    n_slots = xs.shape[0]
    nt = n_slots // MOE_TM
    grid_spec = pltpu.PrefetchScalarGridSpec(
        num_scalar_prefetch=2,
        grid=(nt,),
        in_specs=[
            pl.BlockSpec((MOE_TM * ROW_TILES, LANES), lambda t, te, tv: (t, 0)),
            pl.BlockSpec(memory_space=pl.ANY),
            pl.BlockSpec(memory_space=pl.ANY),
        ],
        out_specs=pl.BlockSpec((MOE_TM * ROW_TILES, LANES), lambda t, te, tv: (t, 0)),
        scratch_shapes=[pltpu.VMEM((MOE_TM, D_MODEL), BF16),
                        pltpu.VMEM((MOE_TM, D_FF_EXPERT), BF16),
                        pltpu.VMEM((D_MODEL, D_FF_EXPERT), BF16),
                        pltpu.VMEM((D_MODEL, D_FF_EXPERT), BF16),
                        pltpu.VMEM((D_FF_EXPERT, D_MODEL), BF16),
                        pltpu.VMEM((2, MOE_WR, 2 * D_FF_EXPERT), F32),
                        pltpu.VMEM((2, MOE_FC, D_MODEL), F32),
                        pltpu.SemaphoreType.DMA((2,))],
    )
    y = pl.pallas_call(
        _moe_kernel,
        grid_spec=grid_spec,
        out_shape=jax.ShapeDtypeStruct((n_slots * ROW_TILES, LANES), F32),
        compiler_params=_cparams(("arbitrary",)),
        name="moe_experts",
    )(tile_expert, tile_valid, xs.reshape(n_slots * ROW_TILES, LANES), w_gu, w_down)
    return y.reshape(n_slots, ROW_TILES, LANES)


def _combine_kernel(slot_ref, next_slot_ref, x3_ref, meta_ref, y_ref, y2_ref, o_ref, buf, sem,
                    *, ts):
    i = pl.program_id(0)
    n = pl.num_programs(0)
    cur = lax.rem(i, 2)

    def gather(slots, half):
        def issue(r, c):
            for k in range(2):
                dst = buf.at[half, k, pl.ds(pl.multiple_of(r * ROW_TILES, ROW_TILES), ROW_TILES)]
                pltpu.make_async_copy(y_ref.at[slots[0, 0, 2 * r + k]], dst,
                                      sem.at[half]).start(priority=k)
            return c

        lax.fori_loop(0, ts, issue, 0, unroll=8)

    @pl.when(i == 0)
    def _():
        gather(slot_ref, 0)

    @pl.when(i + 1 < n)
    def _():
        gather(next_slot_ref, 1 - cur)

    for k in range(2):
        pltpu.make_async_copy(y2_ref.at[pl.ds(0, ts * ROW_TILES)], buf.at[cur, k],
                              sem.at[cur]).wait()
    meta = meta_ref[...]
    g1 = jnp.broadcast_to(meta[:, 2:3], (ts, LANES))
    g2 = jnp.broadcast_to(meta[:, 3:4], (ts, LANES))
    for c in range(ROW_TILES):
        sl = slice(c * LANES, (c + 1) * LANES)
        rows = pl.ds(c, ts, stride=ROW_TILES)
        o_ref[:, sl] = x3_ref[:, sl] + g1 * buf[cur, 0, rows, :] + g2 * buf[cur, 1, rows, :]


def _combine(slots3, x3, meta, y, ts):
    S = x3.shape[0]
    n = S // ts
    return pl.pallas_call(
        functools.partial(_combine_kernel, ts=ts),
        grid=(n,),
        in_specs=[pl.BlockSpec((1, 1, 2 * ts), lambda i: (i, 0, 0), memory_space=pltpu.SMEM),
                  pl.BlockSpec((1, 1, 2 * ts), lambda i: (jnp.minimum(i + 1, n - 1), 0, 0),
                               memory_space=pltpu.SMEM),
                  pl.BlockSpec((ts, D_MODEL), lambda i: (i, 0)),
                  pl.BlockSpec((ts, LANES), lambda i: (i, 0)),
                  pl.BlockSpec(memory_space=pl.ANY), pl.BlockSpec(memory_space=pl.ANY)],
        out_specs=pl.BlockSpec((ts, D_MODEL), lambda i: (i, 0)),
        out_shape=jax.ShapeDtypeStruct((S, D_MODEL), F32),
        scratch_shapes=[pltpu.VMEM((2, 2, ts * ROW_TILES, LANES), F32),
                        pltpu.SemaphoreType.DMA((2,))],
        compiler_params=_cparams(("arbitrary",)),
        name="moe_combine",
    )(slots3, slots3, x3, meta, y, y.reshape(-1, LANES))


def _tile_lanes(v, reps):
    return jnp.tile(v, reps)[None, :].astype(F32)


def _rope_cos_sin(S, dim):
    pos = np.arange(S, dtype=np.float64)
    inv = ROPE_THETA ** (-np.arange(0, dim, 2, dtype=np.float64) / dim)
    ang = pos[:, None] * inv[None, :]
    return np.cos(ang), np.sin(ang)


def _rope_tables_64(S):
    c, s = _rope_cos_sin(S, A_HEAD_DIM)
    return (jnp.asarray(np.concatenate([c, c, c, c], axis=1), F32),
            jnp.asarray(np.concatenate([-s, s, -s, s], axis=1), F32))


def _rope_tables_32(S):
    c, s = _rope_cos_sin(S, C_ROPE)
    one = np.ones((S, C_NOPE))
    zero = np.zeros((S, C_NOPE))
    pad = np.zeros((S, LANES - C_QK))
    return (jnp.asarray(np.concatenate([one, c, c, pad], axis=1), F32),
            jnp.asarray(np.concatenate([zero, -s, s, pad], axis=1), F32))


def _pad_heads(w, n_heads, width):
    k = w.shape[0]
    w = w.reshape(k, n_heads, width)
    return jnp.pad(w, ((0, 0), (0, 0), (0, LANES - width))).reshape(k, n_heads * LANES)


def _swap_rope_cols(w_p):
    k = w_p.shape[0]
    w = w_p.reshape(k, -1, LANES)
    half = C_ROPE // 2
    sw = jnp.concatenate([jnp.zeros_like(w[:, :, :C_NOPE]),
                          w[:, :, C_NOPE + half:C_QK], w[:, :, C_NOPE:C_NOPE + half],
                          jnp.zeros_like(w[:, :, C_QK:])], axis=2)
    return sw.reshape(k, -1)


def kernel(x, l0_norm_attn, l0_w_in, l0_a_q_norm, l0_a_k_norm, l0_a_lambda, l0_a_subln, l0_b_q_norm, l0_b_k_norm, l0_b_sinks, l0_w_out, l0_norm_ffn, l0_ffn_w_gu, l0_ffn_w_down, l1_norm_attn, l1_c_w_in, l1_c_q_lora_norm, l1_c_kv_lora_norm, l1_c_w_uq, l1_c_w_ukv, l1_c_q_norm, l1_c_k_norm, l1_c_w_out, l1_norm_ffn, l1_router, l1_exp_w_gu, l1_exp_w_down):
    B, S, _ = x.shape
    assert B == 1
    x2d = x.reshape(S, D_MODEL)
    tm = min(512, S)

    o_bk = 3 * A_W + B_QW
    o_bv = o_bk + B_KW
    bk_w = l0_w_in[:, o_bk:o_bv].reshape(D_MODEL, B_KV_HEADS, B_HEAD_DIM)
    bv_w = l0_w_in[:, o_bv:].reshape(D_MODEL, B_KV_HEADS, B_HEAD_DIM)
    dup = lambda w: jnp.concatenate([w, w], axis=2).reshape(D_MODEL, 2 * B_KW)
    w_in_p = jnp.concatenate([l0_w_in[:, :2 * A_W], l0_w_in[:, 3 * A_W:o_bk], dup(bk_w),
                              dup(bv_w)], axis=1).astype(BF16)
    w_avt = l0_w_in[:, 2 * A_W:3 * A_W].T.reshape(A_HEADS, 2 * A_HEAD_DIM, D_MODEL)
    w_avt = jnp.pad(w_avt, ((0, 0), (0, V_PAD), (0, 0))).reshape(-1, D_MODEL).astype(BF16)
    a_vone = jnp.zeros((A_HEADS, A_V_ROWS, 1), F32).at[:, 2 * A_HEAD_DIM, 0].set(1.0).reshape(-1, 1)
    gains = jnp.concatenate([_tile_lanes(l0_a_q_norm, 2), _tile_lanes(l0_a_k_norm, 2),
                             _tile_lanes(l0_b_q_norm, 2), _tile_lanes(l0_b_k_norm, 2),
                             jnp.zeros((4, LANES), F32)], axis=0)
    lane = jnp.arange(LANES)
    ones64 = (lane[:, None] // 64 == lane[None, :] // 64).astype(BF16)
    cos64, sin64 = _rope_tables_64(S)
    aq, ak, avt, bq, bk, bv = _l0_in(x2d, l0_norm_attn[None, :], w_in_p, w_avt, a_vone, gains,
                                     ones64, cos64, sin64, tm)
    lam_init = 0.8 - 0.6 * math.exp(-0.3 * 0)
    a_out = _a_attn(l0_a_lambda.astype(F32), aq, ak, avt, l0_a_subln[None, :].astype(F32),
                    lam_init)
    b_out = _b_attn(l0_b_sinks.astype(F32), bq, bk, bv)
    w_out = l0_w_out.astype(BF16)
    x2 = _l0_ffn(a_out, b_out, x2d, w_out[:A_W], w_out[A_W:], l0_norm_ffn[None, :],
                 l0_ffn_w_gu.astype(BF16), l0_ffn_w_down.astype(BF16), tm, D_FF // 2)

    wq = l1_c_w_in[:, :C_Q_RANK].astype(BF16)
    wkv = l1_c_w_in[:, C_Q_RANK:C_Q_RANK + C_KV_RANK].astype(BF16)
    wkr = jnp.pad(l1_c_w_in[:, C_Q_RANK + C_KV_RANK:], ((0, 0), (C_NOPE, LANES - C_QK)))
    wkrs = _swap_rope_cols(wkr).astype(BF16)
    wkr = wkr.astype(BF16)
    wuq = _pad_heads(l1_c_w_uq, C_HEADS, C_QK)
    wuqs = _swap_rope_cols(wuq).astype(BF16)
    wuq = wuq.astype(BF16)
    ukv = l1_c_w_ukv.reshape(C_KV_RANK, C_HEADS, C_NOPE + C_V)
    wuk = _pad_heads(ukv[:, :, :C_NOPE].reshape(C_KV_RANK, -1), C_HEADS, C_NOPE).astype(BF16)
    wuv = jnp.pad(jnp.transpose(ukv[:, :, C_NOPE:], (1, 2, 0)), ((0, 0), (0, V_PAD), (0, 0)))
    wuv = wuv.reshape(C_HEADS * V_ROWS, C_KV_RANK).astype(BF16)
    vone = jnp.zeros((C_HEADS, V_ROWS, 1), F32).at[:, C_V, 0].set(1.0).reshape(-1, 1)
    pad_gain = lambda gvec: jnp.pad(gvec.astype(F32), (0, LANES - C_QK))[None, :]
    qg, kg = pad_gain(l1_c_q_norm), pad_gain(l1_c_k_norm)
    qgs, kgs = _swap_rope_cols(qg), _swap_rope_cols(kg)
    ones128 = jnp.ones((LANES, LANES), BF16)
    lane2 = jnp.arange(2 * LANES)
    ones_pair = (lane2[:, None] // LANES == lane2[None, :] // LANES).astype(BF16)
    cos32, sin32 = _rope_tables_32(S)
    cq, ck, cvt = _c_in(x2, l1_norm_attn[None, :], wq, wkv, wkr, wkrs,
                       l1_c_q_lora_norm[None, :], l1_c_kv_lora_norm[None, :],
                       wuq, wuqs, wuk, wuv, qg, qgs, kg, kgs, vone, ones128, ones_pair, cos32,
                       sin32, tm)
    c_o = _c_attn(cq, ck, cvt)

    r_pad = jnp.pad(l1_router.astype(F32), ((0, 0), (0, LANES - N_EXPERTS)))
    r_hi = r_pad.astype(BF16)
    r_lo = (r_pad - r_hi.astype(F32)).astype(BF16)
    tr = min(256, S)
    ridx = jnp.arange(tr)
    tri = (ridx[None, :] < ridx[:, None]).astype(BF16)
    x3, h2, meta, cnt = _c_out_router(c_o, x2, l1_c_w_out.astype(BF16), l1_norm_ffn[None, :],
                                      r_hi, r_lo, tri, tr)

    counts = cnt[0, :N_EXPERTS].astype(jnp.int32)
    padded = ((counts + MOE_TM - 1) // MOE_TM) * MOE_TM
    ends = jnp.cumsum(padded)
    offs = ends - padded
    idx = meta[:, 0:2].astype(jnp.int32)
    pos = meta[:, 4:6].astype(jnp.int32)
    slots = offs[idx] + pos
    n_slots = 2 * S + N_EXPERTS * MOE_TM
    nt = n_slots // MOE_TM
    tile_start = jnp.arange(nt, dtype=jnp.int32) * MOE_TM
    n_valid = jnp.broadcast_to(ends[-1] // MOE_TM, (N_EXPERTS,))
    tile_valid = (tile_start < ends[-1]).astype(jnp.int32)
    tile_expert = jnp.minimum(
        jnp.sum((tile_start[:, None] >= ends[None, :]).astype(jnp.int32), axis=1),
        N_EXPERTS - 1).astype(jnp.int32)
    pad_info = jnp.stack([offs + counts, padded - counts, n_valid]).astype(jnp.int32)

    ts = min(512, S)
    slots3 = slots.reshape(S // ts, 1, 2 * ts)
    xs = _dispatch(pad_info, slots3, h2.reshape(S, ROW_TILES, LANES), n_slots, ts)
    y = _moe(tile_expert, tile_valid, xs,
             l1_exp_w_gu.astype(F32), l1_exp_w_down.astype(F32))
    out = _combine(slots3, x3, meta, y, ts)
    return out.reshape(B, S, D_MODEL)
```

```python
import functools
import math

import jax
import jax.numpy as jnp
import numpy as np
from jax import lax
from jax.experimental import pallas as pl
from jax.experimental.pallas import tpu as pltpu

F32 = jnp.float32
BF16 = jnp.bfloat16

D_MODEL = 1024
CHUNK = 64
ROPE_THETA = 10000.0
EPS = 1e-6
LANES = 128
ROW_TILES = D_MODEL // LANES

A_HEADS = 4
A_HEAD_DIM = 64
B_HEADS = 8
B_KV_HEADS = 2
B_HEAD_DIM = 64
A_W = A_HEADS * 2 * A_HEAD_DIM
B_QW = B_HEADS * B_HEAD_DIM
B_KW = B_KV_HEADS * B_HEAD_DIM

C_HEADS = 16
C_Q_RANK = 256
C_KV_RANK = 128
C_NOPE = 64
C_ROPE = 32
C_V = 64
C_QK = C_NOPE + C_ROPE

D_FF = 2816
N_EXPERTS = 8
D_FF_EXPERT = 3584

NEG_BIG = -1e30
LOG2E = 1.0 / math.log(2.0)
VMEM_LIMIT = 56 * 1024 * 1024


def _cparams(sem):
    return pltpu.CompilerParams(dimension_semantics=sem, vmem_limit_bytes=VMEM_LIMIT)


def _dot(a, b):
    return jnp.dot(a, b, preferred_element_type=F32)


def _dot_nt(a, b):
    return lax.dot_general(a, b, (((1,), (1,)), ((), ())), preferred_element_type=F32)


def _rms(x, g):
    ms = jnp.mean(x * x, axis=-1, keepdims=True)
    return x * lax.rsqrt(ms + EPS) * g


def _l0_in_kernel(x_ref, g_ref, w_ref, wvt_ref, vone_ref, gains_ref, ones_ref, cos_ref, sin_ref,
                  aq_o, ak_o, avt_o, bq_o, bk_o, bv_o):
    h = _rms(x_ref[...], g_ref[...]).astype(BF16)
    z = _dot(h, w_ref[...])
    avt_o[...] = (_dot_nt(wvt_ref[...], h) + vone_ref[...]).astype(BF16)
    cos = cos_ref[...]
    sin = sin_ref[...]
    ones_blk = ones_ref[...]
    lane = lax.broadcasted_iota(jnp.int32, (1, LANES), 1)
    first_half = (lane % 64) < 32

    def norm_rope(zs, gain, scale):
        ss = _dot((zs * zs).astype(BF16), ones_blk)
        r = lax.rsqrt(ss * (1.0 / 64.0) + EPS) * scale
        y = zs * gain
        sw = jnp.where(first_half, pltpu.roll(y, 96, 1), pltpu.roll(y, 32, 1))
        return ((y * cos + sw * sin) * r).astype(BF16)

    col = 0
    gcol = 0
    scale = A_HEAD_DIM ** -0.5
    scales = {"aq": scale * LOG2E, "ak": 1.0, "bq": scale, "bk": 1.0}
    outs = {"aq": aq_o, "ak": ak_o, "bq": bq_o, "bk": bk_o}
    for name, nslab in (("aq", 4), ("ak", 4), ("bq", 4), ("bk", 2), ("bv", 2)):
        if name == "bv":
            bv_o[...] = z[:, col:col + 2 * LANES].astype(BF16)
            col += 2 * LANES
            continue
        sc = scales[name]
        gain = gains_ref[gcol:gcol + 1, :]
        gcol += 1
        for s in range(nslab):
            zs = z[:, col:col + LANES]
            outs[name][:, s * LANES:(s + 1) * LANES] = norm_rope(zs, gain, sc)
            col += LANES


def _l0_in(x2d, g, w_in_p, w_avt, vone, gains, ones_blk, cos_t, sin_t, tm):
    S = x2d.shape[0]
    nw = w_in_p.shape[1]
    row = lambda i: (i, 0)
    const = lambda i: (0, 0)
    out_shapes = (
        jax.ShapeDtypeStruct((S, A_W), BF16), jax.ShapeDtypeStruct((S, A_W), BF16),
        jax.ShapeDtypeStruct((A_HEADS * A_V_ROWS, S), BF16), jax.ShapeDtypeStruct((S, B_QW), BF16),
        jax.ShapeDtypeStruct((S, 2 * B_KW), BF16), jax.ShapeDtypeStruct((S, 2 * B_KW), BF16))
    return pl.pallas_call(
        _l0_in_kernel,
        grid=(S // tm,),
        in_specs=[
            pl.BlockSpec((tm, D_MODEL), row),
            pl.BlockSpec((1, D_MODEL), const),
            pl.BlockSpec((D_MODEL, nw), const),
            pl.BlockSpec((A_HEADS * A_V_ROWS, D_MODEL), const),
            pl.BlockSpec((A_HEADS * A_V_ROWS, 1), const),
            pl.BlockSpec((8, LANES), const),
            pl.BlockSpec((LANES, LANES), const),
            pl.BlockSpec((tm, LANES), row),
            pl.BlockSpec((tm, LANES), row),
        ],
        out_specs=[
            pl.BlockSpec((tm, A_W), row), pl.BlockSpec((tm, A_W), row),
            pl.BlockSpec((A_HEADS * A_V_ROWS, tm), lambda i: (0, i)), pl.BlockSpec((tm, B_QW), row),
            pl.BlockSpec((tm, 2 * B_KW), row), pl.BlockSpec((tm, 2 * B_KW), row)],
        out_shape=out_shapes,
        compiler_params=_cparams(("arbitrary",)),
        name="l0_in_proj",
    )(x2d, g, w_in_p, w_avt, vone, gains, ones_blk, cos_t, sin_t)


ATT_TQ = 512
ATT_TK = 256
V_PAD = 16


def _softmax_pv(s_ref, s_max, vt, m, acc):
    m_new = jnp.maximum(m, s_max)
    alpha = jnp.exp2(m - m_new)
    p = jnp.exp2(s_ref[...] - m_new).astype(BF16)
    return m_new, alpha * acc + _dot(vt, p)


def _diag_masks():
    key_chunk = lax.broadcasted_iota(jnp.int32, (ATT_TK, ATT_TQ), 0) // CHUNK
    query_chunk = lax.broadcasted_iota(jnp.int32, (ATT_TK, ATT_TQ), 1) // CHUNK
    return [key_chunk + b * (ATT_TK // CHUNK) <= query_chunk for b in range(ATT_TQ // ATT_TK)]


def _attn_pipeline(i, k_ref, vt_ref, streams, v_rows):
    n = len(streams)

    for st in streams:
        st[5][...] = st[0].T

    def scores(st, blk, mask, dst):
        lanes, qt_ref = streams[st][1], streams[st][5]
        off = pl.multiple_of(blk * ATT_TK, ATT_TK)
        s = _dot(k_ref[pl.ds(off, ATT_TK), lanes], qt_ref[...])
        if mask is not None:
            s = jnp.where(mask, s, NEG_BIG)
        dst[...] = s
        return jnp.max(s, axis=0, keepdims=True)

    def consume(st, blk, src, s_max, state):
        off = pl.multiple_of(blk * ATT_TK, ATT_TK)
        return _softmax_pv(src, s_max, vt_ref[streams[st][2], pl.ds(off, ATT_TK)], *state)

    buf_a = [st[3] for st in streams]
    buf_b = [st[4] for st in streams]
    mask0, mask1 = _diag_masks()
    d0 = 2 * i
    d1 = d0 + 1
    state = [(jnp.full((1, ATT_TQ), NEG_BIG, F32), jnp.zeros((v_rows, ATT_TQ), F32))] * n
    max_a = [scores(st, d0, mask0, buf_a[st]) for st in range(n)]
    max_b = [None] * n
    for st in range(n):
        max_b[st] = scores(st, d1, mask1, buf_b[st])
        state[st] = consume(st, d0, buf_a[st], max_a[st], state[st])
    for st in range(n):
        max_a[st] = scores(st, 0, None, buf_a[st])
        state[st] = consume(st, d1, buf_b[st], max_b[st], state[st])

    def body(p, carry):
        carry, max_a = list(carry[0]), list(carry[1])
        max_b = [None] * n
        u0 = 2 * p
        u1 = u0 + 1
        nxt = jnp.minimum(u0 + 2, d0 - 1)
        for st in range(n):
            max_b[st] = scores(st, u1, None, buf_b[st])
            carry[st] = consume(st, u0, buf_a[st], max_a[st], carry[st])
        for st in range(n):
            max_a[st] = scores(st, nxt, None, buf_a[st])
            carry[st] = consume(st, u1, buf_b[st], max_b[st], carry[st])
        return tuple(carry), tuple(max_a)

    final, _ = lax.fori_loop(0, i, body, (tuple(state), tuple(max_a)))
    return [acc for _, acc in final]


_SCORE_BUF = pltpu.VMEM((ATT_TK, ATT_TQ), F32)
_QT_BUF = pltpu.VMEM((LANES, ATT_TQ), BF16)


A_V_ROWS = 2 * A_HEAD_DIM + V_PAD


A_HPS = 2


def _a_attn_kernel(lam_ref, q_ref, k_ref, vt_ref, subln_ref, o_ref, *scratch, lam_init):
    i = pl.program_id(1)
    lane = lax.broadcasted_iota(jnp.int32, (1, LANES), 1)
    streams = []
    for h in range(A_HPS):
        q = q_ref[:, h * LANES:(h + 1) * LANES]
        zero = jnp.zeros_like(q)
        lanes = slice(h * LANES, (h + 1) * LANES)
        rows = slice(h * A_V_ROWS, (h + 1) * A_V_ROWS)
        for mp, sel in enumerate((lane < 64, lane >= 64)):
            st = 2 * h + mp
            streams.append((jnp.where(sel, q, zero), lanes, rows, scratch[2 * st],
                            scratch[2 * st + 1], scratch[4 * A_HPS + st]))
    accs = _attn_pipeline(i, k_ref, vt_ref, tuple(streams), A_V_ROWS)
    dv = 2 * A_HEAD_DIM
    lf = lam_ref[...]
    lam = (jnp.exp(jnp.sum(lf[0:1] * lf[1:2], axis=1, keepdims=True))
           - jnp.exp(jnp.sum(lf[2:3] * lf[3:4], axis=1, keepdims=True)) + lam_init)
    for h in range(A_HPS):
        acc1, acc2 = accs[2 * h], accs[2 * h + 1]
        out = (acc1[:dv] / acc1[dv:dv + 1] - lam * (acc2[:dv] / acc2[dv:dv + 1])).T
        out = _rms(out, subln_ref[...]) * (1.0 - lam_init)
        o_ref[:, h * LANES:(h + 1) * LANES] = out.astype(BF16)


def _a_attn(lam_p, aq, ak, avt, subln, lam_init):
    S = aq.shape[0]
    once = pl.Buffered(1)
    return pl.pallas_call(
        functools.partial(_a_attn_kernel, lam_init=lam_init),
        grid=(A_HEADS // A_HPS, S // ATT_TQ),
        in_specs=[
            pl.BlockSpec((4, A_HEAD_DIM), lambda h, i: (0, 0)),
            pl.BlockSpec((ATT_TQ, A_HPS * LANES), lambda h, i: (i, h)),
            pl.BlockSpec((S, A_HPS * LANES), lambda h, i: (0, h), pipeline_mode=once),
            pl.BlockSpec((A_HPS * A_V_ROWS, S), lambda h, i: (h, 0), pipeline_mode=once),
            pl.BlockSpec((1, LANES), lambda h, i: (0, 0)),
        ],
        out_specs=pl.BlockSpec((ATT_TQ, A_HPS * LANES), lambda h, i: (i, h)),
        out_shape=jax.ShapeDtypeStruct((S, A_W), BF16),
        scratch_shapes=[_SCORE_BUF] * (4 * A_HPS) + [_QT_BUF] * (2 * A_HPS),
        compiler_params=_cparams(("arbitrary", "arbitrary")),
        name="a_diff_attn",
    )(lam_p, aq, ak, avt, subln)


B_BLK = 256
B_BACK = 128


def _b_attn_kernel(sink_ref, q_ref, kp_ref, kc_ref, vp_ref, vc_ref, o_ref):
    i = pl.program_id(0)
    lane = lax.broadcasted_iota(jnp.int32, (1, LANES), 1)
    lo = lane < 64
    r = lax.broadcasted_iota(jnp.int32, (B_BLK, B_BACK + B_BLK), 0) // CHUNK
    c_idx = lax.broadcasted_iota(jnp.int32, (B_BLK, B_BACK + B_BLK), 1)
    c = c_idx // CHUNK
    mask = (c >= r) & (c <= r + 2) & ((c_idx >= B_BACK) | (i > 0))
    for slab in range(B_HEADS // 2):
        g = slab // 2
        k = jnp.concatenate([kp_ref[:, g * LANES:(g + 1) * LANES],
                             kc_ref[:, g * LANES:(g + 1) * LANES]], axis=0)
        v = jnp.concatenate([vp_ref[:, g * LANES:(g + 1) * LANES],
                             vc_ref[:, g * LANES:(g + 1) * LANES]], axis=0)
        qs = q_ref[:, slab * LANES:(slab + 1) * LANES]
        zq = jnp.zeros_like(qs)
        zv = jnp.zeros_like(v)
        out = jnp.zeros((B_BLK, LANES), F32)
        for half in range(2):
            sel = lo if half == 0 else jnp.logical_not(lo)
            sink = sink_ref[2 * slab + half]
            s = _dot_nt(jnp.where(sel, qs, zq), k)
            s = jnp.where(mask, s, NEG_BIG)
            m = jnp.maximum(jnp.max(s, axis=1, keepdims=True), sink)
            e = jnp.exp(s - m)
            denom = jnp.sum(e, axis=1, keepdims=True) + jnp.exp(sink - m)
            p = (e / denom).astype(BF16)
            out = out + _dot(p, jnp.where(sel, v, zv))
        o_ref[:, slab * LANES:(slab + 1) * LANES] = out.astype(BF16)


def _b_attn(sinks, bq, bk, bv):
    S = bq.shape[0]
    prev = lambda i: (jnp.maximum(i * (B_BLK // B_BACK) - 1, 0), 0)
    cur = lambda i: (i, 0)
    return pl.pallas_call(
        _b_attn_kernel,
        grid=(S // B_BLK,),
        in_specs=[
            pl.BlockSpec(memory_space=pltpu.SMEM),
            pl.BlockSpec((B_BLK, B_QW), cur),
            pl.BlockSpec((B_BACK, 2 * B_KW), prev),
            pl.BlockSpec((B_BLK, 2 * B_KW), cur),
            pl.BlockSpec((B_BACK, 2 * B_KW), prev),
            pl.BlockSpec((B_BLK, 2 * B_KW), cur),
        ],
        out_specs=pl.BlockSpec((B_BLK, B_QW), cur),
        out_shape=jax.ShapeDtypeStruct((S, B_QW), BF16),
        compiler_params=_cparams(("arbitrary",)),
        name="b_swa_attn",
    )(sinks, bq, bk, bk, bv, bv)


def _l0_ffn_kernel(a_ref, b_ref, x_ref, woa_ref, wob_ref, g_ref, wg_ref, wu_ref, wd_ref,
                   o_ref, h_sc, acc_sc):
    j = pl.program_id(1)

    @pl.when(j == 0)
    def _():
        x1 = x_ref[...] + _dot(a_ref[...], woa_ref[...]) + _dot(b_ref[...], wob_ref[...])
        acc_sc[...] = x1
        h_sc[...] = _rms(x1, g_ref[...]).astype(BF16)

    h = h_sc[...]
    gate = _dot(h, wg_ref[...])
    up = _dot(h, wu_ref[...])
    act = (gate * jax.nn.sigmoid(gate) * up).astype(BF16)
    acc_sc[...] += _dot(act, wd_ref[...])

    @pl.when(j == pl.num_programs(1) - 1)
    def _():
        o_ref[...] = acc_sc[...]


def _l0_ffn(a_out, b_out, x2d, wo_a, wo_b, g, w_gu, w_down, tm, tf):
    S = x2d.shape[0]
    nf = D_FF // tf
    row = lambda i, j: (i, 0)
    const = lambda i, j: (0, 0)
    return pl.pallas_call(
        _l0_ffn_kernel,
        grid=(S // tm, nf),
        in_specs=[
            pl.BlockSpec((tm, A_W), row),
            pl.BlockSpec((tm, B_QW), row),
            pl.BlockSpec((tm, D_MODEL), row),
            pl.BlockSpec((A_W, D_MODEL), const),
            pl.BlockSpec((B_QW, D_MODEL), const),
            pl.BlockSpec((1, D_MODEL), const),
            pl.BlockSpec((D_MODEL, tf), lambda i, j: (0, j)),
            pl.BlockSpec((D_MODEL, tf), lambda i, j: (0, nf + j)),
            pl.BlockSpec((tf, D_MODEL), lambda i, j: (j, 0)),
        ],
        out_specs=pl.BlockSpec((tm, D_MODEL), row),
        out_shape=jax.ShapeDtypeStruct((S, D_MODEL), F32),
        scratch_shapes=[pltpu.VMEM((tm, D_MODEL), BF16), pltpu.VMEM((tm, D_MODEL), F32)],
        compiler_params=_cparams(("arbitrary", "arbitrary")),
        name="l0_out_ffn",
    )(a_out, b_out, x2d, wo_a, wo_b, g, w_gu, w_gu, w_down)


def _c_in_kernel(x_ref, g_ref, wq_ref, wkv_ref, wkr_ref, wkrs_ref, gql_ref, gkvl_ref,
                 wuq_ref, wuqs_ref, wuk_ref, wuv_ref, qg_ref, qgs_ref, kg_ref, kgs_ref,
                 vone_ref, ones_ref, ones2_ref, cos_ref, sin_ref, q_o, k_o, vt_o):
    h = _rms(x_ref[...], g_ref[...]).astype(BF16)
    cq = _rms(_dot(h, wq_ref[...]), gql_ref[...]).astype(BF16)
    ckv = _rms(_dot(h, wkv_ref[...]), gkvl_ref[...]).astype(BF16)
    kr = _dot(h, wkr_ref[...])
    krs = _dot(h, wkrs_ref[...])
    q = _dot(cq, wuq_ref[...])
    qs = _dot(cq, wuqs_ref[...])
    kn = _dot(ckv, wuk_ref[...])
    vt_o[...] = (_dot_nt(wuv_ref[...], ckv) + vone_ref[...]).astype(BF16)

    cos = cos_ref[...]
    sin = sin_ref[...]
    ones_blk = ones_ref[...]
    qg, qgs, kg, kgs = qg_ref[...], qgs_ref[...], kg_ref[...], kgs_ref[...]
    ss_kr = _dot((kr * kr).astype(BF16), ones_blk)
    kr_roped = kr * kg * cos + krs * kgs * sin
    scale = C_QK ** -0.5 * LOG2E
    inv = 1.0 / C_QK
    ones_pair = ones2_ref[...]
    for pair in range(C_HEADS // 2):
        both = slice(2 * pair * LANES, (2 * pair + 2) * LANES)
        ss_q = _dot((q[:, both] * q[:, both]).astype(BF16), ones_pair)
        ss_k = _dot((kn[:, both] * kn[:, both]).astype(BF16), ones_pair)
        for half in range(2):
            sl = slice((2 * pair + half) * LANES, (2 * pair + half + 1) * LANES)
            hs = slice(half * LANES, (half + 1) * LANES)
            qh = q[:, sl]
            r = lax.rsqrt(ss_q[:, hs] * inv + EPS) * scale
            q_o[:, sl] = ((qh * qg * cos + qs[:, sl] * qgs * sin) * r).astype(BF16)
            rk = lax.rsqrt((ss_k[:, hs] + ss_kr) * inv + EPS)
            k_o[:, sl] = ((kn[:, sl] * kg + kr_roped) * rk).astype(BF16)


def _c_in(x2d, g, wq, wkv, wkr, wkrs, gql, gkvl, wuq, wuqs, wuk, wuv, qg, qgs, kg, kgs,
          vone, ones_blk, ones_pair, cos_t, sin_t, tm):
    S = x2d.shape[0]
    row = lambda i: (i, 0)
    const = lambda i: (0, 0)
    full = lambda a: pl.BlockSpec(a.shape, const)
    W = C_HEADS * LANES
    return pl.pallas_call(
        _c_in_kernel,
        grid=(S // tm,),
        in_specs=[pl.BlockSpec((tm, D_MODEL), row), full(g), full(wq), full(wkv), full(wkr),
                  full(wkrs), full(gql), full(gkvl), full(wuq), full(wuqs), full(wuk),
                  full(wuv), full(qg), full(qgs), full(kg), full(kgs), full(vone),
                  full(ones_blk), full(ones_pair),
                  pl.BlockSpec((tm, LANES), row), pl.BlockSpec((tm, LANES), row)],
        out_specs=[pl.BlockSpec((tm, W), row), pl.BlockSpec((tm, W), row),
                   pl.BlockSpec((C_HEADS * V_ROWS, tm), lambda i: (0, i))],
        out_shape=(jax.ShapeDtypeStruct((S, W), BF16), jax.ShapeDtypeStruct((S, W), BF16),
                   jax.ShapeDtypeStruct((C_HEADS * V_ROWS, S), BF16)),
        compiler_params=_cparams(("arbitrary",)),
        name="c_in_proj",
    )(x2d, g, wq, wkv, wkr, wkrs, gql, gkvl, wuq, wuqs, wuk, wuv, qg, qgs, kg, kgs,
      vone, ones_blk, ones_pair, cos_t, sin_t)


V_ROWS = C_V + V_PAD
C_HPS = 4


def _c_attn_kernel(q_ref, k_ref, vt_ref, o_ref, *scratch):
    i = pl.program_id(1)
    streams = tuple(
        (q_ref[:, h * LANES:(h + 1) * LANES], slice(h * LANES, (h + 1) * LANES),
         slice(h * V_ROWS, (h + 1) * V_ROWS), scratch[2 * h], scratch[2 * h + 1],
         scratch[2 * C_HPS + h])
        for h in range(C_HPS))
    accs = _attn_pipeline(i, k_ref, vt_ref, streams, V_ROWS)
    for pair in range(C_HPS // 2):
        a, b = accs[2 * pair], accs[2 * pair + 1]
        out = jnp.concatenate([a[:C_V] / a[C_V:C_V + 1], b[:C_V] / b[C_V:C_V + 1]], axis=0)
        o_ref[:, pair * LANES:(pair + 1) * LANES] = out.T.astype(BF16)


def _c_attn(q, k, vt):
    S = q.shape[0]
    once = pl.Buffered(1)
    return pl.pallas_call(
        _c_attn_kernel,
        grid=(C_HEADS // C_HPS, S // ATT_TQ),
        in_specs=[
            pl.BlockSpec((ATT_TQ, C_HPS * LANES), lambda h, i: (i, h)),
            pl.BlockSpec((S, C_HPS * LANES), lambda h, i: (0, h), pipeline_mode=once),
            pl.BlockSpec((C_HPS * V_ROWS, S), lambda h, i: (h, 0), pipeline_mode=once),
        ],
        out_specs=pl.BlockSpec((ATT_TQ, C_HPS * C_V), lambda h, i: (i, h)),
        out_shape=jax.ShapeDtypeStruct((S, C_HEADS * C_V), BF16),
        scratch_shapes=[_SCORE_BUF] * (2 * C_HPS) + [_QT_BUF] * C_HPS,
        compiler_params=_cparams(("arbitrary", "arbitrary")),
        name="c_mla_attn",
    )(q, k, vt)


def _c_out_router_kernel(o_ref, x_ref, wo_ref, g_ref, rhi_ref, rlo_ref, tri_ref,
                         x3_o, h_o, meta_o, cnt_o, run_sc):
    i = pl.program_id(0)

    @pl.when(i == 0)
    def _():
        run_sc[...] = jnp.zeros_like(run_sc)

    x3 = x_ref[...] + _dot(o_ref[...], wo_ref[...])
    x3_o[...] = x3
    h = _rms(x3, g_ref[...])
    for c in range(ROW_TILES):
        h_o[pl.ds(c, h.shape[0], stride=ROW_TILES), :] = h[:, c * LANES:(c + 1) * LANES]
    h_hi = h.astype(BF16)
    h_lo = (h - h_hi.astype(F32)).astype(BF16)
    logits = (_dot(h_hi, rhi_ref[...]) + _dot(h_hi, rlo_ref[...])) + _dot(h_lo, rhi_ref[...])
    tm = logits.shape[0]
    lane = lax.broadcasted_iota(jnp.int32, (tm, LANES), 1)
    lanef = lane.astype(F32)
    logits = jnp.where(lane < N_EXPERTS, logits, -jnp.inf)
    v1 = jnp.max(logits, axis=1, keepdims=True)
    i1 = jnp.min(jnp.where(logits == v1, lanef, float(LANES)), axis=1, keepdims=True)
    m1 = lanef == i1
    rest = jnp.where(m1, -jnp.inf, logits)
    v2 = jnp.max(rest, axis=1, keepdims=True)
    i2 = jnp.min(jnp.where(rest == v2, lanef, float(LANES)), axis=1, keepdims=True)
    m2 = lanef == i2
    e = jnp.exp(v2 - v1)
    g1 = 1.0 / (1.0 + e)
    g2 = e / (1.0 + e)
    chosen = jnp.where(m1 | m2, 1.0, 0.0)
    before = _dot(tri_ref[...], chosen.astype(BF16)) + run_sc[0:1, :]
    p1 = jnp.sum(jnp.where(m1, before, 0.0), axis=1, keepdims=True)
    p2 = jnp.sum(jnp.where(m2, before, 0.0), axis=1, keepdims=True)
    run_sc[...] = run_sc[...] + jnp.sum(chosen, axis=0, keepdims=True)
    meta = jnp.where(lane == 0, i1, 0.0)
    meta = jnp.where(lane == 1, i2, meta)
    meta = jnp.where(lane == 2, g1, meta)
    meta = jnp.where(lane == 3, g2, meta)
    meta = jnp.where(lane == 4, p1, meta)
    meta = jnp.where(lane == 5, p2, meta)
    meta_o[...] = meta
    cnt_o[...] = run_sc[...]


def _c_out_router(o, x2d, wo, g, rhi, rlo, tri, tm):
    S = x2d.shape[0]
    row = lambda i: (i, 0)
    const = lambda i: (0, 0)
    return pl.pallas_call(
        _c_out_router_kernel,
        grid=(S // tm,),
        in_specs=[pl.BlockSpec((tm, D_MODEL), row), pl.BlockSpec((tm, D_MODEL), row),
                  pl.BlockSpec((D_MODEL, D_MODEL), const), pl.BlockSpec((1, D_MODEL), const),
                  pl.BlockSpec((D_MODEL, LANES), const), pl.BlockSpec((D_MODEL, LANES), const),
                  pl.BlockSpec((tm, tm), const)],
        out_specs=[pl.BlockSpec((tm, D_MODEL), row),
                   pl.BlockSpec((tm * ROW_TILES, LANES), row),
                   pl.BlockSpec((tm, LANES), row), pl.BlockSpec((8, LANES), const)],
        out_shape=(jax.ShapeDtypeStruct((S, D_MODEL), F32),
                   jax.ShapeDtypeStruct((S * ROW_TILES, LANES), F32),
                   jax.ShapeDtypeStruct((S, LANES), F32), jax.ShapeDtypeStruct((8, LANES), F32)),
        scratch_shapes=[pltpu.VMEM((8, LANES), F32)],
        compiler_params=_cparams(("arbitrary",)),
        name="c_out_router",
    )(o, x2d, wo, g, rhi, rlo, tri)


MOE_TM = 256
PAD_PIECES = (128, 64, 32, 16, 8, 4, 2, 1)


def _dispatch_kernel(pad_ref, slot_ref, h_ref, xs_ref, zero_sc, sem, zsem, *, ts):
    i = pl.program_id(0)

    @pl.when(i == 0)
    def _():
        zero_sc[...] = jnp.zeros_like(zero_sc)
        for e in range(N_EXPERTS):
            start = pad_ref[0, e]
            npad = pad_ref[1, e]
            for p in PAD_PIECES:
                hit = (npad & p) != 0

                @pl.when(hit)
                def _(start=start, p=p):
                    cp = pltpu.make_async_copy(zero_sc.at[pl.ds(0, p)],
                                               xs_ref.at[pl.ds(start, p)], zsem)
                    cp.start()
                    cp.wait()

                start = start + jnp.where(hit, p, 0)

        half = MOE_TM // 2

        def zero_tile(tile, c):
            for part in range(2):
                cp = pltpu.make_async_copy(
                    zero_sc, xs_ref.at[pl.ds(tile * MOE_TM + part * half, half)], zsem)
                cp.start()
                cp.wait()
            return c

        lax.fori_loop(pad_ref[2, 0], xs_ref.shape[0] // MOE_TM, zero_tile, 0)

    def row_copy(r, k):
        return pltpu.make_async_copy(h_ref.at[r], xs_ref.at[slot_ref[0, 0, 2 * r + k]], sem)

    def issue(r, c):
        row_copy(r, 0).start(priority=0)
        row_copy(r, 1).start(priority=1)
        return c

    lax.fori_loop(0, ts, issue, 0, unroll=8)
    for _ in range(2):
        pltpu.make_async_copy(h_ref, xs_ref.at[pl.ds(0, ts)], sem).wait()


def _dispatch(pad_info, slots3, h2, n_slots, ts):
    S = h2.shape[0]
    return pl.pallas_call(
        functools.partial(_dispatch_kernel, ts=ts),
        grid=(S // ts,),
        in_specs=[pl.BlockSpec(memory_space=pltpu.SMEM),
                  pl.BlockSpec((1, 1, 2 * ts), lambda i: (i, 0, 0), memory_space=pltpu.SMEM),
                  pl.BlockSpec((ts, ROW_TILES, LANES), lambda i: (i, 0, 0))],
        out_specs=pl.BlockSpec(memory_space=pl.ANY),
        out_shape=jax.ShapeDtypeStruct((n_slots, ROW_TILES, LANES), F32),
        scratch_shapes=[pltpu.VMEM((MOE_TM // 2, ROW_TILES, LANES), F32),
                        pltpu.SemaphoreType.DMA(()), pltpu.SemaphoreType.DMA(())],
        compiler_params=_cparams(("arbitrary",)),
        name="moe_dispatch",
    )(pad_info, slots3, h2)


MOE_FC = 512


N_FC = D_FF_EXPERT // MOE_FC
MOE_WR = 64
MOE_STAGES = 4


def _moe_kernel(te_ref, tv_ref, xs_ref, wgu_ref, wdn_ref, y_ref, x_sc, act_sc, wg_sc, wu_sc,
                wd_sc, stage_in, stage_out, sem):
    t = pl.program_id(0)
    e = te_ref[t]
    valid = tv_ref[t] != 0

    @pl.when(valid & ((t == 0) | (e != te_ref[jnp.maximum(t - 1, 0)])))
    def _():
        n_in = D_MODEL // MOE_WR
        n_chunks = n_in + N_FC
        ahead = MOE_STAGES - 1

        def copy(j):
            if j < n_in:
                src, stage = wgu_ref.at[e, pl.ds(j * MOE_WR, MOE_WR), :], stage_in
            else:
                src, stage = wdn_ref.at[e, pl.ds((j - n_in) * MOE_FC, MOE_FC), :], stage_out
            return pltpu.make_async_copy(src, stage.at[j % MOE_STAGES], sem.at[j % MOE_STAGES])

        for j in range(ahead):
            copy(j).start()
        for j in range(n_chunks):
            if j + ahead < n_chunks:
                copy(j + ahead).start()
            copy(j).wait()
            slot = j % MOE_STAGES
            if j < n_in:
                rows = pl.ds(j * MOE_WR, MOE_WR)
                wg_sc[rows, :] = stage_in[slot, :, :D_FF_EXPERT].astype(BF16)
                wu_sc[rows, :] = stage_in[slot, :, D_FF_EXPERT:].astype(BF16)
            else:
                rows = pl.ds((j - n_in) * MOE_FC, MOE_FC)
                wd_sc[rows, :] = stage_out[slot].astype(BF16)

    @pl.when(jnp.logical_not(valid))
    def _():
        y_ref[...] = jnp.zeros_like(y_ref)

    @pl.when(valid)
    def _():
        for c in range(ROW_TILES):
            chunk = xs_ref[pl.ds(c, MOE_TM, stride=ROW_TILES), :]
            x_sc[:, c * LANES:(c + 1) * LANES] = chunk.astype(BF16)
        x = x_sc[...]
        for c in range(N_FC):
            sl = slice(c * MOE_FC, (c + 1) * MOE_FC)
            gate = _dot(x, wg_sc[:, sl])
            up = _dot(x, wu_sc[:, sl])
            act_sc[:, sl] = (gate * jax.nn.sigmoid(gate) * up).astype(BF16)
        acc = _dot(act_sc[...], wd_sc[...])
        for c in range(ROW_TILES):
            y_ref[pl.ds(c, MOE_TM, stride=ROW_TILES), :] = acc[:, c * LANES:(c + 1) * LANES]


def _moe(tile_expert, tile_valid, xs, w_gu, w_down):
    n_slots = xs.shape[0]
    nt = n_slots // MOE_TM
    grid_spec = pltpu.PrefetchScalarGridSpec(
        num_scalar_prefetch=2,
        grid=(nt,),
        in_specs=[
            pl.BlockSpec((MOE_TM * ROW_TILES, LANES), lambda t, te, tv: (t, 0)),
            pl.BlockSpec(memory_space=pl.ANY),
            pl.BlockSpec(memory_space=pl.ANY),
        ],
        out_specs=pl.BlockSpec((MOE_TM * ROW_TILES, LANES), lambda t, te, tv: (t, 0)),
        scratch_shapes=[pltpu.VMEM((MOE_TM, D_MODEL), BF16),
                        pltpu.VMEM((MOE_TM, D_FF_EXPERT), BF16),
                        pltpu.VMEM((D_MODEL, D_FF_EXPERT), BF16),
                        pltpu.VMEM((D_MODEL, D_FF_EXPERT), BF16),
                        pltpu.VMEM((D_FF_EXPERT, D_MODEL), BF16),
                        pltpu.VMEM((MOE_STAGES, MOE_WR, 2 * D_FF_EXPERT), F32),
                        pltpu.VMEM((MOE_STAGES, MOE_FC, D_MODEL), F32),
                        pltpu.SemaphoreType.DMA((MOE_STAGES,))],
    )
    y = pl.pallas_call(
        _moe_kernel,
        grid_spec=grid_spec,
        out_shape=jax.ShapeDtypeStruct((n_slots * ROW_TILES, LANES), F32),
        compiler_params=_cparams(("arbitrary",)),
        name="moe_experts",
    )(tile_expert, tile_valid, xs.reshape(n_slots * ROW_TILES, LANES), w_gu, w_down)
    return y.reshape(n_slots, ROW_TILES, LANES)


def _combine_kernel(slot_ref, next_slot_ref, x3_ref, meta_ref, y_ref, y2_ref, o_ref, buf, sem,
                    *, ts):
    i = pl.program_id(0)
    n = pl.num_programs(0)
    cur = lax.rem(i, 2)

    def gather(slots, half):
        def issue(r, c):
            for k in range(2):
                dst = buf.at[half, k, pl.ds(pl.multiple_of(r * ROW_TILES, ROW_TILES), ROW_TILES)]
                pltpu.make_async_copy(y_ref.at[slots[0, 0, 2 * r + k]], dst,
                                      sem.at[half]).start(priority=k)
            return c

        lax.fori_loop(0, ts, issue, 0, unroll=8)

    @pl.when(i == 0)
    def _():
        gather(slot_ref, 0)

    @pl.when(i + 1 < n)
    def _():
        gather(next_slot_ref, 1 - cur)

    for k in range(2):
        pltpu.make_async_copy(y2_ref.at[pl.ds(0, ts * ROW_TILES)], buf.at[cur, k],
                              sem.at[cur]).wait()
    meta = meta_ref[...]
    g1 = jnp.broadcast_to(meta[:, 2:3], (ts, LANES))
    g2 = jnp.broadcast_to(meta[:, 3:4], (ts, LANES))
    for c in range(ROW_TILES):
        sl = slice(c * LANES, (c + 1) * LANES)
        rows = pl.ds(c, ts, stride=ROW_TILES)
        o_ref[:, sl] = x3_ref[:, sl] + g1 * buf[cur, 0, rows, :] + g2 * buf[cur, 1, rows, :]


def _combine(slots3, x3, meta, y, ts):
    S = x3.shape[0]
    n = S // ts
    return pl.pallas_call(
        functools.partial(_combine_kernel, ts=ts),
        grid=(n,),
        in_specs=[pl.BlockSpec((1, 1, 2 * ts), lambda i: (i, 0, 0), memory_space=pltpu.SMEM),
                  pl.BlockSpec((1, 1, 2 * ts), lambda i: (jnp.minimum(i + 1, n - 1), 0, 0),
                               memory_space=pltpu.SMEM),
                  pl.BlockSpec((ts, D_MODEL), lambda i: (i, 0)),
                  pl.BlockSpec((ts, LANES), lambda i: (i, 0)),
                  pl.BlockSpec(memory_space=pl.ANY), pl.BlockSpec(memory_space=pl.ANY)],
        out_specs=pl.BlockSpec((ts, D_MODEL), lambda i: (i, 0)),
        out_shape=jax.ShapeDtypeStruct((S, D_MODEL), F32),
        scratch_shapes=[pltpu.VMEM((2, 2, ts * ROW_TILES, LANES), F32),
                        pltpu.SemaphoreType.DMA((2,))],
        compiler_params=_cparams(("arbitrary",)),
        name="moe_combine",
    )(slots3, slots3, x3, meta, y, y.reshape(-1, LANES))


def _tile_lanes(v, reps):
    return jnp.tile(v, reps)[None, :].astype(F32)


def _rope_cos_sin(S, dim):
    pos = np.arange(S, dtype=np.float64)
    inv = ROPE_THETA ** (-np.arange(0, dim, 2, dtype=np.float64) / dim)
    ang = pos[:, None] * inv[None, :]
    return np.cos(ang), np.sin(ang)


def _rope_tables_64(S):
    c, s = _rope_cos_sin(S, A_HEAD_DIM)
    return (jnp.asarray(np.concatenate([c, c, c, c], axis=1), F32),
            jnp.asarray(np.concatenate([-s, s, -s, s], axis=1), F32))


def _rope_tables_32(S):
    c, s = _rope_cos_sin(S, C_ROPE)
    one = np.ones((S, C_NOPE))
    zero = np.zeros((S, C_NOPE))
    pad = np.zeros((S, LANES - C_QK))
    return (jnp.asarray(np.concatenate([one, c, c, pad], axis=1), F32),
            jnp.asarray(np.concatenate([zero, -s, s, pad], axis=1), F32))


def _pad_heads(w, n_heads, width):
    k = w.shape[0]
    w = w.reshape(k, n_heads, width)
    return jnp.pad(w, ((0, 0), (0, 0), (0, LANES - width))).reshape(k, n_heads * LANES)


def _swap_rope_cols(w_p):
    k = w_p.shape[0]
    w = w_p.reshape(k, -1, LANES)
    half = C_ROPE // 2
    sw = jnp.concatenate([jnp.zeros_like(w[:, :, :C_NOPE]),
                          w[:, :, C_NOPE + half:C_QK], w[:, :, C_NOPE:C_NOPE + half],
                          jnp.zeros_like(w[:, :, C_QK:])], axis=2)
    return sw.reshape(k, -1)


def kernel(x, l0_norm_attn, l0_w_in, l0_a_q_norm, l0_a_k_norm, l0_a_lambda, l0_a_subln, l0_b_q_norm, l0_b_k_norm, l0_b_sinks, l0_w_out, l0_norm_ffn, l0_ffn_w_gu, l0_ffn_w_down, l1_norm_attn, l1_c_w_in, l1_c_q_lora_norm, l1_c_kv_lora_norm, l1_c_w_uq, l1_c_w_ukv, l1_c_q_norm, l1_c_k_norm, l1_c_w_out, l1_norm_ffn, l1_router, l1_exp_w_gu, l1_exp_w_down):
    B, S, _ = x.shape
    assert B == 1
    x2d = x.reshape(S, D_MODEL)
    tm = min(512, S)

    o_bk = 3 * A_W + B_QW
    o_bv = o_bk + B_KW
    bk_w = l0_w_in[:, o_bk:o_bv].reshape(D_MODEL, B_KV_HEADS, B_HEAD_DIM)
    bv_w = l0_w_in[:, o_bv:].reshape(D_MODEL, B_KV_HEADS, B_HEAD_DIM)
    dup = lambda w: jnp.concatenate([w, w], axis=2).reshape(D_MODEL, 2 * B_KW)
    w_in_p = jnp.concatenate([l0_w_in[:, :2 * A_W], l0_w_in[:, 3 * A_W:o_bk], dup(bk_w),
                              dup(bv_w)], axis=1).astype(BF16)
    w_avt = l0_w_in[:, 2 * A_W:3 * A_W].T.reshape(A_HEADS, 2 * A_HEAD_DIM, D_MODEL)
    w_avt = jnp.pad(w_avt, ((0, 0), (0, V_PAD), (0, 0))).reshape(-1, D_MODEL).astype(BF16)
    a_vone = jnp.zeros((A_HEADS, A_V_ROWS, 1), F32).at[:, 2 * A_HEAD_DIM, 0].set(1.0).reshape(-1, 1)
    gains = jnp.concatenate([_tile_lanes(l0_a_q_norm, 2), _tile_lanes(l0_a_k_norm, 2),
                             _tile_lanes(l0_b_q_norm, 2), _tile_lanes(l0_b_k_norm, 2),
                             jnp.zeros((4, LANES), F32)], axis=0)
    lane = jnp.arange(LANES)
    ones64 = (lane[:, None] // 64 == lane[None, :] // 64).astype(BF16)
    cos64, sin64 = _rope_tables_64(S)
    aq, ak, avt, bq, bk, bv = _l0_in(x2d, l0_norm_attn[None, :], w_in_p, w_avt, a_vone, gains,
                                     ones64, cos64, sin64, tm)
    lam_init = 0.8 - 0.6 * math.exp(-0.3 * 0)
    a_out = _a_attn(l0_a_lambda.astype(F32), aq, ak, avt, l0_a_subln[None, :].astype(F32),
                    lam_init)
    b_out = _b_attn(l0_b_sinks.astype(F32), bq, bk, bv)
    w_out = l0_w_out.astype(BF16)
    x2 = _l0_ffn(a_out, b_out, x2d, w_out[:A_W], w_out[A_W:], l0_norm_ffn[None, :],
                 l0_ffn_w_gu.astype(BF16), l0_ffn_w_down.astype(BF16), tm, D_FF // 2)

    wq = l1_c_w_in[:, :C_Q_RANK].astype(BF16)
    wkv = l1_c_w_in[:, C_Q_RANK:C_Q_RANK + C_KV_RANK].astype(BF16)
    wkr = jnp.pad(l1_c_w_in[:, C_Q_RANK + C_KV_RANK:], ((0, 0), (C_NOPE, LANES - C_QK)))
    wkrs = _swap_rope_cols(wkr).astype(BF16)
    wkr = wkr.astype(BF16)
    wuq = _pad_heads(l1_c_w_uq, C_HEADS, C_QK)
    wuqs = _swap_rope_cols(wuq).astype(BF16)
    wuq = wuq.astype(BF16)
    ukv = l1_c_w_ukv.reshape(C_KV_RANK, C_HEADS, C_NOPE + C_V)
    wuk = _pad_heads(ukv[:, :, :C_NOPE].reshape(C_KV_RANK, -1), C_HEADS, C_NOPE).astype(BF16)
    wuv = jnp.pad(jnp.transpose(ukv[:, :, C_NOPE:], (1, 2, 0)), ((0, 0), (0, V_PAD), (0, 0)))
    wuv = wuv.reshape(C_HEADS * V_ROWS, C_KV_RANK).astype(BF16)
    vone = jnp.zeros((C_HEADS, V_ROWS, 1), F32).at[:, C_V, 0].set(1.0).reshape(-1, 1)
    pad_gain = lambda gvec: jnp.pad(gvec.astype(F32), (0, LANES - C_QK))[None, :]
    qg, kg = pad_gain(l1_c_q_norm), pad_gain(l1_c_k_norm)
    qgs, kgs = _swap_rope_cols(qg), _swap_rope_cols(kg)
    ones128 = jnp.ones((LANES, LANES), BF16)
    lane2 = jnp.arange(2 * LANES)
    ones_pair = (lane2[:, None] // LANES == lane2[None, :] // LANES).astype(BF16)
    cos32, sin32 = _rope_tables_32(S)
    cq, ck, cvt = _c_in(x2, l1_norm_attn[None, :], wq, wkv, wkr, wkrs,
                       l1_c_q_lora_norm[None, :], l1_c_kv_lora_norm[None, :],
                       wuq, wuqs, wuk, wuv, qg, qgs, kg, kgs, vone, ones128, ones_pair, cos32,
                       sin32, tm)
    c_o = _c_attn(cq, ck, cvt)

    r_pad = jnp.pad(l1_router.astype(F32), ((0, 0), (0, LANES - N_EXPERTS)))
    r_hi = r_pad.astype(BF16)
    r_lo = (r_pad - r_hi.astype(F32)).astype(BF16)
    tr = min(256, S)
    ridx = jnp.arange(tr)
    tri = (ridx[None, :] < ridx[:, None]).astype(BF16)
    x3, h2, meta, cnt = _c_out_router(c_o, x2, l1_c_w_out.astype(BF16), l1_norm_ffn[None, :],
                                      r_hi, r_lo, tri, tr)

    counts = cnt[0, :N_EXPERTS].astype(jnp.int32)
    padded = ((counts + MOE_TM - 1) // MOE_TM) * MOE_TM
    ends = jnp.cumsum(padded)
    offs = ends - padded
    idx = meta[:, 0:2].astype(jnp.int32)
    pos = meta[:, 4:6].astype(jnp.int32)
    slots = offs[idx] + pos
    n_slots = 2 * S + N_EXPERTS * MOE_TM
    nt = n_slots // MOE_TM
    tile_start = jnp.arange(nt, dtype=jnp.int32) * MOE_TM
    n_valid = jnp.broadcast_to(ends[-1] // MOE_TM, (N_EXPERTS,))
    tile_valid = (tile_start < ends[-1]).astype(jnp.int32)
    tile_expert = jnp.minimum(
        jnp.sum((tile_start[:, None] >= ends[None, :]).astype(jnp.int32), axis=1),
        N_EXPERTS - 1).astype(jnp.int32)
    pad_info = jnp.stack([offs + counts, padded - counts, n_valid]).astype(jnp.int32)

    ts = min(512, S)
    slots3 = slots.reshape(S // ts, 1, 2 * ts)
    xs = _dispatch(pad_info, slots3, h2.reshape(S, ROW_TILES, LANES), n_slots, ts)
    y = _moe(tile_expert, tile_valid, xs,
             l1_exp_w_gu.astype(F32), l1_exp_w_down.astype(F32))
    out = _combine(slots3, x3, meta, y, ts)
    return out.reshape(B, S, D_MODEL)
```

```python
import functools
import math

import jax
import jax.numpy as jnp
import numpy as np
from jax import lax
from jax.experimental import pallas as pl
from jax.experimental.pallas import tpu as pltpu

F32 = jnp.float32
BF16 = jnp.bfloat16

D_MODEL = 1024
CHUNK = 64
ROPE_THETA = 10000.0
EPS = 1e-6
LANES = 128
ROW_TILES = D_MODEL // LANES

A_HEADS = 4
A_HEAD_DIM = 64
B_HEADS = 8
B_KV_HEADS = 2
B_HEAD_DIM = 64
A_W = A_HEADS * 2 * A_HEAD_DIM
B_QW = B_HEADS * B_HEAD_DIM
B_KW = B_KV_HEADS * B_HEAD_DIM

C_HEADS = 16
C_Q_RANK = 256
C_KV_RANK = 128
C_NOPE = 64
C_ROPE = 32
C_V = 64
C_QK = C_NOPE + C_ROPE

D_FF = 2816
N_EXPERTS = 8
D_FF_EXPERT = 3584

NEG_BIG = -1e30
LOG2E = 1.0 / math.log(2.0)
VMEM_LIMIT = 56 * 1024 * 1024


def _cparams(sem):
    return pltpu.CompilerParams(dimension_semantics=sem, vmem_limit_bytes=VMEM_LIMIT)


def _dot(a, b):
    return jnp.dot(a, b, preferred_element_type=F32)


def _dot_nt(a, b):
    return lax.dot_general(a, b, (((1,), (1,)), ((), ())), preferred_element_type=F32)


def _rms(x, g):
    ms = jnp.mean(x * x, axis=-1, keepdims=True)
    return x * lax.rsqrt(ms + EPS) * g


def _l0_in_kernel(x_ref, g_ref, w_ref, wvt_ref, vone_ref, gains_ref, ones_ref, cos_ref, sin_ref,
                  aq_o, ak_o, avt_o, bq_o, bk_o, bv_o):
    h = _rms(x_ref[...], g_ref[...]).astype(BF16)
    z = _dot(h, w_ref[...])
    avt_o[...] = (_dot_nt(wvt_ref[...], h) + vone_ref[...]).astype(BF16)
    cos = cos_ref[...]
    sin = sin_ref[...]
    ones_blk = ones_ref[...]
    lane = lax.broadcasted_iota(jnp.int32, (1, LANES), 1)
    first_half = (lane % 64) < 32

    def norm_rope(zs, gain, scale):
        ss = _dot((zs * zs).astype(BF16), ones_blk)
        r = lax.rsqrt(ss * (1.0 / 64.0) + EPS) * scale
        y = zs * gain
        sw = jnp.where(first_half, pltpu.roll(y, 96, 1), pltpu.roll(y, 32, 1))
        return ((y * cos + sw * sin) * r).astype(BF16)

    col = 0
    gcol = 0
    scale = A_HEAD_DIM ** -0.5
    scales = {"aq": scale * LOG2E, "ak": 1.0, "bq": scale, "bk": 1.0}
    outs = {"aq": aq_o, "ak": ak_o, "bq": bq_o, "bk": bk_o}
    for name, nslab in (("aq", 4), ("ak", 4), ("bq", 4), ("bk", 2), ("bv", 2)):
        if name == "bv":
            bv_o[...] = z[:, col:col + 2 * LANES].astype(BF16)
            col += 2 * LANES
            continue
        sc = scales[name]
        gain = gains_ref[gcol:gcol + 1, :]
        gcol += 1
        for s in range(nslab):
            zs = z[:, col:col + LANES]
            outs[name][:, s * LANES:(s + 1) * LANES] = norm_rope(zs, gain, sc)
            col += LANES


def _l0_in(x2d, g, w_in_p, w_avt, vone, gains, ones_blk, cos_t, sin_t, tm):
    S = x2d.shape[0]
    nw = w_in_p.shape[1]
    row = lambda i: (i, 0)
    const = lambda i: (0, 0)
    out_shapes = (
        jax.ShapeDtypeStruct((S, A_W), BF16), jax.ShapeDtypeStruct((S, A_W), BF16),
        jax.ShapeDtypeStruct((A_HEADS * A_V_ROWS, S), BF16), jax.ShapeDtypeStruct((S, B_QW), BF16),
        jax.ShapeDtypeStruct((S, 2 * B_KW), BF16), jax.ShapeDtypeStruct((S, 2 * B_KW), BF16))
    return pl.pallas_call(
        _l0_in_kernel,
        grid=(S // tm,),
        in_specs=[
            pl.BlockSpec((tm, D_MODEL), row),
            pl.BlockSpec((1, D_MODEL), const),
            pl.BlockSpec((D_MODEL, nw), const),
            pl.BlockSpec((A_HEADS * A_V_ROWS, D_MODEL), const),
            pl.BlockSpec((A_HEADS * A_V_ROWS, 1), const),
            pl.BlockSpec((8, LANES), const),
            pl.BlockSpec((LANES, LANES), const),
            pl.BlockSpec((tm, LANES), row),
            pl.BlockSpec((tm, LANES), row),
        ],
        out_specs=[
            pl.BlockSpec((tm, A_W), row), pl.BlockSpec((tm, A_W), row),
            pl.BlockSpec((A_HEADS * A_V_ROWS, tm), lambda i: (0, i)), pl.BlockSpec((tm, B_QW), row),
            pl.BlockSpec((tm, 2 * B_KW), row), pl.BlockSpec((tm, 2 * B_KW), row)],
        out_shape=out_shapes,
        compiler_params=_cparams(("arbitrary",)),
        name="l0_in_proj",
    )(x2d, g, w_in_p, w_avt, vone, gains, ones_blk, cos_t, sin_t)


ATT_TQ = 512
ATT_TK = 256
V_PAD = 16


def _softmax_pv(s_ref, s_max, vt, m, acc):
    m_new = jnp.maximum(m, s_max)
    alpha = jnp.exp2(m - m_new)
    p = jnp.exp2(s_ref[...] - m_new).astype(BF16)
    return m_new, alpha * acc + _dot(vt, p)


def _diag_masks():
    key_chunk = lax.broadcasted_iota(jnp.int32, (ATT_TK, ATT_TQ), 0) // CHUNK
    query_chunk = lax.broadcasted_iota(jnp.int32, (ATT_TK, ATT_TQ), 1) // CHUNK
    return [key_chunk + b * (ATT_TK // CHUNK) <= query_chunk for b in range(ATT_TQ // ATT_TK)]


def _attn_pipeline(i, k_ref, vt_ref, streams, v_rows):
    n = len(streams)

    for st in streams:
        st[5][...] = st[0].T

    def scores(st, blk, mask, dst):
        lanes, qt_ref = streams[st][1], streams[st][5]
        off = pl.multiple_of(blk * ATT_TK, ATT_TK)
        s = _dot(k_ref[pl.ds(off, ATT_TK), lanes], qt_ref[...])
        if mask is not None:
            s = jnp.where(mask, s, NEG_BIG)
        dst[...] = s
        return jnp.max(s, axis=0, keepdims=True)

    def consume(st, blk, src, s_max, state):
        off = pl.multiple_of(blk * ATT_TK, ATT_TK)
        return _softmax_pv(src, s_max, vt_ref[streams[st][2], pl.ds(off, ATT_TK)], *state)

    buf_a = [st[3] for st in streams]
    buf_b = [st[4] for st in streams]
    mask0, mask1 = _diag_masks()
    d0 = 2 * i
    d1 = d0 + 1
    state = [(jnp.full((1, ATT_TQ), NEG_BIG, F32), jnp.zeros((v_rows, ATT_TQ), F32))] * n
    max_a = [scores(st, d0, mask0, buf_a[st]) for st in range(n)]
    max_b = [None] * n
    for st in range(n):
        max_b[st] = scores(st, d1, mask1, buf_b[st])
        state[st] = consume(st, d0, buf_a[st], max_a[st], state[st])
    for st in range(n):
        max_a[st] = scores(st, 0, None, buf_a[st])
        state[st] = consume(st, d1, buf_b[st], max_b[st], state[st])

    def body(p, carry):
        carry, max_a = list(carry[0]), list(carry[1])
        max_b = [None] * n
        u0 = 2 * p
        u1 = u0 + 1
        nxt = jnp.minimum(u0 + 2, d0 - 1)
        for st in range(n):
            max_b[st] = scores(st, u1, None, buf_b[st])
            carry[st] = consume(st, u0, buf_a[st], max_a[st], carry[st])
        for st in range(n):
            max_a[st] = scores(st, nxt, None, buf_a[st])
            carry[st] = consume(st, u1, buf_b[st], max_b[st], carry[st])
        return tuple(carry), tuple(max_a)

    final, _ = lax.fori_loop(0, i, body, (tuple(state), tuple(max_a)))
    return [acc for _, acc in final]


_SCORE_BUF = pltpu.VMEM((ATT_TK, ATT_TQ), F32)
_QT_BUF = pltpu.VMEM((LANES, ATT_TQ), BF16)


A_V_ROWS = 2 * A_HEAD_DIM + V_PAD


A_HPS = 2


def _a_attn_kernel(lam_ref, q_ref, k_ref, vt_ref, subln_ref, o_ref, *scratch, lam_init):
    i = pl.program_id(1)
    lane = lax.broadcasted_iota(jnp.int32, (1, LANES), 1)
    streams = []
    for h in range(A_HPS):
        q = q_ref[:, h * LANES:(h + 1) * LANES]
        zero = jnp.zeros_like(q)
        lanes = slice(h * LANES, (h + 1) * LANES)
        rows = slice(h * A_V_ROWS, (h + 1) * A_V_ROWS)
        for mp, sel in enumerate((lane < 64, lane >= 64)):
            st = 2 * h + mp
            streams.append((jnp.where(sel, q, zero), lanes, rows, scratch[2 * st],
                            scratch[2 * st + 1], scratch[4 * A_HPS + st]))
    accs = _attn_pipeline(i, k_ref, vt_ref, tuple(streams), A_V_ROWS)
    dv = 2 * A_HEAD_DIM
    lf = lam_ref[...]
    lam = (jnp.exp(jnp.sum(lf[0:1] * lf[1:2], axis=1, keepdims=True))
           - jnp.exp(jnp.sum(lf[2:3] * lf[3:4], axis=1, keepdims=True)) + lam_init)
    for h in range(A_HPS):
        acc1, acc2 = accs[2 * h], accs[2 * h + 1]
        out = (acc1[:dv] / acc1[dv:dv + 1] - lam * (acc2[:dv] / acc2[dv:dv + 1])).T
        out = _rms(out, subln_ref[...]) * (1.0 - lam_init)
        o_ref[:, h * LANES:(h + 1) * LANES] = out.astype(BF16)


def _a_attn(lam_p, aq, ak, avt, subln, lam_init):
    S = aq.shape[0]
    once = pl.Buffered(1)
    return pl.pallas_call(
        functools.partial(_a_attn_kernel, lam_init=lam_init),
        grid=(A_HEADS // A_HPS, S // ATT_TQ),
        in_specs=[
            pl.BlockSpec((4, A_HEAD_DIM), lambda h, i: (0, 0)),
            pl.BlockSpec((ATT_TQ, A_HPS * LANES), lambda h, i: (i, h)),
            pl.BlockSpec((S, A_HPS * LANES), lambda h, i: (0, h), pipeline_mode=once),
            pl.BlockSpec((A_HPS * A_V_ROWS, S), lambda h, i: (h, 0), pipeline_mode=once),
            pl.BlockSpec((1, LANES), lambda h, i: (0, 0)),
        ],
        out_specs=pl.BlockSpec((ATT_TQ, A_HPS * LANES), lambda h, i: (i, h)),
        out_shape=jax.ShapeDtypeStruct((S, A_W), BF16),
        scratch_shapes=[_SCORE_BUF] * (4 * A_HPS) + [_QT_BUF] * (2 * A_HPS),
        compiler_params=_cparams(("arbitrary", "arbitrary")),
        name="a_diff_attn",
    )(lam_p, aq, ak, avt, subln)


B_BLK = 256
B_BACK = 128


def _b_attn_kernel(sink_ref, q_ref, kp_ref, kc_ref, vp_ref, vc_ref, o_ref):
    i = pl.program_id(0)
    lane = lax.broadcasted_iota(jnp.int32, (1, LANES), 1)
    lo = lane < 64
    r = lax.broadcasted_iota(jnp.int32, (B_BLK, B_BACK + B_BLK), 0) // CHUNK
    c_idx = lax.broadcasted_iota(jnp.int32, (B_BLK, B_BACK + B_BLK), 1)
    c = c_idx // CHUNK
    mask = (c >= r) & (c <= r + 2) & ((c_idx >= B_BACK) | (i > 0))
    for slab in range(B_HEADS // 2):
        g = slab // 2
        k = jnp.concatenate([kp_ref[:, g * LANES:(g + 1) * LANES],
                             kc_ref[:, g * LANES:(g + 1) * LANES]], axis=0)
        v = jnp.concatenate([vp_ref[:, g * LANES:(g + 1) * LANES],
                             vc_ref[:, g * LANES:(g + 1) * LANES]], axis=0)
        qs = q_ref[:, slab * LANES:(slab + 1) * LANES]
        zq = jnp.zeros_like(qs)
        zv = jnp.zeros_like(v)
        out = jnp.zeros((B_BLK, LANES), F32)
        for half in range(2):
            sel = lo if half == 0 else jnp.logical_not(lo)
            sink = sink_ref[2 * slab + half]
            s = _dot_nt(jnp.where(sel, qs, zq), k)
            s = jnp.where(mask, s, NEG_BIG)
            m = jnp.maximum(jnp.max(s, axis=1, keepdims=True), sink)
            e = jnp.exp(s - m)
            denom = jnp.sum(e, axis=1, keepdims=True) + jnp.exp(sink - m)
            p = (e / denom).astype(BF16)
            out = out + _dot(p, jnp.where(sel, v, zv))
        o_ref[:, slab * LANES:(slab + 1) * LANES] = out.astype(BF16)


def _b_attn(sinks, bq, bk, bv):
    S = bq.shape[0]
    prev = lambda i: (jnp.maximum(i * (B_BLK // B_BACK) - 1, 0), 0)
    cur = lambda i: (i, 0)
    return pl.pallas_call(
        _b_attn_kernel,
        grid=(S // B_BLK,),
        in_specs=[
            pl.BlockSpec(memory_space=pltpu.SMEM),
            pl.BlockSpec((B_BLK, B_QW), cur),
            pl.BlockSpec((B_BACK, 2 * B_KW), prev),
            pl.BlockSpec((B_BLK, 2 * B_KW), cur),
            pl.BlockSpec((B_BACK, 2 * B_KW), prev),
            pl.BlockSpec((B_BLK, 2 * B_KW), cur),
        ],
        out_specs=pl.BlockSpec((B_BLK, B_QW), cur),
        out_shape=jax.ShapeDtypeStruct((S, B_QW), BF16),
        compiler_params=_cparams(("arbitrary",)),
        name="b_swa_attn",
    )(sinks, bq, bk, bk, bv, bv)


def _l0_ffn_kernel(a_ref, b_ref, x_ref, woa_ref, wob_ref, g_ref, wgu_ref, wd_ref, o_ref, act_sc,
                   *, tf):
    x1 = x_ref[...] + _dot(a_ref[...], woa_ref[...]) + _dot(b_ref[...], wob_ref[...])
    h = _rms(x1, g_ref[...]).astype(BF16)
    for c in range(D_FF // tf):
        gate = _dot(h, wgu_ref[:, c * tf:(c + 1) * tf])
        up = _dot(h, wgu_ref[:, D_FF + c * tf:D_FF + (c + 1) * tf])
        act_sc[:, c * tf:(c + 1) * tf] = (gate * jax.nn.sigmoid(gate) * up).astype(BF16)
    o_ref[...] = x1 + _dot(act_sc[...], wd_ref[...])


def _l0_ffn(a_out, b_out, x2d, wo_a, wo_b, g, w_gu, w_down, tm, tf):
    S = x2d.shape[0]
    row = lambda i: (i, 0)
    const = lambda i: (0, 0)
    return pl.pallas_call(
        functools.partial(_l0_ffn_kernel, tf=tf),
        grid=(S // tm,),
        in_specs=[
            pl.BlockSpec((tm, A_W), row),
            pl.BlockSpec((tm, B_QW), row),
            pl.BlockSpec((tm, D_MODEL), row),
            pl.BlockSpec((A_W, D_MODEL), const),
            pl.BlockSpec((B_QW, D_MODEL), const),
            pl.BlockSpec((1, D_MODEL), const),
            pl.BlockSpec((D_MODEL, 2 * D_FF), const, pipeline_mode=pl.Buffered(1)),
            pl.BlockSpec((D_FF, D_MODEL), const, pipeline_mode=pl.Buffered(1)),
        ],
        out_specs=pl.BlockSpec((tm, D_MODEL), row),
        out_shape=jax.ShapeDtypeStruct((S, D_MODEL), F32),
        scratch_shapes=[pltpu.VMEM((tm, D_FF), BF16)],
        compiler_params=_cparams(("arbitrary",)),
        name="l0_out_ffn",
    )(a_out, b_out, x2d, wo_a, wo_b, g, w_gu, w_down)


def _c_in_kernel(x_ref, g_ref, wq_ref, wkv_ref, wkr_ref, wkrs_ref, gql_ref, gkvl_ref,
                 wuq_ref, wuqs_ref, wuk_ref, wuv_ref, qg_ref, qgs_ref, kg_ref, kgs_ref,
                 vone_ref, ones_ref, ones2_ref, cos_ref, sin_ref, q_o, k_o, vt_o):
    h = _rms(x_ref[...], g_ref[...]).astype(BF16)
    cq = _rms(_dot(h, wq_ref[...]), gql_ref[...]).astype(BF16)
    ckv = _rms(_dot(h, wkv_ref[...]), gkvl_ref[...]).astype(BF16)
    kr = _dot(h, wkr_ref[...])
    krs = _dot(h, wkrs_ref[...])
    q = _dot(cq, wuq_ref[...])
    qs = _dot(cq, wuqs_ref[...])
    kn = _dot(ckv, wuk_ref[...])
    vt_o[...] = (_dot_nt(wuv_ref[...], ckv) + vone_ref[...]).astype(BF16)

    cos = cos_ref[...]
    sin = sin_ref[...]
    ones_blk = ones_ref[...]
    qg, qgs, kg, kgs = qg_ref[...], qgs_ref[...], kg_ref[...], kgs_ref[...]
    ss_kr = _dot((kr * kr).astype(BF16), ones_blk)
    kr_roped = kr * kg * cos + krs * kgs * sin
    scale = C_QK ** -0.5 * LOG2E
    inv = 1.0 / C_QK
    ones_pair = ones2_ref[...]
    for pair in range(C_HEADS // 2):
        both = slice(2 * pair * LANES, (2 * pair + 2) * LANES)
        ss_q = _dot((q[:, both] * q[:, both]).astype(BF16), ones_pair)
        ss_k = _dot((kn[:, both] * kn[:, both]).astype(BF16), ones_pair)
        for half in range(2):
            sl = slice((2 * pair + half) * LANES, (2 * pair + half + 1) * LANES)
            hs = slice(half * LANES, (half + 1) * LANES)
            qh = q[:, sl]
            r = lax.rsqrt(ss_q[:, hs] * inv + EPS) * scale
            q_o[:, sl] = ((qh * qg * cos + qs[:, sl] * qgs * sin) * r).astype(BF16)
            rk = lax.rsqrt((ss_k[:, hs] + ss_kr) * inv + EPS)
            k_o[:, sl] = ((kn[:, sl] * kg + kr_roped) * rk).astype(BF16)


def _c_in(x2d, g, wq, wkv, wkr, wkrs, gql, gkvl, wuq, wuqs, wuk, wuv, qg, qgs, kg, kgs,
          vone, ones_blk, ones_pair, cos_t, sin_t, tm):
    S = x2d.shape[0]
    row = lambda i: (i, 0)
    const = lambda i: (0, 0)
    full = lambda a: pl.BlockSpec(a.shape, const)
    W = C_HEADS * LANES
    return pl.pallas_call(
        _c_in_kernel,
        grid=(S // tm,),
        in_specs=[pl.BlockSpec((tm, D_MODEL), row), full(g), full(wq), full(wkv), full(wkr),
                  full(wkrs), full(gql), full(gkvl), full(wuq), full(wuqs), full(wuk),
                  full(wuv), full(qg), full(qgs), full(kg), full(kgs), full(vone),
                  full(ones_blk), full(ones_pair),
                  pl.BlockSpec((tm, LANES), row), pl.BlockSpec((tm, LANES), row)],
        out_specs=[pl.BlockSpec((tm, W), row), pl.BlockSpec((tm, W), row),
                   pl.BlockSpec((C_HEADS * V_ROWS, tm), lambda i: (0, i))],
        out_shape=(jax.ShapeDtypeStruct((S, W), BF16), jax.ShapeDtypeStruct((S, W), BF16),
                   jax.ShapeDtypeStruct((C_HEADS * V_ROWS, S), BF16)),
        compiler_params=_cparams(("arbitrary",)),
        name="c_in_proj",
    )(x2d, g, wq, wkv, wkr, wkrs, gql, gkvl, wuq, wuqs, wuk, wuv, qg, qgs, kg, kgs,
      vone, ones_blk, ones_pair, cos_t, sin_t)


V_ROWS = C_V + V_PAD
C_HPS = 4


def _c_attn_kernel(q_ref, k_ref, vt_ref, o_ref, *scratch):
    i = pl.program_id(1)
    streams = tuple(
        (q_ref[:, h * LANES:(h + 1) * LANES], slice(h * LANES, (h + 1) * LANES),
         slice(h * V_ROWS, (h + 1) * V_ROWS), scratch[2 * h], scratch[2 * h + 1],
         scratch[2 * C_HPS + h])
        for h in range(C_HPS))
    accs = _attn_pipeline(i, k_ref, vt_ref, streams, V_ROWS)
    for pair in range(C_HPS // 2):
        a, b = accs[2 * pair], accs[2 * pair + 1]
        out = jnp.concatenate([a[:C_V] / a[C_V:C_V + 1], b[:C_V] / b[C_V:C_V + 1]], axis=0)
        o_ref[:, pair * LANES:(pair + 1) * LANES] = out.T.astype(BF16)


def _c_attn(q, k, vt):
    S = q.shape[0]
    once = pl.Buffered(1)
    return pl.pallas_call(
        _c_attn_kernel,
        grid=(C_HEADS // C_HPS, S // ATT_TQ),
        in_specs=[
            pl.BlockSpec((ATT_TQ, C_HPS * LANES), lambda h, i: (i, h)),
            pl.BlockSpec((S, C_HPS * LANES), lambda h, i: (0, h), pipeline_mode=once),
            pl.BlockSpec((C_HPS * V_ROWS, S), lambda h, i: (h, 0), pipeline_mode=once),
        ],
        out_specs=pl.BlockSpec((ATT_TQ, C_HPS * C_V), lambda h, i: (i, h)),
        out_shape=jax.ShapeDtypeStruct((S, C_HEADS * C_V), BF16),
        scratch_shapes=[_SCORE_BUF] * (2 * C_HPS) + [_QT_BUF] * C_HPS,
        compiler_params=_cparams(("arbitrary", "arbitrary")),
        name="c_mla_attn",
    )(q, k, vt)


def _c_out_router_kernel(o_ref, x_ref, wo_ref, g_ref, rhi_ref, rlo_ref, tri_ref,
                         x3_o, h_o, meta_o, cnt_o, run_sc):
    i = pl.program_id(0)

    @pl.when(i == 0)
    def _():
        run_sc[...] = jnp.zeros_like(run_sc)

    x3 = x_ref[...] + _dot(o_ref[...], wo_ref[...])
    x3_o[...] = x3
    h = _rms(x3, g_ref[...])
    for c in range(ROW_TILES):
        h_o[pl.ds(c, h.shape[0], stride=ROW_TILES), :] = h[:, c * LANES:(c + 1) * LANES]
    h_hi = h.astype(BF16)
    h_lo = (h - h_hi.astype(F32)).astype(BF16)
    logits = (_dot(h_hi, rhi_ref[...]) + _dot(h_hi, rlo_ref[...])) + _dot(h_lo, rhi_ref[...])
    tm = logits.shape[0]
    lane = lax.broadcasted_iota(jnp.int32, (tm, LANES), 1)
    lanef = lane.astype(F32)
    logits = jnp.where(lane < N_EXPERTS, logits, -jnp.inf)
    v1 = jnp.max(logits, axis=1, keepdims=True)
    i1 = jnp.min(jnp.where(logits == v1, lanef, float(LANES)), axis=1, keepdims=True)
    m1 = lanef == i1
    rest = jnp.where(m1, -jnp.inf, logits)
    v2 = jnp.max(rest, axis=1, keepdims=True)
    i2 = jnp.min(jnp.where(rest == v2, lanef, float(LANES)), axis=1, keepdims=True)
    m2 = lanef == i2
    e = jnp.exp(v2 - v1)
    g1 = 1.0 / (1.0 + e)
    g2 = e / (1.0 + e)
    chosen = jnp.where(m1 | m2, 1.0, 0.0)
    before = _dot(tri_ref[...], chosen.astype(BF16)) + run_sc[0:1, :]
    p1 = jnp.sum(jnp.where(m1, before, 0.0), axis=1, keepdims=True)
    p2 = jnp.sum(jnp.where(m2, before, 0.0), axis=1, keepdims=True)
    run_sc[...] = run_sc[...] + jnp.sum(chosen, axis=0, keepdims=True)
    meta = jnp.where(lane == 0, i1, 0.0)
    meta = jnp.where(lane == 1, i2, meta)
    meta = jnp.where(lane == 2, g1, meta)
    meta = jnp.where(lane == 3, g2, meta)
    meta = jnp.where(lane == 4, p1, meta)
    meta = jnp.where(lane == 5, p2, meta)
    meta_o[...] = meta
    cnt_o[...] = run_sc[...]


def _c_out_router(o, x2d, wo, g, rhi, rlo, tri, tm):
    S = x2d.shape[0]
    row = lambda i: (i, 0)
    const = lambda i: (0, 0)
    return pl.pallas_call(
        _c_out_router_kernel,
        grid=(S // tm,),
        in_specs=[pl.BlockSpec((tm, D_MODEL), row), pl.BlockSpec((tm, D_MODEL), row),
                  pl.BlockSpec((D_MODEL, D_MODEL), const), pl.BlockSpec((1, D_MODEL), const),
                  pl.BlockSpec((D_MODEL, LANES), const), pl.BlockSpec((D_MODEL, LANES), const),
                  pl.BlockSpec((tm, tm), const)],
        out_specs=[pl.BlockSpec((tm, D_MODEL), row),
                   pl.BlockSpec((tm * ROW_TILES, LANES), row),
                   pl.BlockSpec((tm, LANES), row), pl.BlockSpec((8, LANES), const)],
        out_shape=(jax.ShapeDtypeStruct((S, D_MODEL), F32),
                   jax.ShapeDtypeStruct((S * ROW_TILES, LANES), F32),
                   jax.ShapeDtypeStruct((S, LANES), F32), jax.ShapeDtypeStruct((8, LANES), F32)),
        scratch_shapes=[pltpu.VMEM((8, LANES), F32)],
        compiler_params=_cparams(("arbitrary",)),
        name="c_out_router",
    )(o, x2d, wo, g, rhi, rlo, tri)


MOE_TM = 256
PAD_PIECES = (128, 64, 32, 16, 8, 4, 2, 1)


def _dispatch_kernel(pad_ref, slot_ref, h_ref, xs_ref, zero_sc, sem, zsem, *, ts):
    i = pl.program_id(0)

    @pl.when(i == 0)
    def _():
        zero_sc[...] = jnp.zeros_like(zero_sc)
        for e in range(N_EXPERTS):
            start = pad_ref[0, e]
            npad = pad_ref[1, e]
            for p in PAD_PIECES:
                hit = (npad & p) != 0

                @pl.when(hit)
                def _(start=start, p=p):
                    cp = pltpu.make_async_copy(zero_sc.at[pl.ds(0, p)],
                                               xs_ref.at[pl.ds(start, p)], zsem)
                    cp.start()
                    cp.wait()

                start = start + jnp.where(hit, p, 0)

        half = MOE_TM // 2

        def zero_tile(tile, c):
            for part in range(2):
                cp = pltpu.make_async_copy(
                    zero_sc, xs_ref.at[pl.ds(tile * MOE_TM + part * half, half)], zsem)
                cp.start()
                cp.wait()
            return c

        lax.fori_loop(pad_ref[2, 0], xs_ref.shape[0] // MOE_TM, zero_tile, 0)

    def row_copy(r, k):
        return pltpu.make_async_copy(h_ref.at[r], xs_ref.at[slot_ref[0, 0, 2 * r + k]], sem)

    def issue(r, c):
        row_copy(r, 0).start(priority=0)
        row_copy(r, 1).start(priority=1)
        return c

    lax.fori_loop(0, ts, issue, 0, unroll=8)
    for _ in range(2):
        pltpu.make_async_copy(h_ref, xs_ref.at[pl.ds(0, ts)], sem).wait()


def _dispatch(pad_info, slots3, h2, n_slots, ts):
    S = h2.shape[0]
    return pl.pallas_call(
        functools.partial(_dispatch_kernel, ts=ts),
        grid=(S // ts,),
        in_specs=[pl.BlockSpec(memory_space=pltpu.SMEM),
                  pl.BlockSpec((1, 1, 2 * ts), lambda i: (i, 0, 0), memory_space=pltpu.SMEM),
                  pl.BlockSpec((ts, ROW_TILES, LANES), lambda i: (i, 0, 0))],
        out_specs=pl.BlockSpec(memory_space=pl.ANY),
        out_shape=jax.ShapeDtypeStruct((n_slots, ROW_TILES, LANES), F32),
        scratch_shapes=[pltpu.VMEM((MOE_TM // 2, ROW_TILES, LANES), F32),
                        pltpu.SemaphoreType.DMA(()), pltpu.SemaphoreType.DMA(())],
        compiler_params=_cparams(("arbitrary",)),
        name="moe_dispatch",
    )(pad_info, slots3, h2)


MOE_FC = 512


N_FC = D_FF_EXPERT // MOE_FC
MOE_WR = 64
MOE_STAGES = 4


def _moe_kernel(te_ref, tv_ref, xs_ref, wgu_ref, wdn_ref, y_ref, x_sc, act_sc, wg_sc, wu_sc,
                wd_sc, stage_in, stage_out, sem):
    t = pl.program_id(0)
    e = te_ref[t]
    valid = tv_ref[t] != 0

    @pl.when(valid & ((t == 0) | (e != te_ref[jnp.maximum(t - 1, 0)])))
    def _():
        n_in = D_MODEL // MOE_WR
        n_chunks = n_in + N_FC
        ahead = MOE_STAGES - 1

        def copy(j):
            if j < n_in:
                src, stage = wgu_ref.at[e, pl.ds(j * MOE_WR, MOE_WR), :], stage_in
            else:
                src, stage = wdn_ref.at[e, pl.ds((j - n_in) * MOE_FC, MOE_FC), :], stage_out
            return pltpu.make_async_copy(src, stage.at[j % MOE_STAGES], sem.at[j % MOE_STAGES])

        for j in range(ahead):
            copy(j).start()
        for j in range(n_chunks):
            if j + ahead < n_chunks:
                copy(j + ahead).start()
            copy(j).wait()
            slot = j % MOE_STAGES
            if j < n_in:
                rows = pl.ds(j * MOE_WR, MOE_WR)
                wg_sc[rows, :] = stage_in[slot, :, :D_FF_EXPERT].astype(BF16)
                wu_sc[rows, :] = stage_in[slot, :, D_FF_EXPERT:].astype(BF16)
            else:
                rows = pl.ds((j - n_in) * MOE_FC, MOE_FC)
                wd_sc[rows, :] = stage_out[slot].astype(BF16)

    @pl.when(jnp.logical_not(valid))
    def _():
        y_ref[...] = jnp.zeros_like(y_ref)

    @pl.when(valid)
    def _():
        for c in range(ROW_TILES):
            chunk = xs_ref[pl.ds(c, MOE_TM, stride=ROW_TILES), :]
            x_sc[:, c * LANES:(c + 1) * LANES] = chunk.astype(BF16)
        x = x_sc[...]
        for c in range(N_FC):
            sl = slice(c * MOE_FC, (c + 1) * MOE_FC)
            gate = _dot(x, wg_sc[:, sl])
            up = _dot(x, wu_sc[:, sl])
            act_sc[:, sl] = (gate * jax.nn.sigmoid(gate) * up).astype(BF16)
        acc = _dot(act_sc[...], wd_sc[...])
        for c in range(ROW_TILES):
            y_ref[pl.ds(c, MOE_TM, stride=ROW_TILES), :] = acc[:, c * LANES:(c + 1) * LANES]


def _moe(tile_expert, tile_valid, xs, w_gu, w_down):
    n_slots = xs.shape[0]
    nt = n_slots // MOE_TM
    grid_spec = pltpu.PrefetchScalarGridSpec(
        num_scalar_prefetch=2,
        grid=(nt,),
        in_specs=[
            pl.BlockSpec((MOE_TM * ROW_TILES, LANES), lambda t, te, tv: (t, 0)),
            pl.BlockSpec(memory_space=pl.ANY),
            pl.BlockSpec(memory_space=pl.ANY),
        ],
        out_specs=pl.BlockSpec((MOE_TM * ROW_TILES, LANES), lambda t, te, tv: (t, 0)),
        scratch_shapes=[pltpu.VMEM((MOE_TM, D_MODEL), BF16),
                        pltpu.VMEM((MOE_TM, D_FF_EXPERT), BF16),
                        pltpu.VMEM((D_MODEL, D_FF_EXPERT), BF16),
                        pltpu.VMEM((D_MODEL, D_FF_EXPERT), BF16),
                        pltpu.VMEM((D_FF_EXPERT, D_MODEL), BF16),
                        pltpu.VMEM((MOE_STAGES, MOE_WR, 2 * D_FF_EXPERT), F32),
                        pltpu.VMEM((MOE_STAGES, MOE_FC, D_MODEL), F32),
                        pltpu.SemaphoreType.DMA((MOE_STAGES,))],
    )
    y = pl.pallas_call(
        _moe_kernel,
        grid_spec=grid_spec,
        out_shape=jax.ShapeDtypeStruct((n_slots * ROW_TILES, LANES), F32),
        compiler_params=_cparams(("arbitrary",)),
        name="moe_experts",
    )(tile_expert, tile_valid, xs.reshape(n_slots * ROW_TILES, LANES), w_gu, w_down)
    return y.reshape(n_slots, ROW_TILES, LANES)


def _combine_kernel(slot_ref, next_slot_ref, x3_ref, meta_ref, y_ref, y2_ref, o_ref, buf, sem,
                    *, ts):
    i = pl.program_id(0)
    n = pl.num_programs(0)
    cur = lax.rem(i, 2)

    def gather(slots, half):
        def issue(r, c):
            for k in range(2):
                dst = buf.at[half, k, pl.ds(pl.multiple_of(r * ROW_TILES, ROW_TILES), ROW_TILES)]
                pltpu.make_async_copy(y_ref.at[slots[0, 0, 2 * r + k]], dst,
                                      sem.at[half]).start(priority=k)
            return c

        lax.fori_loop(0, ts, issue, 0, unroll=8)

    @pl.when(i == 0)
    def _():
        gather(slot_ref, 0)

    @pl.when(i + 1 < n)
    def _():
        gather(next_slot_ref, 1 - cur)

    for k in range(2):
        pltpu.make_async_copy(y2_ref.at[pl.ds(0, ts * ROW_TILES)], buf.at[cur, k],
                              sem.at[cur]).wait()
    meta = meta_ref[...]
    g1 = jnp.broadcast_to(meta[:, 2:3], (ts, LANES))
    g2 = jnp.broadcast_to(meta[:, 3:4], (ts, LANES))
    for c in range(ROW_TILES):
        sl = slice(c * LANES, (c + 1) * LANES)
        rows = pl.ds(c, ts, stride=ROW_TILES)
        o_ref[:, sl] = x3_ref[:, sl] + g1 * buf[cur, 0, rows, :] + g2 * buf[cur, 1, rows, :]


def _combine(slots3, x3, meta, y, ts):
    S = x3.shape[0]
    n = S // ts
    return pl.pallas_call(
        functools.partial(_combine_kernel, ts=ts),
        grid=(n,),
        in_specs=[pl.BlockSpec((1, 1, 2 * ts), lambda i: (i, 0, 0), memory_space=pltpu.SMEM),
                  pl.BlockSpec((1, 1, 2 * ts), lambda i: (jnp.minimum(i + 1, n - 1), 0, 0),
                               memory_space=pltpu.SMEM),
                  pl.BlockSpec((ts, D_MODEL), lambda i: (i, 0)),
                  pl.BlockSpec((ts, LANES), lambda i: (i, 0)),
                  pl.BlockSpec(memory_space=pl.ANY), pl.BlockSpec(memory_space=pl.ANY)],
        out_specs=pl.BlockSpec((ts, D_MODEL), lambda i: (i, 0)),
        out_shape=jax.ShapeDtypeStruct((S, D_MODEL), F32),
        scratch_shapes=[pltpu.VMEM((2, 2, ts * ROW_TILES, LANES), F32),
                        pltpu.SemaphoreType.DMA((2,))],
        compiler_params=_cparams(("arbitrary",)),
        name="moe_combine",
    )(slots3, slots3, x3, meta, y, y.reshape(-1, LANES))


def _tile_lanes(v, reps):
    return jnp.tile(v, reps)[None, :].astype(F32)


def _rope_cos_sin(S, dim):
    pos = np.arange(S, dtype=np.float64)
    inv = ROPE_THETA ** (-np.arange(0, dim, 2, dtype=np.float64) / dim)
    ang = pos[:, None] * inv[None, :]
    return np.cos(ang), np.sin(ang)


def _rope_tables_64(S):
    c, s = _rope_cos_sin(S, A_HEAD_DIM)
    return (jnp.asarray(np.concatenate([c, c, c, c], axis=1), F32),
            jnp.asarray(np.concatenate([-s, s, -s, s], axis=1), F32))


def _rope_tables_32(S):
    c, s = _rope_cos_sin(S, C_ROPE)
    one = np.ones((S, C_NOPE))
    zero = np.zeros((S, C_NOPE))
    pad = np.zeros((S, LANES - C_QK))
    return (jnp.asarray(np.concatenate([one, c, c, pad], axis=1), F32),
            jnp.asarray(np.concatenate([zero, -s, s, pad], axis=1), F32))


def _pad_heads(w, n_heads, width):
    k = w.shape[0]
    w = w.reshape(k, n_heads, width)
    return jnp.pad(w, ((0, 0), (0, 0), (0, LANES - width))).reshape(k, n_heads * LANES)


def _swap_rope_cols(w_p):
    k = w_p.shape[0]
    w = w_p.reshape(k, -1, LANES)
    half = C_ROPE // 2
    sw = jnp.concatenate([jnp.zeros_like(w[:, :, :C_NOPE]),
                          w[:, :, C_NOPE + half:C_QK], w[:, :, C_NOPE:C_NOPE + half],
                          jnp.zeros_like(w[:, :, C_QK:])], axis=2)
    return sw.reshape(k, -1)


def kernel(x, l0_norm_attn, l0_w_in, l0_a_q_norm, l0_a_k_norm, l0_a_lambda, l0_a_subln, l0_b_q_norm, l0_b_k_norm, l0_b_sinks, l0_w_out, l0_norm_ffn, l0_ffn_w_gu, l0_ffn_w_down, l1_norm_attn, l1_c_w_in, l1_c_q_lora_norm, l1_c_kv_lora_norm, l1_c_w_uq, l1_c_w_ukv, l1_c_q_norm, l1_c_k_norm, l1_c_w_out, l1_norm_ffn, l1_router, l1_exp_w_gu, l1_exp_w_down):
    B, S, _ = x.shape
    assert B == 1
    x2d = x.reshape(S, D_MODEL)
    tm = min(512, S)

    o_bk = 3 * A_W + B_QW
    o_bv = o_bk + B_KW
    bk_w = l0_w_in[:, o_bk:o_bv].reshape(D_MODEL, B_KV_HEADS, B_HEAD_DIM)
    bv_w = l0_w_in[:, o_bv:].reshape(D_MODEL, B_KV_HEADS, B_HEAD_DIM)
    dup = lambda w: jnp.concatenate([w, w], axis=2).reshape(D_MODEL, 2 * B_KW)
    w_in_p = jnp.concatenate([l0_w_in[:, :2 * A_W], l0_w_in[:, 3 * A_W:o_bk], dup(bk_w),
                              dup(bv_w)], axis=1).astype(BF16)
    w_avt = l0_w_in[:, 2 * A_W:3 * A_W].T.reshape(A_HEADS, 2 * A_HEAD_DIM, D_MODEL)
    w_avt = jnp.pad(w_avt, ((0, 0), (0, V_PAD), (0, 0))).reshape(-1, D_MODEL).astype(BF16)
    a_vone = jnp.zeros((A_HEADS, A_V_ROWS, 1), F32).at[:, 2 * A_HEAD_DIM, 0].set(1.0).reshape(-1, 1)
    gains = jnp.concatenate([_tile_lanes(l0_a_q_norm, 2), _tile_lanes(l0_a_k_norm, 2),
                             _tile_lanes(l0_b_q_norm, 2), _tile_lanes(l0_b_k_norm, 2),
                             jnp.zeros((4, LANES), F32)], axis=0)
    lane = jnp.arange(LANES)
    ones64 = (lane[:, None] // 64 == lane[None, :] // 64).astype(BF16)
    cos64, sin64 = _rope_tables_64(S)
    aq, ak, avt, bq, bk, bv = _l0_in(x2d, l0_norm_attn[None, :], w_in_p, w_avt, a_vone, gains,
                                     ones64, cos64, sin64, tm)
    lam_init = 0.8 - 0.6 * math.exp(-0.3 * 0)
    a_out = _a_attn(l0_a_lambda.astype(F32), aq, ak, avt, l0_a_subln[None, :].astype(F32),
                    lam_init)
    b_out = _b_attn(l0_b_sinks.astype(F32), bq, bk, bv)
    w_out = l0_w_out.astype(BF16)
    x2 = _l0_ffn(a_out, b_out, x2d, w_out[:A_W], w_out[A_W:], l0_norm_ffn[None, :],
                 l0_ffn_w_gu.astype(BF16), l0_ffn_w_down.astype(BF16), tm, D_FF // 2)

    wq = l1_c_w_in[:, :C_Q_RANK].astype(BF16)
    wkv = l1_c_w_in[:, C_Q_RANK:C_Q_RANK + C_KV_RANK].astype(BF16)
    wkr = jnp.pad(l1_c_w_in[:, C_Q_RANK + C_KV_RANK:], ((0, 0), (C_NOPE, LANES - C_QK)))
    wkrs = _swap_rope_cols(wkr).astype(BF16)
    wkr = wkr.astype(BF16)
    wuq = _pad_heads(l1_c_w_uq, C_HEADS, C_QK)
    wuqs = _swap_rope_cols(wuq).astype(BF16)
    wuq = wuq.astype(BF16)
    ukv = l1_c_w_ukv.reshape(C_KV_RANK, C_HEADS, C_NOPE + C_V)
    wuk = _pad_heads(ukv[:, :, :C_NOPE].reshape(C_KV_RANK, -1), C_HEADS, C_NOPE).astype(BF16)
    wuv = jnp.pad(jnp.transpose(ukv[:, :, C_NOPE:], (1, 2, 0)), ((0, 0), (0, V_PAD), (0, 0)))
    wuv = wuv.reshape(C_HEADS * V_ROWS, C_KV_RANK).astype(BF16)
    vone = jnp.zeros((C_HEADS, V_ROWS, 1), F32).at[:, C_V, 0].set(1.0).reshape(-1, 1)
    pad_gain = lambda gvec: jnp.pad(gvec.astype(F32), (0, LANES - C_QK))[None, :]
    qg, kg = pad_gain(l1_c_q_norm), pad_gain(l1_c_k_norm)
    qgs, kgs = _swap_rope_cols(qg), _swap_rope_cols(kg)
    ones128 = jnp.ones((LANES, LANES), BF16)
    lane2 = jnp.arange(2 * LANES)
    ones_pair = (lane2[:, None] // LANES == lane2[None, :] // LANES).astype(BF16)
    cos32, sin32 = _rope_tables_32(S)
    cq, ck, cvt = _c_in(x2, l1_norm_attn[None, :], wq, wkv, wkr, wkrs,
                       l1_c_q_lora_norm[None, :], l1_c_kv_lora_norm[None, :],
                       wuq, wuqs, wuk, wuv, qg, qgs, kg, kgs, vone, ones128, ones_pair, cos32,
                       sin32, tm)
    c_o = _c_attn(cq, ck, cvt)

    r_pad = jnp.pad(l1_router.astype(F32), ((0, 0), (0, LANES - N_EXPERTS)))
    r_hi = r_pad.astype(BF16)
    r_lo = (r_pad - r_hi.astype(F32)).astype(BF16)
    tr = min(256, S)
    ridx = jnp.arange(tr)
    tri = (ridx[None, :] < ridx[:, None]).astype(BF16)
    x3, h2, meta, cnt = _c_out_router(c_o, x2, l1_c_w_out.astype(BF16), l1_norm_ffn[None, :],
                                      r_hi, r_lo, tri, tr)

    counts = cnt[0, :N_EXPERTS].astype(jnp.int32)
    padded = ((counts + MOE_TM - 1) // MOE_TM) * MOE_TM
    ends = jnp.cumsum(padded)
    offs = ends - padded
    idx = meta[:, 0:2].astype(jnp.int32)
    pos = meta[:, 4:6].astype(jnp.int32)
    slots = offs[idx] + pos
    n_slots = 2 * S + N_EXPERTS * MOE_TM
    nt = n_slots // MOE_TM
    tile_start = jnp.arange(nt, dtype=jnp.int32) * MOE_TM
    n_valid = jnp.broadcast_to(ends[-1] // MOE_TM, (N_EXPERTS,))
    tile_valid = (tile_start < ends[-1]).astype(jnp.int32)
    tile_expert = jnp.minimum(
        jnp.sum((tile_start[:, None] >= ends[None, :]).astype(jnp.int32), axis=1),
        N_EXPERTS - 1).astype(jnp.int32)
    pad_info = jnp.stack([offs + counts, padded - counts, n_valid]).astype(jnp.int32)

    ts = min(512, S)
    slots3 = slots.reshape(S // ts, 1, 2 * ts)
    xs = _dispatch(pad_info, slots3, h2.reshape(S, ROW_TILES, LANES), n_slots, ts)
    y = _moe(tile_expert, tile_valid, xs,
             l1_exp_w_gu.astype(F32), l1_exp_w_down.astype(F32))
    out = _combine(slots3, x3, meta, y, ts)
    return out.reshape(B, S, D_MODEL)
```
